```python
import jax, jax.numpy as jnp
from jax import lax
import numpy as np

D_MODEL = 2048
BATCH = 4
SEQ = 2048
DEPTH = 1
DEC_BATCH = 128
DEC_SEQ = 4
PAST_LEN = 16384
PAGE_SIZE = 128

D_GLA = D_MODEL // 2
D_RWKV = D_MODEL - D_GLA
GLA_HEADS = 4
GLA_DV = D_GLA // GLA_HEADS
GLA_DK = GLA_DV // 2
GLA_K = GLA_HEADS * GLA_DK
GLA_GATE_RANK = 16
GLA_GATE_TEMP = 16.0
GLA_CHUNK = 16
RWKV_HEAD = 64
RWKV_HEADS = D_RWKV // RWKV_HEAD
RWKV_DECAY_RANK = 64
RWKV_A_RANK = 64
RWKV_G_RANK = 160
N_GROUPS = 8
EXPERTS_PER_GROUP = 8
TOP_K_IN_GROUP = 2
D_EXPERT = 512
NORM_EPS = 1e-6
GN_EPS = 64e-5
RWKV_OFF = 2 * GLA_K + 2 * D_GLA
D_IN = RWKV_OFF + 3 * D_RWKV

kernel_name = 'hymba_gla_rwkv7_hier_moe_step'


def _rmsnorm(x, gain):
    xf = x.astype(jnp.float32)
    y = xf * lax.rsqrt(jnp.mean(xf * xf, -1, keepdims=True) + NORM_EPS)
    return (y * gain.astype(jnp.float32)).astype(x.dtype)


def _gla_chunked(q, k, v, log_a, s0):
    B, H, T, DK = q.shape
    DV = v.shape[-1]
    C = GLA_CHUNK
    pad = (-T) % C
    if pad:
        pw = ((0, 0), (0, 0), (0, pad), (0, 0))
        q, k, v, log_a = [jnp.pad(t, pw) for t in (q, k, v, log_a)]
    n = (T + pad) // C
    q, k, v, log_a = [t.reshape(B, H, n, C, t.shape[-1]) for t in (q, k, v, log_a)]
    b = jnp.cumsum(log_a, axis=3)
    b_last = b[:, :, :, -1:]
    causal = jnp.tril(jnp.ones((C, C), bool))[:, :, None]
    rel = b[:, :, :, :, None, :] - b[:, :, :, None, :, :]
    dec = jnp.exp(jnp.where(causal, rel, -jnp.inf))
    scores = jnp.einsum('bhntd,bhnsd,bhntsd->bhnts', q, k, dec)
    o_intra = jnp.einsum('bhnts,bhnsv->bhntv', scores, v)
    q_dec = q * jnp.exp(b)
    k_dec = k * jnp.exp(b_last - b)
    a_last = jnp.exp(b_last[:, :, :, 0])

    def step(S, xs):
        qd, kd, vv, al = xs
        o = jnp.einsum('bhtd,bhdv->bhtv', qd, S)
        S = al[..., None] * S + jnp.einsum('bhtd,bhtv->bhdv', kd, vv)
        return S, o

    xs = (jnp.moveaxis(q_dec, 2, 0), jnp.moveaxis(k_dec, 2, 0),
          jnp.moveaxis(v, 2, 0), jnp.moveaxis(a_last, 2, 0))
    s_fin, o_inter = lax.scan(step, s0, xs)
    o = o_intra + jnp.moveaxis(o_inter, 0, 2)
    o = o.reshape(B, H, n * C, DV)[:, :, :T]
    return o, s_fin


def _wkv7_scan(r, decay, k, v, a_vec, b_vec, s0):
    def step(S, xs):
        rt, wt, kt, vt, at, bt = xs
        sa = jnp.einsum('bhij,bhj->bhi', S, at)
        S = S * wt[:, :, None, :] + sa[..., None] * bt[:, :, None, :] + vt[..., None] * kt[:, :, None, :]
        y = jnp.einsum('bhij,bhj->bhi', S, rt)
        return S, y

    xs = tuple(jnp.moveaxis(t, 1, 0) for t in (r, decay, k, v, a_vec, b_vec))
    s_fin, y = lax.scan(step, s0, xs)
    return jnp.moveaxis(y, 0, 1), s_fin


def _mixer(h, h_last, s_gla, s_wkv, w_in, gla_gate_w1, gla_gate_w2, gla_gate_b, gla_norm,
           mu_rkv, mu_wag, w0, w1, w2, a0, a1, a2, g1, g2, k_k, k_a, r_k, ln_w, ln_b, w_out):
    B, T, _ = h.shape
    f32 = jnp.float32
    z = h @ w_in
    q, k, v, g, zr = jnp.split(z, [GLA_K, 2 * GLA_K, 2 * GLA_K + D_GLA, RWKV_OFF], axis=-1)

    def gheads(t, d):
        return t.reshape(B, T, GLA_HEADS, d).transpose(0, 2, 1, 3).astype(f32)
    log_a = jax.nn.log_sigmoid(((h @ gla_gate_w1) @ gla_gate_w2 + gla_gate_b).astype(f32)) / GLA_GATE_TEMP
    o_g, s_gla_new = _gla_chunked(gheads(q, GLA_DK) * (GLA_DK ** -0.5), gheads(k, GLA_DK),
                                  gheads(v, GLA_DV), gheads(log_a, GLA_DK), s_gla.astype(f32))
    o_g = o_g.transpose(0, 2, 1, 3)
    o_g = o_g * lax.rsqrt(jnp.mean(o_g * o_g, -1, keepdims=True) + NORM_EPS)
    o_g = o_g.reshape(B, T, D_GLA) * gla_norm.astype(f32) * jax.nn.silu(g.astype(f32))

    h_last = h_last.astype(h.dtype)
    h_shift = jnp.concatenate([h_last[:, None], h[:, :-1]], axis=1)
    zr_prev = jnp.concatenate([(h_last @ w_in[:, RWKV_OFF:])[:, None], zr[:, :-1]], axis=1)
    zr = zr + mu_rkv.reshape(-1) * (zr_prev - zr)
    r, kr, vr = jnp.split(zr, 3, axis=-1)
    dx = h_shift - h
    xw = h + dx * mu_wag[0]
    xa = h + dx * mu_wag[1]
    xg = h + dx * mu_wag[2]
    w_log = -jax.nn.softplus(-(w0 + jnp.tanh(xw @ w1) @ w2).astype(f32)) - 0.5
    decay = jnp.exp(-jnp.exp(w_log))
    a = jax.nn.sigmoid((a0 + (xa @ a1) @ a2).astype(f32))
    gate = (jax.nn.sigmoid(xg @ g1) @ g2).astype(f32)

    def rheads(t):
        return t.reshape(B, T, RWKV_HEADS, RWKV_HEAD).astype(f32)
    r_h, k_h, v_h, a_h, d_h = rheads(r), rheads(kr), rheads(vr), rheads(a), rheads(decay)
    kk = k_h * rheads(k_k.astype(f32) * jnp.ones((B, T, D_RWKV), f32))
    kk = kk / jnp.maximum(jnp.sqrt(jnp.sum(kk * kk, -1, keepdims=True)), 1e-12)
    k_h = k_h * (1.0 + (a_h - 1.0) * k_a.astype(f32).reshape(RWKV_HEADS, RWKV_HEAD))
    y, s_wkv_new = _wkv7_scan(r_h, d_h, k_h, v_h, -kk, kk * a_h, s_wkv.astype(f32))
    mu = jnp.mean(y, -1, keepdims=True)
    var = jnp.mean(jnp.square(y - mu), -1, keepdims=True)
    yn = ((y - mu) * lax.rsqrt(var + GN_EPS)).reshape(B, T, D_RWKV) * ln_w.astype(f32) + ln_b.astype(f32)
    bonus = (jnp.sum(r_h * k_h * r_k.astype(f32), -1, keepdims=True) * v_h).reshape(B, T, D_RWKV)
    o_r = (yn + bonus) * gate

    out = jnp.concatenate([o_g, o_r], axis=-1).astype(h.dtype) @ w_out
    return out, s_gla_new.astype(h.dtype), s_wkv_new.astype(h.dtype), h[:, -1]


def _hier_moe(h, rg_w, rg_b, re_w, re_b, w_gate, w_up, w_down):
    B, T, D = h.shape
    f32 = jnp.float32
    M = B * T
    x = h.reshape(M, D)
    lg = (x @ rg_w).astype(f32) + rg_b.astype(f32)
    pg = jax.nn.softmax(lg, -1)
    g_idx = jnp.argmax(lg, -1)
    p_group = jnp.take_along_axis(pg, g_idx[:, None], 1)
    le = ((x @ re_w).astype(f32) + re_b.astype(f32)).reshape(M, N_GROUPS, EXPERTS_PER_GROUP)
    le = jnp.take_along_axis(le, g_idx[:, None, None], 1)[:, 0]
    pe = jax.nn.softmax(le, -1)
    top_p, top_i = lax.top_k(pe, TOP_K_IN_GROUP)
    top_p = top_p / jnp.sum(top_p, -1, keepdims=True)
    w_e = jnp.einsum('mk,mke->me', top_p, jax.nn.one_hot(top_i, EXPERTS_PER_GROUP, dtype=f32)) * p_group
    w_ge = jax.nn.one_hot(g_idx, N_GROUPS, dtype=f32)[:, :, None] * w_e[:, None, :]
    out = jnp.zeros((M, D), f32)
    for gi in range(N_GROUPS):
        hid = jax.nn.silu(jnp.einsum('md,edf->mef', x, w_gate[gi])) * jnp.einsum('md,edf->mef', x, w_up[gi])
        hid = hid * w_ge[:, gi, :, None].astype(hid.dtype)
        out = out + jnp.einsum('mef,efd->md', hid, w_down[gi]).astype(f32)
    return out.astype(h.dtype).reshape(B, T, D)


def setup_inputs(seed: int = 0) -> dict:
    key = jax.random.key(seed)
    ks = iter(jax.random.split(key, 48))
    f32 = jnp.float32
    L, D = DEPTH, D_MODEL

    def nrm(shape, scale):
        return jax.random.normal(next(ks), shape, f32) * scale

    def uni(shape, lo, hi):
        return jax.random.uniform(next(ks), shape, f32, lo, hi)

    return {
        'x_prompt': nrm((BATCH, SEQ, D), 1.0),
        'x_sample': nrm((DEC_BATCH, DEC_SEQ, D), 1.0),
        'state_gla': nrm((L, DEC_BATCH, GLA_HEADS, GLA_DK, GLA_DV), 0.5),
        'state_wkv': nrm((L, DEC_BATCH, RWKV_HEADS, RWKV_HEAD, RWKV_HEAD), 0.5),
        'state_shift': nrm((L, DEC_BATCH, D), 1.0),
        'mix_norm': 1.0 + nrm((L, D), 0.02),
        'w_in': nrm((L, D, D_IN), D ** -0.5),
        'gla_gate_w1': nrm((L, D, GLA_GATE_RANK), D ** -0.5),
        'gla_gate_w2': nrm((L, GLA_GATE_RANK, GLA_K), GLA_GATE_RANK ** -0.5),
        'gla_gate_b': nrm((L, GLA_K), 0.1),
        'gla_norm': 1.0 + nrm((L, D_GLA), 0.02),
        'rwkv_mu_rkv': uni((L, 3, D_RWKV), 0.0, 1.0),
        'rwkv_mu_wag': uni((L, 3, D), 0.0, 1.0),
        'rwkv_w0': uni((L, D_RWKV), -5.0, -1.0),
        'rwkv_w1': nrm((L, D, RWKV_DECAY_RANK), D ** -0.5),
        'rwkv_w2': nrm((L, RWKV_DECAY_RANK, D_RWKV), 0.1 * RWKV_DECAY_RANK ** -0.5),
        'rwkv_a0': nrm((L, D_RWKV), 0.1),
        'rwkv_a1': nrm((L, D, RWKV_A_RANK), D ** -0.5),
        'rwkv_a2': nrm((L, RWKV_A_RANK, D_RWKV), RWKV_A_RANK ** -0.5),
        'rwkv_g1': nrm((L, D, RWKV_G_RANK), D ** -0.5),
        'rwkv_g2': nrm((L, RWKV_G_RANK, D_RWKV), RWKV_G_RANK ** -0.5),
        'rwkv_k_k': 0.85 + nrm((L, D_RWKV), 0.02),
        'rwkv_k_a': 1.0 + nrm((L, D_RWKV), 0.02),
        'rwkv_r_k': nrm((L, RWKV_HEADS, RWKV_HEAD), 0.1),
        'rwkv_ln_w': 1.0 + nrm((L, D_RWKV), 0.02),
        'rwkv_ln_b': nrm((L, D_RWKV), 0.02),
        'w_out': nrm((L, D, D), D ** -0.5),
        'ffn_norm': 1.0 + nrm((L, D), 0.02),
        'router_group_w': nrm((L, D, N_GROUPS), D ** -0.5),
        'router_group_b': nrm((L, N_GROUPS), 0.01),
        'router_expert_w': nrm((L, D, N_GROUPS * EXPERTS_PER_GROUP), D ** -0.5),
        'router_expert_b': nrm((L, N_GROUPS * EXPERTS_PER_GROUP), 0.01),
        'expert_w_gate': nrm((L, N_GROUPS, EXPERTS_PER_GROUP, D, D_EXPERT), D ** -0.5),
        'expert_w_up': nrm((L, N_GROUPS, EXPERTS_PER_GROUP, D, D_EXPERT), D ** -0.5),
        'expert_w_down': nrm((L, N_GROUPS, EXPERTS_PER_GROUP, D_EXPERT, D), D_EXPERT ** -0.5),
        'final_norm': 1.0 + nrm((D,), 0.02),
    }


def reference(x_prompt, x_sample, state_gla, state_wkv, state_shift, mix_norm, w_in,
              gla_gate_w1, gla_gate_w2, gla_gate_b, gla_norm, rwkv_mu_rkv, rwkv_mu_wag,
              rwkv_w0, rwkv_w1, rwkv_w2, rwkv_a0, rwkv_a1, rwkv_a2, rwkv_g1, rwkv_g2,
              rwkv_k_k, rwkv_k_a, rwkv_r_k, rwkv_ln_w, rwkv_ln_b, w_out, ffn_norm,
              router_group_w, router_group_b, router_expert_w, router_expert_b,
              expert_w_gate, expert_w_up, expert_w_down, final_norm):
    def run(x, s_gla_all, s_wkv_all, s_shift_all):
        new_gla, new_wkv, new_shift = [], [], []
        for l in range(DEPTH):
            h = _rmsnorm(x, mix_norm[l])
            mix, sg, sw, sh = _mixer(h, s_shift_all[l], s_gla_all[l], s_wkv_all[l], w_in[l],
                                     gla_gate_w1[l], gla_gate_w2[l], gla_gate_b[l], gla_norm[l],
                                     rwkv_mu_rkv[l], rwkv_mu_wag[l], rwkv_w0[l], rwkv_w1[l], rwkv_w2[l],
                                     rwkv_a0[l], rwkv_a1[l], rwkv_a2[l], rwkv_g1[l], rwkv_g2[l],
                                     rwkv_k_k[l], rwkv_k_a[l], rwkv_r_k[l], rwkv_ln_w[l], rwkv_ln_b[l],
                                     w_out[l])
            x = x + mix
            x = x + _hier_moe(_rmsnorm(x, ffn_norm[l]), router_group_w[l], router_group_b[l],
                              router_expert_w[l], router_expert_b[l], expert_w_gate[l],
                              expert_w_up[l], expert_w_down[l])
            new_gla.append(sg)
            new_wkv.append(sw)
            new_shift.append(sh)
        return _rmsnorm(x, final_norm), jnp.stack(new_gla), jnp.stack(new_wkv), jnp.stack(new_shift)

    bp = x_prompt.shape[0]
    dt = x_prompt.dtype
    zg = jnp.zeros((DEPTH, bp, GLA_HEADS, GLA_DK, GLA_DV), dt)
    zw = jnp.zeros((DEPTH, bp, RWKV_HEADS, RWKV_HEAD, RWKV_HEAD), dt)
    zs = jnp.zeros((DEPTH, bp, D_MODEL), dt)
    y_prompt, gla_p, wkv_p, shift_p = run(x_prompt, zg, zw, zs)
    y_sample, gla_s, wkv_s, shift_s = run(x_sample, state_gla, state_wkv, state_shift)
    return (y_prompt, y_sample, gla_p, wkv_p, shift_p, gla_s, wkv_s, shift_s)
```

```python
import functools

import jax
import jax.numpy as jnp
from jax import lax
from jax.experimental import pallas as pl
from jax.experimental.pallas import tpu as pltpu

F32 = jnp.float32
BF16 = jnp.bfloat16

D_MODEL = 2048
D_GLA = 1024
D_RWKV = 1024
GLA_HEADS = 4
GLA_DV = 256
GLA_DK = 128
GLA_K = 512
GLA_GATE_TEMP = 16.0
RWKV_HEAD = 64
RWKV_HEADS = 16
N_GROUPS = 8
EXPERTS_PER_GROUP = 8
D_EXPERT = 512
NORM_EPS = 1e-6
GN_EPS = 64e-5
RWKV_OFF = 2 * GLA_K + 2 * D_GLA
D_IN = RWKV_OFF + 3 * D_RWKV

LANE = 128
MXU_TILE = 256
VMEM_LIMIT = 56 * 1024 * 1024

GLA_CHUNK = 16
WKV_CHUNK = 8
MOE_TILE = 512
ROUTE_LANES = 128
X1_WIDTH = D_MODEL + ROUTE_LANES


def _cparams(sem):
    return pltpu.CompilerParams(dimension_semantics=sem, vmem_limit_bytes=VMEM_LIMIT)


def _rms(x, gain):
    return x * lax.rsqrt(jnp.mean(x * x, axis=-1, keepdims=True) + NORM_EPS) * gain


def _softplus(y):
    return jnp.maximum(y, 0.0) + jnp.log1p(jnp.exp(-jnp.abs(y)))


def _dot(a, b):
    return jnp.dot(a.astype(BF16), b.astype(BF16), preferred_element_type=F32)


def _dot_nt(a, b):
    return lax.dot_general(a.astype(BF16), b.astype(BF16), (((1,), (1,)), ((), ())),
                           preferred_element_type=F32)


def _dot_tn(a, b):
    return lax.dot_general(a.astype(BF16), b.astype(BF16), (((0,), (0,)), ((), ())),
                           preferred_element_type=F32)


def _norm_proj_kernel(x_ref, hprev_ref, gain_ref, mu_ref, gw1_ref, gw2_ref, gb_ref,
                      w1_ref, w2_ref, w0_ref, a1_ref, a2_ref, a0_ref, g1_ref, g2_ref,
                      h_ref, la_ref, lw_ref, a_ref, gate_ref, hlast_ref, carry_ref,
                      *, tm, seq, multi_seq):
    i = pl.program_id(0)
    h = _rms(x_ref[...], gain_ref[...])
    row = lax.broadcasted_iota(jnp.int32, (tm, 1), 0)
    rolled = pltpu.roll(h, 1, 0)
    if multi_seq:
        h_shift = jnp.where(row % seq == 0, hprev_ref[...], rolled)
        hlast_ref[...] = h
    else:
        tiles_per_seq = seq // tm
        first = (i % tiles_per_seq) == 0
        prev = jnp.where(first, hprev_ref[0], carry_ref[...])
        h_shift = jnp.where(row == 0, prev, rolled)
        carry_ref[...] = h[tm - 1:tm]
        hlast_ref[0] = h[tm - 1:tm]
    h_ref[...] = h.astype(BF16)

    dx = h_shift - h
    mu = mu_ref[...]
    xw = h + dx * mu[0:1]
    xa = h + dx * mu[1:2]
    xg = h + dx * mu[2:3]

    gl = _dot(_dot(h, gw1_ref[...]), gw2_ref[...]) + gb_ref[...]
    la_ref[...] = -_softplus(-gl) * (1.0 / GLA_GATE_TEMP)

    u = w0_ref[...] + _dot(jnp.tanh(_dot(xw, w1_ref[...])), w2_ref[...])
    w_log = -_softplus(-u) - 0.5
    lw_ref[...] = -jnp.exp(w_log)

    a_ref[...] = jax.nn.sigmoid(a0_ref[...] + _dot(_dot(xa, a1_ref[...]), a2_ref[...]))
    gate_ref[...] = _dot(jax.nn.sigmoid(_dot(xg, g1_ref[...])), g2_ref[...])


def _pad_axis(w, axis, to):
    pad = [(0, 0)] * w.ndim
    pad[axis] = (0, to - w.shape[axis])
    return jnp.pad(w, pad)


def _norm_proj(x2, hprev, p, *, seq, tm):
    m = x2.shape[0]
    multi_seq = tm > seq
    nb = m // seq
    full = lambda shape: pl.BlockSpec(shape, lambda i: (0,) * len(shape))
    rows = lambda w: pl.BlockSpec((tm, w), lambda i: (i, 0))
    if multi_seq:
        assert tm == m
        hprev_spec = rows(D_MODEL)
        hlast_shape = jax.ShapeDtypeStruct((m, D_MODEL), F32)
        hlast_spec = rows(D_MODEL)
    else:
        assert seq % tm == 0
        tps = seq // tm
        hprev_spec = pl.BlockSpec((1, 1, D_MODEL), lambda i: (i // tps, 0, 0))
        hlast_shape = jax.ShapeDtypeStruct((nb, 1, D_MODEL), F32)
        hlast_spec = pl.BlockSpec((1, 1, D_MODEL), lambda i: (i // tps, 0, 0))
    weights = [p['gw1'], p['gw2'], p['gb'], p['w1'], p['w2'], p['w0'], p['a1'], p['a2'], p['a0'],
               p['g1'], p['g2']]
    return pl.pallas_call(
        functools.partial(_norm_proj_kernel, tm=tm, seq=seq, multi_seq=multi_seq),
        grid=(m // tm,),
        in_specs=[rows(D_MODEL), hprev_spec, full((1, D_MODEL)), full((3, D_MODEL))]
                 + [full(w.shape) for w in weights],
        out_specs=[rows(D_MODEL), rows(GLA_K), rows(D_RWKV), rows(D_RWKV), rows(D_RWKV), hlast_spec],
        out_shape=[jax.ShapeDtypeStruct((m, D_MODEL), BF16),
                   jax.ShapeDtypeStruct((m, GLA_K), F32),
                   jax.ShapeDtypeStruct((m, D_RWKV), F32),
                   jax.ShapeDtypeStruct((m, D_RWKV), F32),
                   jax.ShapeDtypeStruct((m, D_RWKV), F32),
                   hlast_shape],
        scratch_shapes=[pltpu.VMEM((1, D_MODEL), F32)],
        compiler_params=_cparams(("arbitrary",)),
        name="norm_proj",
    )(x2, hprev, p['mix_norm'], p['mu_wag'], *weights)


def _in_proj_kernel(h_ref, w_ref, mu_ref, zprev_ref, z_ref, carry_ref, *, tm, seq, multi_seq):
    i = pl.program_id(1)
    z = jnp.dot(h_ref[...], w_ref[...], preferred_element_type=F32)
    row = lax.broadcasted_iota(jnp.int32, (tm, 1), 0)
    rolled = pltpu.roll(z, 1, 0)
    if multi_seq:
        z_prev = jnp.where(row % seq == 0, zprev_ref[...], rolled)
    else:
        tiles_per_seq = seq // tm
        first = (i % tiles_per_seq) == 0
        prev = jnp.where(first, zprev_ref[0], carry_ref[...])
        z_prev = jnp.where(row == 0, prev, rolled)
        carry_ref[...] = z[tm - 1:tm]
    z_ref[...] = z + mu_ref[...] * (z_prev - z)


def _in_proj(h_bf, w_in, mu_ext, zprev, *, seq, tm, tn):
    m = h_bf.shape[0]
    multi_seq = tm > seq
    if multi_seq:
        zprev_spec = pl.BlockSpec((tm, tn), lambda j, i: (i, j))
    else:
        tps = seq // tm
        zprev_spec = pl.BlockSpec((1, 1, tn), lambda j, i: (i // tps, 0, j))
    return pl.pallas_call(
        functools.partial(_in_proj_kernel, tm=tm, seq=seq, multi_seq=multi_seq),
        grid=(D_IN // tn, m // tm),
        in_specs=[pl.BlockSpec((tm, D_MODEL), lambda j, i: (i, 0)),
                  pl.BlockSpec((D_MODEL, tn), lambda j, i: (0, j)),
                  pl.BlockSpec((1, tn), lambda j, i: (0, j)),
                  zprev_spec],
        out_specs=pl.BlockSpec((tm, tn), lambda j, i: (i, j)),
        out_shape=jax.ShapeDtypeStruct((m, D_IN), F32),
        scratch_shapes=[pltpu.VMEM((1, tn), F32)],
        compiler_params=_cparams(("arbitrary", "arbitrary")),
        name="in_proj",
    )(h_bf, w_in, mu_ext, zprev)


def _plain_proj_kernel(h_ref, w_ref, z_ref):
    z_ref[...] = jnp.dot(h_ref[...], w_ref[...], preferred_element_type=F32)


def _plain_proj(h_bf, w_in, *, tn):
    m = h_bf.shape[0]
    return pl.pallas_call(
        _plain_proj_kernel,
        grid=(D_IN // tn,),
        in_specs=[pl.BlockSpec((m, D_MODEL), lambda j: (0, 0)),
                  pl.BlockSpec((D_MODEL, tn), lambda j: (0, j))],
        out_specs=pl.BlockSpec((m, tn), lambda j: (0, j)),
        out_shape=jax.ShapeDtypeStruct((m, D_IN), F32),
        compiler_params=_cparams(("arbitrary",)),
        name="prev_row_proj",
    )(h_bf, w_in)


def _gla_kernel(q_ref, k_ref, v_ref, g_ref, la_ref, s0_ref, e_ref, gn_ref,
                o_ref, s_ref, st_ref, obuf_ref, *, tc, chunk):
    tb = pl.program_id(1)
    c = chunk

    @pl.when(tb == 0)
    def _():
        st_ref[...] = s0_ref[0]

    row = lax.broadcasted_iota(jnp.int32, (c, 1), 0)
    e_mat = e_ref[...]

    def body(ci, carry):
        r0 = pl.multiple_of(ci * c, c)
        q = q_ref[0, pl.ds(r0, c), :] * (GLA_DK ** -0.5)
        k = k_ref[0, pl.ds(r0, c), :]
        v = v_ref[0, pl.ds(r0, c), :]
        la = la_ref[0, pl.ds(r0, c), :]
        b = jnp.zeros_like(la)
        for s in range(c):
            b = b + jnp.where(row >= s, la[s:s + 1], 0.0)
        b_last = b[c - 1:c]

        parts = []
        for s in range(c):
            dec = jnp.exp(jnp.minimum(b - b[s:s + 1], 0.0))
            parts.append(jnp.where(row >= s, q * k[s:s + 1] * dec, 0.0).astype(BF16))
        pcat = jnp.concatenate(parts, axis=0)
        sc = jnp.concatenate(
            [jnp.dot(pcat[:, hp * MXU_TILE:(hp + 1) * MXU_TILE], e_mat, preferred_element_type=F32)
             for hp in range(GLA_K // MXU_TILE)], axis=1)
        o = jnp.zeros((c, D_GLA), F32)
        for s in range(c):
            o = o + sc[s * c:(s + 1) * c] * v[s:s + 1]

        qd = q * jnp.exp(b)
        kd = k * jnp.exp(b_last - b)
        a_last = jnp.exp(b_last)
        o_heads = []
        for hh in range(GLA_HEADS):
            dk = slice(hh * GLA_DK, (hh + 1) * GLA_DK)
            dv = slice(hh * GLA_DV, (hh + 1) * GLA_DV)
            st = st_ref[:, dk]
            oh = o[:, dv] + _dot_nt(qd[:, dk], st)
            st_ref[:, dk] = a_last[:, dk] * st + _dot_tn(v[:, dv], kd[:, dk])
            oh = oh * lax.rsqrt(jnp.mean(oh * oh, axis=-1, keepdims=True) + NORM_EPS)
            o_heads.append(oh)
        og = jnp.concatenate(o_heads, axis=1)
        g = g_ref[0, pl.ds(r0, c), :]
        obuf_ref[pl.ds(r0, c), :] = og * gn_ref[...] * (g * jax.nn.sigmoid(g))
        return carry

    lax.fori_loop(0, tc // c, body, 0)
    o_ref[0] = obuf_ref[...].astype(BF16)
    s_ref[0] = st_ref[...]


def _gla(z3, la3, s0t, e_mat, gla_norm, *, tc, chunk):
    nb, seq, _ = z3.shape
    blk = lambda w, col: pl.BlockSpec((1, tc, w), lambda b, t: (b, t, col))
    st_spec = pl.BlockSpec((1, GLA_DV, GLA_K), lambda b, t: (b, 0, 0))
    return pl.pallas_call(
        functools.partial(_gla_kernel, tc=tc, chunk=chunk),
        grid=(nb, seq // tc),
        in_specs=[blk(GLA_K, 0), blk(GLA_K, 1), blk(D_GLA, 1), blk(D_GLA, 2), blk(GLA_K, 0),
                  st_spec,
                  pl.BlockSpec(e_mat.shape, lambda b, t: (0, 0)),
                  pl.BlockSpec((1, D_GLA), lambda b, t: (0, 0))],
        out_specs=[blk(D_GLA, 0), st_spec],
        out_shape=[jax.ShapeDtypeStruct((nb, seq, D_GLA), BF16),
                   jax.ShapeDtypeStruct((nb, GLA_DV, GLA_K), F32)],
        scratch_shapes=[pltpu.VMEM((GLA_DV, GLA_K), F32), pltpu.VMEM((tc, D_GLA), F32)],
        compiler_params=_cparams(("arbitrary", "arbitrary")),
        name="gla",
    )(z3, z3, z3, z3, la3, s0t, e_mat, gla_norm)


def _wkv_kernel(r_ref, k_ref, v_ref, lw_ref, a_ref, gate_ref, s0_ref, bd_ref,
                kk_ref, ka_ref, rk_ref, lnw_ref, lnb_ref,
                o_ref, s_ref, st_ref, obuf_ref, *, tc, chunk):
    tb = pl.program_id(1)
    c = chunk
    n_tiles = D_RWKV // MXU_TILE
    heads_per_tile = MXU_TILE // RWKV_HEAD

    @pl.when(tb == 0)
    def _():
        st_ref[...] = s0_ref[0]

    row = lax.broadcasted_iota(jnp.int32, (c, 1), 0)
    bd = bd_ref[...]
    hr = lax.broadcasted_iota(jnp.int32, (MXU_TILE, MXU_TILE), 0) // RWKV_HEAD
    hc = lax.broadcasted_iota(jnp.int32, (MXU_TILE, MXU_TILE), 1) // RWKV_HEAD
    diag_mask = hr == hc
    lane_head = lax.broadcasted_iota(jnp.int32, (RWKV_HEAD, MXU_TILE), 1) // RWKV_HEAD

    def seg_sum(x):
        hi = x.astype(BF16)
        lo = (x - hi.astype(F32)).astype(BF16)
        return jnp.concatenate(
            [jnp.dot(hi[:, t * MXU_TILE:(t + 1) * MXU_TILE], bd, preferred_element_type=F32)
             + jnp.dot(lo[:, t * MXU_TILE:(t + 1) * MXU_TILE], bd, preferred_element_type=F32)
             for t in range(n_tiles)], axis=1)

    def seg_sum_fast(x_bf):
        return jnp.concatenate(
            [jnp.dot(x_bf[:, t * MXU_TILE:(t + 1) * MXU_TILE], bd, preferred_element_type=F32)
             for t in range(n_tiles)], axis=1)

    def body(ci, carry):
        r0 = pl.multiple_of(ci * c, c)
        r = r_ref[0, pl.ds(r0, c), :]
        k = k_ref[0, pl.ds(r0, c), :]
        v = v_ref[0, pl.ds(r0, c), :]
        lw = lw_ref[0, pl.ds(r0, c), :]
        a_sig = a_ref[0, pl.ds(r0, c), :]

        kk = k * kk_ref[...]
        kk = kk / jnp.maximum(jnp.sqrt(seg_sum(kk * kk)), 1e-12)
        a_vec = -kk
        b_vec = kk * a_sig
        k_eff = k * (1.0 + (a_sig - 1.0) * ka_ref[...])

        lb = jnp.zeros_like(lw)
        for s in range(c):
            lb = lb + jnp.where(row >= s, lw[s:s + 1], 0.0)
        lbp = lb - lw
        lb_last = lb[c - 1:c]

        p_ab, p_ak, p_rb, p_rk = [], [], [], []
        for s in range(c):
            lbs = lb[s:s + 1]
            a_d = a_vec * jnp.exp(jnp.minimum(lbp - lbs, 0.0))
            r_d = r * jnp.exp(jnp.minimum(lb - lbs, 0.0))
            lower = row > s
            lower_eq = row >= s
            p_ab.append(jnp.where(lower, a_d * b_vec[s:s + 1], 0.0).astype(BF16))
            p_ak.append(jnp.where(lower, a_d * k_eff[s:s + 1], 0.0).astype(BF16))
            p_rb.append(jnp.where(lower_eq, r_d * b_vec[s:s + 1], 0.0).astype(BF16))
            p_rk.append(jnp.where(lower_eq, r_d * k_eff[s:s + 1], 0.0).astype(BF16))
        coef = seg_sum_fast(jnp.concatenate(p_ab + p_ak + p_rb + p_rk, axis=0))
        cc = c * c

        lhs = jnp.concatenate([a_vec * jnp.exp(lbp), r * jnp.exp(lb)], axis=0).astype(BF16)
        x0 = []
        for t in range(n_tiles):
            ln = slice(t * MXU_TILE, (t + 1) * MXU_TILE)
            st_t = st_ref[:, ln]
            w_t = jnp.where(diag_mask, jnp.concatenate([st_t] * heads_per_tile, axis=0), 0.0)
            x0.append(_dot_nt(lhs[:, ln], w_t))
        x0 = jnp.concatenate(x0, axis=1)
        u = x0[:c]
        o = x0[c:]

        for s in range(c):
            u = u + coef[cc + s * c:cc + (s + 1) * c] * v[s:s + 1]
        for s in range(c):
            u = u + coef[s * c:(s + 1) * c] * u[s:s + 1]
        for s in range(c):
            o = (o + coef[2 * cc + s * c:2 * cc + (s + 1) * c] * u[s:s + 1]
                 + coef[3 * cc + s * c:3 * cc + (s + 1) * c] * v[s:s + 1])

        tail = jnp.exp(lb_last - lb)
        uv = jnp.concatenate([u, v], axis=0).astype(BF16)
        bk = jnp.concatenate([b_vec * tail, k_eff * tail], axis=0).astype(BF16)
        w_last = jnp.exp(lb_last)
        for t in range(n_tiles):
            ln = slice(t * MXU_TILE, (t + 1) * MXU_TILE)
            zt = _dot_tn(uv[:, ln], bk[:, ln])
            upd = jnp.zeros((RWKV_HEAD, MXU_TILE), F32)
            for hh in range(heads_per_tile):
                upd = upd + jnp.where(lane_head == hh, zt[hh * RWKV_HEAD:(hh + 1) * RWKV_HEAD], 0.0)
            st_ref[:, ln] = w_last[:, ln] * st_ref[:, ln] + upd

        inv_n = 1.0 / RWKV_HEAD
        mu = seg_sum(o) * inv_n
        dev = o - mu
        var = seg_sum(dev * dev) * inv_n
        yn = dev * lax.rsqrt(var + GN_EPS) * lnw_ref[...] + lnb_ref[...]
        bonus = seg_sum(r * k_eff * rk_ref[...]) * v
        obuf_ref[pl.ds(r0, c), :] = (yn + bonus) * gate_ref[0, pl.ds(r0, c), :]
        return carry

    lax.fori_loop(0, tc // c, body, 0)
    o_ref[0] = obuf_ref[...].astype(BF16)
    s_ref[0] = st_ref[...]


def _wkv(z3, lw3, a3, gate3, s0t, bd, p, *, tc, chunk):
    nb, seq, _ = z3.shape
    col0 = RWKV_OFF // D_RWKV
    blk = lambda col: pl.BlockSpec((1, tc, D_RWKV), lambda b, t: (b, t, col))
    st_spec = pl.BlockSpec((1, RWKV_HEAD, D_RWKV), lambda b, t: (b, 0, 0))
    vec = pl.BlockSpec((1, D_RWKV), lambda b, t: (0, 0))
    return pl.pallas_call(
        functools.partial(_wkv_kernel, tc=tc, chunk=chunk),
        grid=(nb, seq // tc),
        in_specs=[blk(col0), blk(col0 + 1), blk(col0 + 2), blk(0), blk(0), blk(0), st_spec,
                  pl.BlockSpec(bd.shape, lambda b, t: (0, 0)), vec, vec, vec, vec, vec],
        out_specs=[blk(0), st_spec],
        out_shape=[jax.ShapeDtypeStruct((nb, seq, D_RWKV), BF16),
                   jax.ShapeDtypeStruct((nb, RWKV_HEAD, D_RWKV), F32)],
        scratch_shapes=[pltpu.VMEM((RWKV_HEAD, D_RWKV), F32), pltpu.VMEM((tc, D_RWKV), F32)],
        compiler_params=_cparams(("arbitrary", "arbitrary")),
        name="wkv",
    )(z3, z3, z3, lw3, a3, gate3, s0t, bd, p['k_k'], p['k_a'], p['r_k'], p['ln_w'], p['ln_b'])


def _out_router_kernel(x_ref, og_ref, or_ref, wo_ref, fg_ref, rw_ref, rb_ref, out_ref, *, tm):
    mix = (jnp.dot(og_ref[...], wo_ref[0], preferred_element_type=F32)
           + jnp.dot(or_ref[...], wo_ref[1], preferred_element_type=F32))
    x1 = x_ref[...] + mix
    out_ref[:, :D_MODEL] = x1
    h2 = _rms(x1, fg_ref[...])
    logits = _dot(h2, rw_ref[...]) + rb_ref[...]
    lane = lax.broadcasted_iota(jnp.int32, (tm, ROUTE_LANES), 1)
    neg = jnp.float32(-jnp.inf)
    big = jnp.int32(1 << 20)

    def first_argmax(vals):
        mx = jnp.max(vals, axis=-1, keepdims=True)
        idx = jnp.min(jnp.where(vals == mx, lane, big), axis=-1, keepdims=True)
        return mx, idx

    lg = jnp.where(lane < N_GROUPS, logits, neg)
    g_max, g_idx = first_argmax(lg)
    p_group = 1.0 / jnp.sum(jnp.exp(lg - g_max), axis=-1, keepdims=True)
    lo = N_GROUPS + g_idx * EXPERTS_PER_GROUP
    le = jnp.where((lane >= lo) & (lane < lo + EXPERTS_PER_GROUP), logits, neg)
    m1, i1 = first_argmax(le)
    m2, i2 = first_argmax(jnp.where(lane == i1, neg, le))
    e2 = jnp.exp(m2 - m1)
    w1 = 1.0 / (1.0 + e2)
    w2 = e2 / (1.0 + e2)
    route = jnp.where(lane == i1, w1, jnp.where(lane == i2, w2, 0.0)) * p_group
    out_ref[:, D_MODEL:] = jnp.where(lane == ROUTE_LANES - 1, g_idx.astype(F32), route)


def _out_router(x2, og, orr, wo, ffn_norm, rw, rb, *, tm):
    m = x2.shape[0]
    rows = lambda w: pl.BlockSpec((tm, w), lambda i: (i, 0))
    full = lambda shape: pl.BlockSpec(shape, lambda i: (0,) * len(shape))
    return pl.pallas_call(
        functools.partial(_out_router_kernel, tm=tm),
        grid=(m // tm,),
        in_specs=[rows(D_MODEL), rows(D_GLA), rows(D_RWKV), full(wo.shape), full((1, D_MODEL)),
                  full(rw.shape), full((1, ROUTE_LANES))],
        out_specs=rows(X1_WIDTH),
        out_shape=jax.ShapeDtypeStruct((m, X1_WIDTH), F32),
        compiler_params=_cparams(("arbitrary",)),
        name="out_router",
    )(x2, og, orr, wo, ffn_norm, rw, rb)


def _moe_kernel(tile_group_ref, tile_rows_ref, src_ref,
                x1_hbm, wg_ref, wu_ref, wd_ref, fg_ref, fn_ref, y_hbm,
                xbuf, hbuf, acc, ybuf, gsem, ssem, *, tile):
    j = pl.program_id(0)
    e = pl.program_id(1)
    nrows = tile_rows_ref[j]
    base = j * tile

    def gather_copy(rr):
        return pltpu.make_async_copy(x1_hbm.at[pl.ds(src_ref[base + rr], 1)], xbuf.at[pl.ds(rr, 1)], gsem)

    def scatter_copy(rr):
        return pltpu.make_async_copy(ybuf.at[pl.ds(rr, 1)], y_hbm.at[pl.ds(src_ref[base + rr], 1)], ssem)

    @pl.when((e == 0) & (nrows > 0))
    def _():
        def start(rr, cy):
            gather_copy(rr).start()
            return cy
        lax.fori_loop(0, tile, start, 0)

        def wait(rr, cy):
            gather_copy(rr).wait()
            return cy
        lax.fori_loop(0, tile, wait, 0)
        hbuf[...] = _rms(xbuf[:, :D_MODEL], fg_ref[...]).astype(BF16)
        acc[...] = jnp.zeros_like(acc)

    @pl.when(nrows > 0)
    def _():
        g = tile_group_ref[j]
        lane = lax.broadcasted_iota(jnp.int32, (tile, ROUTE_LANES), 1)
        wcol = jnp.sum(jnp.where(lane == N_GROUPS + g * EXPERTS_PER_GROUP + e, xbuf[:, D_MODEL:], 0.0),
                       axis=-1, keepdims=True)
        h2 = hbuf[...]
        gate = jnp.dot(h2, wg_ref[0, 0].astype(BF16), preferred_element_type=F32)
        up = jnp.dot(h2, wu_ref[0, 0].astype(BF16), preferred_element_type=F32)
        hid = (gate * jax.nn.sigmoid(gate)) * up * wcol
        acc[...] += jnp.dot(hid.astype(BF16), wd_ref[0, 0].astype(BF16), preferred_element_type=F32)

    @pl.when((e == EXPERTS_PER_GROUP - 1) & (nrows > 0))
    def _():
        ybuf[...] = _rms(xbuf[:, :D_MODEL] + acc[...], fn_ref[...])

        def start(rr, cy):
            scatter_copy(rr).start()
            return cy
        lax.fori_loop(0, nrows, start, 0)

        def wait(rr, cy):
            scatter_copy(rr).wait()
            return cy
        lax.fori_loop(0, nrows, wait, 0)


def _moe(x1e, tile_group, tile_rows, src, wg, wu, wd, ffn_norm, final_norm, *, tile):
    m = x1e.shape[0]
    n_tiles = tile_group.shape[0]

    def w_idx(j, e, tg, tr, sr):
        return (tg[j], jnp.where(tr[j] > 0, e, EXPERTS_PER_GROUP - 1), 0, 0)

    vec = pl.BlockSpec((1, D_MODEL), lambda j, e, tg, tr, sr: (0, 0))
    grid_spec = pltpu.PrefetchScalarGridSpec(
        num_scalar_prefetch=3,
        grid=(n_tiles, EXPERTS_PER_GROUP),
        in_specs=[pl.BlockSpec(memory_space=pl.ANY),
                  pl.BlockSpec((1, 1, D_MODEL, D_EXPERT), w_idx),
                  pl.BlockSpec((1, 1, D_MODEL, D_EXPERT), w_idx),
                  pl.BlockSpec((1, 1, D_EXPERT, D_MODEL), w_idx),
                  vec, vec],
        out_specs=pl.BlockSpec(memory_space=pl.ANY),
        scratch_shapes=[pltpu.VMEM((tile, X1_WIDTH), F32),
                        pltpu.VMEM((tile, D_MODEL), BF16),
                        pltpu.VMEM((tile, D_MODEL), F32),
                        pltpu.VMEM((tile, D_MODEL), F32),
                        pltpu.SemaphoreType.DMA(()),
                        pltpu.SemaphoreType.DMA(())],
    )
    return pl.pallas_call(
        functools.partial(_moe_kernel, tile=tile),
        grid_spec=grid_spec,
        out_shape=jax.ShapeDtypeStruct((m, D_MODEL), F32),
        compiler_params=_cparams(("arbitrary", "arbitrary")),
        name="moe",
    )(tile_group, tile_rows, src, x1e, wg, wu, wd, ffn_norm, final_norm)


def _moe_plan(group_id, tile):
    m = group_id.shape[0]
    n_tiles = m // tile + N_GROUPS
    order = jnp.argsort(group_id, stable=True).astype(jnp.int32)
    counts = jnp.sum(group_id[:, None] == jnp.arange(N_GROUPS, dtype=jnp.int32)[None, :], axis=0).astype(jnp.int32)
    tiles_per = (counts + tile - 1) // tile
    tile_end = jnp.cumsum(tiles_per)
    tile_start = tile_end - tiles_per
    row_start = jnp.cumsum(counts) - counts
    t = jnp.arange(n_tiles, dtype=jnp.int32)
    used = t < tile_end[-1]
    grp = jnp.minimum(jnp.sum(t[:, None] >= tile_end[None, :], axis=1), N_GROUPS - 1).astype(jnp.int32)
    last_group = jnp.max(jnp.where(counts > 0, jnp.arange(N_GROUPS, dtype=jnp.int32), 0))
    tile_group = jnp.where(used, grp, last_group).astype(jnp.int32)
    local = (t - tile_start[grp]) * tile
    tile_rows = jnp.where(used, jnp.clip(counts[grp] - local, 0, tile), 0).astype(jnp.int32)
    slot = jnp.arange(n_tiles * tile, dtype=jnp.int32)
    st = slot // tile
    within = local[st] + (slot - st * tile)
    g_of = tile_group[st]
    pos = row_start[g_of] + jnp.minimum(within, jnp.maximum(counts[g_of] - 1, 0))
    src = order[jnp.clip(pos, 0, m - 1)]
    return tile_group, tile_rows, src.astype(jnp.int32)


def _prep_params(mix_norm, w_in, gla_gate_w1, gla_gate_w2, gla_gate_b, gla_norm, rwkv_mu_rkv, rwkv_mu_wag,
                 rwkv_w0, rwkv_w1, rwkv_w2, rwkv_a0, rwkv_a1, rwkv_a2, rwkv_g1, rwkv_g2,
                 rwkv_k_k, rwkv_k_a, rwkv_r_k, rwkv_ln_w, rwkv_ln_b, w_out, ffn_norm,
                 router_group_w, router_group_b, router_expert_w, router_expert_b, final_norm):
    row = lambda v: v.reshape(1, -1).astype(F32)
    g_rank = 2 * LANE
    p = dict(
        mix_norm=row(mix_norm[0]),
        mu_wag=rwkv_mu_wag[0],
        w_in=w_in[0].astype(BF16),
        gw1=_pad_axis(gla_gate_w1[0], 1, LANE).astype(BF16),
        gw2=_pad_axis(gla_gate_w2[0], 0, LANE).astype(BF16),
        gb=row(gla_gate_b[0]),
        w1=_pad_axis(rwkv_w1[0], 1, LANE).astype(BF16),
        w2=_pad_axis(rwkv_w2[0], 0, LANE).astype(BF16),
        w0=row(rwkv_w0[0]),
        a1=_pad_axis(rwkv_a1[0], 1, LANE).astype(BF16),
        a2=_pad_axis(rwkv_a2[0], 0, LANE).astype(BF16),
        a0=row(rwkv_a0[0]),
        g1=_pad_axis(rwkv_g1[0], 1, g_rank).astype(BF16),
        g2=_pad_axis(rwkv_g2[0], 0, g_rank).astype(BF16),
        gla_norm=row(gla_norm[0]),
        mu_ext=jnp.concatenate([jnp.zeros((1, RWKV_OFF), F32), rwkv_mu_rkv[0].reshape(1, -1)], axis=1),
        k_k=row(rwkv_k_k[0]), k_a=row(rwkv_k_a[0]), r_k=row(rwkv_r_k[0]),
        ln_w=row(rwkv_ln_w[0]), ln_b=row(rwkv_ln_b[0]),
        wo=w_out[0].astype(BF16).reshape(2, D_GLA, D_MODEL),
        ffn_norm=row(ffn_norm[0]),
        router_w=_pad_axis(jnp.concatenate([router_group_w[0], router_expert_w[0]], axis=1), 1,
                           ROUTE_LANES).astype(BF16),
        router_b=_pad_axis(jnp.concatenate([router_group_b[0], router_expert_b[0]]).reshape(1, -1), 1,
                           ROUTE_LANES).astype(F32),
        final_norm=row(final_norm),
    )
    r = jnp.arange(MXU_TILE)
    p['gla_e'] = (r[:, None] // GLA_DK == jnp.arange(2 * GLA_DV)[None, :] // GLA_DV).astype(BF16)
    p['wkv_bd'] = (r[:, None] // RWKV_HEAD == r[None, :] // RWKV_HEAD).astype(BF16)
    return p


def _run(x, s_gla, s_wkv, s_shift, p, wg, wu, wd, *, tm, tc):
    nb, seq, _ = x.shape
    m = nb * seq
    x2 = x.reshape(m, D_MODEL)
    multi_seq = tm > seq

    h_last = jnp.zeros((nb, D_MODEL), F32) if s_shift is None else s_shift
    zl = _plain_proj(_pad_axis(h_last, 0, -(-nb // 16) * 16).astype(BF16), p['w_in'], tn=1024)[:nb]
    if multi_seq:
        expand = lambda a: jnp.repeat(a, seq, axis=0)
        hprev, zprev = expand(h_last), expand(zl)
    else:
        hprev, zprev = h_last.reshape(nb, 1, D_MODEL), zl.reshape(nb, 1, D_IN)

    h_bf, la, lw, a_sig, gate, hlast = _norm_proj(x2, hprev, p, seq=seq, tm=tm)
    shift_new = hlast.reshape(nb, seq, D_MODEL)[:, -1] if multi_seq else hlast.reshape(nb, D_MODEL)
    z = _in_proj(h_bf, p['w_in'], p['mu_ext'], zprev, seq=seq, tm=tm, tn=1024)

    z3 = z.reshape(nb, seq, D_IN)
    r3 = lambda a: a.reshape(nb, seq, a.shape[-1])
    if s_gla is None:
        g0 = jnp.zeros((nb, GLA_DV, GLA_K), F32)
        w0 = jnp.zeros((nb, RWKV_HEAD, D_RWKV), F32)
    else:
        g0 = s_gla.transpose(0, 3, 1, 2).reshape(nb, GLA_DV, GLA_K)
        w0 = s_wkv.transpose(0, 2, 1, 3).reshape(nb, RWKV_HEAD, D_RWKV)
    og, g_new = _gla(z3, r3(la), g0, p['gla_e'], p['gla_norm'], tc=tc, chunk=min(GLA_CHUNK, seq))
    orr, w_new = _wkv(z3, r3(lw), r3(a_sig), r3(gate), w0, p['wkv_bd'], p, tc=tc, chunk=min(WKV_CHUNK, seq))
    gla_new = g_new.reshape(nb, GLA_DV, GLA_HEADS, GLA_DK).transpose(0, 2, 3, 1)
    wkv_new = w_new.reshape(nb, RWKV_HEAD, RWKV_HEADS, RWKV_HEAD).transpose(0, 2, 1, 3)

    x1e = _out_router(x2, og.reshape(m, D_GLA), orr.reshape(m, D_RWKV), p['wo'], p['ffn_norm'],
                      p['router_w'], p['router_b'], tm=min(tm, 256))
    group_id = x1e[:, X1_WIDTH - 1].astype(jnp.int32)
    moe_tile = min(MOE_TILE, m // 4)
    tile_group, tile_rows, src = _moe_plan(group_id, moe_tile)
    y = _moe(x1e, tile_group, tile_rows, src, wg, wu, wd, p['ffn_norm'], p['final_norm'], tile=moe_tile)
    return y.reshape(nb, seq, D_MODEL), gla_new, wkv_new, shift_new


def kernel(x_prompt, x_sample, state_gla, state_wkv, state_shift, mix_norm, w_in, gla_gate_w1, gla_gate_w2, gla_gate_b, gla_norm, rwkv_mu_rkv, rwkv_mu_wag, rwkv_w0, rwkv_w1, rwkv_w2, rwkv_a0, rwkv_a1, rwkv_a2, rwkv_g1, rwkv_g2, rwkv_k_k, rwkv_k_a, rwkv_r_k, rwkv_ln_w, rwkv_ln_b, w_out, ffn_norm, router_group_w, router_group_b, router_expert_w, router_expert_b, expert_w_gate, expert_w_up, expert_w_down, final_norm):
    p = _prep_params(mix_norm, w_in, gla_gate_w1, gla_gate_w2, gla_gate_b, gla_norm, rwkv_mu_rkv, rwkv_mu_wag,
                     rwkv_w0, rwkv_w1, rwkv_w2, rwkv_a0, rwkv_a1, rwkv_a2, rwkv_g1, rwkv_g2,
                     rwkv_k_k, rwkv_k_a, rwkv_r_k, rwkv_ln_w, rwkv_ln_b, w_out, ffn_norm,
                     router_group_w, router_group_b, router_expert_w, router_expert_b, final_norm)
    wg, wu, wd = expert_w_gate[0], expert_w_up[0], expert_w_down[0]
    seq_p = x_prompt.shape[1]
    y_p, gla_p, wkv_p, shift_p = _run(x_prompt, None, None, None, p, wg, wu, wd, tm=256, tc=min(256, seq_p))
    m_s = x_sample.shape[0] * x_sample.shape[1]
    y_s, gla_s, wkv_s, shift_s = _run(x_sample, state_gla[0], state_wkv[0], state_shift[0], p, wg, wu, wd,
                                      tm=m_s, tc=x_sample.shape[1])
    return (y_p, y_s, gla_p[None], wkv_p[None], shift_p[None], gla_s[None], wkv_s[None], shift_s[None])
```

```python
import functools

import jax
import jax.numpy as jnp
from jax import lax
from jax.experimental import pallas as pl
from jax.experimental.pallas import tpu as pltpu

F32 = jnp.float32
BF16 = jnp.bfloat16

D_MODEL = 2048
D_GLA = 1024
D_RWKV = 1024
GLA_HEADS = 4
GLA_DV = 256
GLA_DK = 128
GLA_K = 512
GLA_GATE_TEMP = 16.0
RWKV_HEAD = 64
RWKV_HEADS = 16
N_GROUPS = 8
EXPERTS_PER_GROUP = 8
D_EXPERT = 512
NORM_EPS = 1e-6
GN_EPS = 64e-5
RWKV_OFF = 2 * GLA_K + 2 * D_GLA
D_IN = RWKV_OFF + 3 * D_RWKV

LANE = 128
SUBLANE = 8
MXU_TILE = 256
VMEM_LIMIT = 56 * 1024 * 1024

GLA_CHUNK = 16
WKV_CHUNK = 8
MOE_SUB = 256
ROUTE_LANES = 128
X1_WIDTH = D_MODEL + ROUTE_LANES
DMA_UNROLL = 8


def _cparams(sem):
    return pltpu.CompilerParams(dimension_semantics=sem, vmem_limit_bytes=VMEM_LIMIT)


def _rms(x, gain):
    return x * lax.rsqrt(jnp.mean(x * x, axis=-1, keepdims=True) + NORM_EPS) * gain


def _softplus(y):
    return jnp.maximum(y, 0.0) + jnp.log1p(jnp.exp(-jnp.abs(y)))


def _dot(a, b):
    return jnp.dot(a.astype(BF16), b.astype(BF16), preferred_element_type=F32)


def _dot_nt(a, b):
    return lax.dot_general(a.astype(BF16), b.astype(BF16), (((1,), (1,)), ((), ())),
                           preferred_element_type=F32)


def _dot_tn(a, b):
    return lax.dot_general(a.astype(BF16), b.astype(BF16), (((0,), (0,)), ((), ())),
                           preferred_element_type=F32)


def _norm_proj_kernel(x_ref, hprev_ref, gain_ref, mu_ref, gw1_ref, gw2_ref, gb_ref,
                      w1_ref, w2_ref, w0_ref, a1_ref, a2_ref, a0_ref, g1_ref, g2_ref,
                      h_ref, la_ref, lw_ref, a_ref, gate_ref, hlast_ref, carry_ref,
                      *, tm, seq, multi_seq):
    i = pl.program_id(0)
    h = _rms(x_ref[...], gain_ref[...])
    row = lax.broadcasted_iota(jnp.int32, (tm, 1), 0)
    rolled = pltpu.roll(h, 1, 0)
    if multi_seq:
        h_shift = jnp.where(row % seq == 0, hprev_ref[...], rolled)
        hlast_ref[...] = h
    else:
        tiles_per_seq = seq // tm
        first = (i % tiles_per_seq) == 0
        prev = jnp.where(first, hprev_ref[0], carry_ref[...])
        h_shift = jnp.where(row == 0, prev, rolled)
        carry_ref[...] = h[tm - 1:tm]
        hlast_ref[0] = h[tm - 1:tm]
    h_ref[...] = h.astype(BF16)

    dx = h_shift - h
    mu = mu_ref[...]
    xw = h + dx * mu[0:1]
    xa = h + dx * mu[1:2]
    xg = h + dx * mu[2:3]

    gl = _dot(_dot(h, gw1_ref[...]), gw2_ref[...]) + gb_ref[...]
    la_ref[...] = -_softplus(-gl) * (1.0 / GLA_GATE_TEMP)

    u = w0_ref[...] + _dot(jnp.tanh(_dot(xw, w1_ref[...])), w2_ref[...])
    w_log = -_softplus(-u) - 0.5
    lw_ref[...] = -jnp.exp(w_log)

    a_ref[...] = jax.nn.sigmoid(a0_ref[...] + _dot(_dot(xa, a1_ref[...]), a2_ref[...]))
    gate_ref[...] = _dot(jax.nn.sigmoid(_dot(xg, g1_ref[...])), g2_ref[...])


def _pad_axis(w, axis, to):
    pad = [(0, 0)] * w.ndim
    pad[axis] = (0, to - w.shape[axis])
    return jnp.pad(w, pad)


def _norm_proj(x2, hprev, p, *, seq, tm):
    m = x2.shape[0]
    multi_seq = tm > seq
    nb = m // seq
    full = lambda shape: pl.BlockSpec(shape, lambda i: (0,) * len(shape))
    rows = lambda w: pl.BlockSpec((tm, w), lambda i: (i, 0))
    if multi_seq:
        assert tm == m
        hprev_spec = rows(D_MODEL)
        hlast_shape = jax.ShapeDtypeStruct((m, D_MODEL), F32)
        hlast_spec = rows(D_MODEL)
    else:
        assert seq % tm == 0
        tps = seq // tm
        hprev_spec = pl.BlockSpec((1, 1, D_MODEL), lambda i: (i // tps, 0, 0))
        hlast_shape = jax.ShapeDtypeStruct((nb, 1, D_MODEL), F32)
        hlast_spec = pl.BlockSpec((1, 1, D_MODEL), lambda i: (i // tps, 0, 0))
    weights = [p['gw1'], p['gw2'], p['gb'], p['w1'], p['w2'], p['w0'], p['a1'], p['a2'], p['a0'],
               p['g1'], p['g2']]
    return pl.pallas_call(
        functools.partial(_norm_proj_kernel, tm=tm, seq=seq, multi_seq=multi_seq),
        grid=(m // tm,),
        in_specs=[rows(D_MODEL), hprev_spec, full((1, D_MODEL)), full((3, D_MODEL))]
                 + [full(w.shape) for w in weights],
        out_specs=[rows(D_MODEL), rows(GLA_K), rows(D_RWKV), rows(D_RWKV), rows(D_RWKV), hlast_spec],
        out_shape=[jax.ShapeDtypeStruct((m, D_MODEL), BF16),
                   jax.ShapeDtypeStruct((m, GLA_K), F32),
                   jax.ShapeDtypeStruct((m, D_RWKV), F32),
                   jax.ShapeDtypeStruct((m, D_RWKV), F32),
                   jax.ShapeDtypeStruct((m, D_RWKV), F32),
                   hlast_shape],
        scratch_shapes=[pltpu.VMEM((1, D_MODEL), F32)],
        compiler_params=_cparams(("arbitrary",)),
        name="norm_proj",
    )(x2, hprev, p['mix_norm'], p['mu_wag'], *weights)


def _in_proj_kernel(h_ref, w_ref, mu_ref, zprev_ref, z_ref, carry_ref, *, tm, seq, multi_seq):
    i = pl.program_id(1)
    z = jnp.dot(h_ref[...], w_ref[...], preferred_element_type=F32)
    row = lax.broadcasted_iota(jnp.int32, (tm, 1), 0)
    rolled = pltpu.roll(z, 1, 0)
    if multi_seq:
        z_prev = jnp.where(row % seq == 0, zprev_ref[...], rolled)
    else:
        tiles_per_seq = seq // tm
        first = (i % tiles_per_seq) == 0
        prev = jnp.where(first, zprev_ref[0], carry_ref[...])
        z_prev = jnp.where(row == 0, prev, rolled)
        carry_ref[...] = z[tm - 1:tm]
    z_ref[...] = z + mu_ref[...] * (z_prev - z)


def _in_proj(h_bf, w_in, mu_ext, zprev, *, seq, tm, tn):
    m = h_bf.shape[0]
    multi_seq = tm > seq
    if multi_seq:
        zprev_spec = pl.BlockSpec((tm, tn), lambda j, i: (i, j))
    else:
        tps = seq // tm
        zprev_spec = pl.BlockSpec((1, 1, tn), lambda j, i: (i // tps, 0, j))
    return pl.pallas_call(
        functools.partial(_in_proj_kernel, tm=tm, seq=seq, multi_seq=multi_seq),
        grid=(D_IN // tn, m // tm),
        in_specs=[pl.BlockSpec((tm, D_MODEL), lambda j, i: (i, 0)),
                  pl.BlockSpec((D_MODEL, tn), lambda j, i: (0, j)),
                  pl.BlockSpec((1, tn), lambda j, i: (0, j)),
                  zprev_spec],
        out_specs=pl.BlockSpec((tm, tn), lambda j, i: (i, j)),
        out_shape=jax.ShapeDtypeStruct((m, D_IN), F32),
        scratch_shapes=[pltpu.VMEM((1, tn), F32)],
        compiler_params=_cparams(("arbitrary", "arbitrary")),
        name="in_proj",
    )(h_bf, w_in, mu_ext, zprev)


def _plain_proj_kernel(h_ref, w_ref, z_ref):
    z_ref[...] = jnp.dot(h_ref[...], w_ref[...], preferred_element_type=F32)


def _plain_proj(h_bf, w_in, *, tn):
    m = h_bf.shape[0]
    return pl.pallas_call(
        _plain_proj_kernel,
        grid=(D_IN // tn,),
        in_specs=[pl.BlockSpec((m, D_MODEL), lambda j: (0, 0)),
                  pl.BlockSpec((D_MODEL, tn), lambda j: (0, j))],
        out_specs=pl.BlockSpec((m, tn), lambda j: (0, j)),
        out_shape=jax.ShapeDtypeStruct((m, D_IN), F32),
        compiler_params=_cparams(("arbitrary",)),
        name="prev_row_proj",
    )(h_bf, w_in)


def _gla_kernel(q_ref, k_ref, v_ref, g_ref, la_ref, s0_ref, e_ref, gn_ref,
                o_ref, s_ref, st_ref, obuf_ref, *, tc, chunk):
    tb = pl.program_id(1)
    c = chunk

    @pl.when(tb == 0)
    def _():
        st_ref[...] = s0_ref[0]

    row = lax.broadcasted_iota(jnp.int32, (c, 1), 0)
    e_mat = e_ref[...]

    def body(ci, carry):
        r0 = pl.multiple_of(ci * c, c)
        q = q_ref[0, pl.ds(r0, c), :] * (GLA_DK ** -0.5)
        k = k_ref[0, pl.ds(r0, c), :]
        v = v_ref[0, pl.ds(r0, c), :]
        la = la_ref[0, pl.ds(r0, c), :]
        b = jnp.zeros_like(la)
        for s in range(c):
            b = b + jnp.where(row >= s, la[s:s + 1], 0.0)
        b_last = b[c - 1:c]

        parts = []
        for s in range(c):
            dec = jnp.exp(jnp.minimum(b - b[s:s + 1], 0.0))
            parts.append(jnp.where(row >= s, q * k[s:s + 1] * dec, 0.0).astype(BF16))
        pcat = jnp.concatenate(parts, axis=0)
        sc = jnp.concatenate(
            [jnp.dot(pcat[:, hp * MXU_TILE:(hp + 1) * MXU_TILE], e_mat, preferred_element_type=F32)
             for hp in range(GLA_K // MXU_TILE)], axis=1)
        o = jnp.zeros((c, D_GLA), F32)
        for s in range(c):
            o = o + sc[s * c:(s + 1) * c] * v[s:s + 1]

        qd = q * jnp.exp(b)
        kd = k * jnp.exp(b_last - b)
        a_last = jnp.exp(b_last)
        o_heads = []
        for hh in range(GLA_HEADS):
            dk = slice(hh * GLA_DK, (hh + 1) * GLA_DK)
            dv = slice(hh * GLA_DV, (hh + 1) * GLA_DV)
            st = st_ref[:, dk]
            oh = o[:, dv] + _dot_nt(qd[:, dk], st)
            st_ref[:, dk] = a_last[:, dk] * st + _dot_tn(v[:, dv], kd[:, dk])
            oh = oh * lax.rsqrt(jnp.mean(oh * oh, axis=-1, keepdims=True) + NORM_EPS)
            o_heads.append(oh)
        og = jnp.concatenate(o_heads, axis=1)
        g = g_ref[0, pl.ds(r0, c), :]
        obuf_ref[pl.ds(r0, c), :] = og * gn_ref[...] * (g * jax.nn.sigmoid(g))
        return carry

    lax.fori_loop(0, tc // c, body, 0)
    o_ref[0] = obuf_ref[...].astype(BF16)
    s_ref[0] = st_ref[...]


def _gla(z3, la3, s0t, e_mat, gla_norm, *, tc, chunk):
    nb, seq, _ = z3.shape
    blk = lambda w, col: pl.BlockSpec((1, tc, w), lambda b, t: (b, t, col))
    st_spec = pl.BlockSpec((1, GLA_DV, GLA_K), lambda b, t: (b, 0, 0))
    return pl.pallas_call(
        functools.partial(_gla_kernel, tc=tc, chunk=chunk),
        grid=(nb, seq // tc),
        in_specs=[blk(GLA_K, 0), blk(GLA_K, 1), blk(D_GLA, 1), blk(D_GLA, 2), blk(GLA_K, 0),
                  st_spec,
                  pl.BlockSpec(e_mat.shape, lambda b, t: (0, 0)),
                  pl.BlockSpec((1, D_GLA), lambda b, t: (0, 0))],
        out_specs=[blk(D_GLA, 0), st_spec],
        out_shape=[jax.ShapeDtypeStruct((nb, seq, D_GLA), BF16),
                   jax.ShapeDtypeStruct((nb, GLA_DV, GLA_K), F32)],
        scratch_shapes=[pltpu.VMEM((GLA_DV, GLA_K), F32), pltpu.VMEM((tc, D_GLA), F32)],
        compiler_params=_cparams(("arbitrary", "arbitrary")),
        name="gla",
    )(z3, z3, z3, z3, la3, s0t, e_mat, gla_norm)


N_LANE_TILES = D_RWKV // MXU_TILE
HEADS_PER_TILE = MXU_TILE // RWKV_HEAD


def _stack_tiles(x):
    return jnp.concatenate([x[:, t * MXU_TILE:(t + 1) * MXU_TILE] for t in range(N_LANE_TILES)], axis=0)


def _unstack_tiles(y, r):
    return jnp.concatenate([y[t * r:(t + 1) * r] for t in range(N_LANE_TILES)], axis=1)


def _seg_sums(xs, bd):
    c = xs[0].shape[0]
    st = jnp.concatenate([_stack_tiles(x) for x in xs], axis=0)
    hi = st.astype(BF16)
    lo = (st - hi.astype(F32)).astype(BF16)
    res = jnp.dot(jnp.concatenate([hi, lo], axis=0), bd, preferred_element_type=F32)
    half = st.shape[0]
    res = res[:half] + res[half:]
    rows = N_LANE_TILES * c
    return [_unstack_tiles(res[n * rows:(n + 1) * rows], c) for n in range(len(xs))]


def _cumsum_rows(x, row, c):
    if c == SUBLANE:
        step = 1
        while step < c:
            x = x + jnp.where(row >= step, pltpu.roll(x, step, 0), 0.0)
            step *= 2
        return x
    out = jnp.zeros_like(x)
    for s in range(c):
        out = out + jnp.where(row >= s, x[s:s + 1], 0.0)
    return out


def _wkv_kernel(r_ref, k_ref, v_ref, lw_ref, a_ref, gate_ref, s0_ref, bd_ref,
                kk_ref, ka_ref, rk_ref, lnw_ref, lnb_ref,
                o_ref, s_ref, st_ref, obuf_ref, *, tc, chunk, nseq):
    tb = pl.program_id(1)
    c = chunk
    cc = c * c

    @pl.when(tb == 0)
    def _():
        st_ref[...] = s0_ref[...]

    row = lax.broadcasted_iota(jnp.int32, (c, 1), 0)
    bd = bd_ref[...]
    lane_head = lax.broadcasted_iota(jnp.int32, (RWKV_HEAD, MXU_TILE), 1) // RWKV_HEAD
    neg = jnp.float32(-1e30)

    def one_chunk(q, r0):
        r = r_ref[q, pl.ds(r0, c), :]
        k = k_ref[q, pl.ds(r0, c), :]
        v = v_ref[q, pl.ds(r0, c), :]
        lw = lw_ref[q, pl.ds(r0, c), :]
        a_sig = a_ref[q, pl.ds(r0, c), :]

        kk = k * kk_ref[...]
        k_eff = k * (1.0 + (a_sig - 1.0) * ka_ref[...])
        kk_ss, bonus_dot = _seg_sums([kk * kk, r * k_eff * rk_ref[...]], bd)
        kk = kk / jnp.maximum(jnp.sqrt(kk_ss), 1e-12)
        a_vec = -kk
        b_vec = kk * a_sig

        lb = _cumsum_rows(lw, row, c)
        lbp = lb - lw
        lb_last = lb[c - 1:c]
        kinds = ([], [], [], [])
        for s in range(c):
            lbs = lb[s:s + 1]
            bs = b_vec[s:s + 1]
            ks = k_eff[s:s + 1]
            a_d = a_vec * jnp.exp(jnp.where(row > s, lbp - lbs, neg))
            r_d = r * jnp.exp(jnp.where(row >= s, lb - lbs, neg))
            kinds[0].append(a_d * bs)
            kinds[1].append(a_d * ks)
            kinds[2].append(r_d * bs)
            kinds[3].append(r_d * ks)
        prod = jnp.concatenate(
            [p[:, t * MXU_TILE:(t + 1) * MXU_TILE]
             for t in range(N_LANE_TILES) for kind in kinds for p in kind], axis=0)
        coef_t = jnp.dot(prod.astype(BF16), bd, preferred_element_type=F32)

        def coef(kind, s):
            return jnp.concatenate(
                [coef_t[(t * 4 + kind) * cc + s * c:(t * 4 + kind) * cc + (s + 1) * c]
                 for t in range(N_LANE_TILES)], axis=1)

        lhs = jnp.concatenate([a_vec * jnp.exp(lbp), r * jnp.exp(lb)], axis=0).astype(BF16)
        x0 = []
        for t in range(N_LANE_TILES):
            ln = slice(t * MXU_TILE, (t + 1) * MXU_TILE)
            st_t = st_ref[q, :, ln].astype(BF16)
            w_t = jnp.concatenate([st_t] * HEADS_PER_TILE, axis=0) * bd
            x0.append(lax.dot_general(lhs[:, ln], w_t, (((1,), (1,)), ((), ())),
                                      preferred_element_type=F32))
        x0 = jnp.concatenate(x0, axis=1)
        u = x0[:c]
        o = x0[c:]

        for s in range(c):
            u = u + coef(1, s) * v[s:s + 1]
        for s in range(c):
            u = u + coef(0, s) * u[s:s + 1]
        for s in range(c):
            o = o + coef(2, s) * u[s:s + 1] + coef(3, s) * v[s:s + 1]

        tail = jnp.exp(lb_last - lb)
        uv = jnp.concatenate([u, v], axis=0).astype(BF16)
        bk = jnp.concatenate([b_vec * tail, k_eff * tail], axis=0).astype(BF16)
        w_last = jnp.exp(lb_last)
        for t in range(N_LANE_TILES):
            ln = slice(t * MXU_TILE, (t + 1) * MXU_TILE)
            zt = _dot_tn(uv[:, ln], bk[:, ln])
            upd = zt[(HEADS_PER_TILE - 1) * RWKV_HEAD:]
            for hh in range(HEADS_PER_TILE - 2, -1, -1):
                upd = jnp.where(lane_head == hh, zt[hh * RWKV_HEAD:(hh + 1) * RWKV_HEAD], upd)
            st_ref[q, :, ln] = w_last[:, ln] * st_ref[q, :, ln] + upd

        inv_n = 1.0 / RWKV_HEAD
        mu = _seg_sums([o], bd)[0] * inv_n
        dev = o - mu
        var = _seg_sums([dev * dev], bd)[0] * inv_n
        yn = dev * lax.rsqrt(var + GN_EPS) * lnw_ref[...] + lnb_ref[...]
        obuf_ref[q, pl.ds(r0, c), :] = (yn + bonus_dot * v) * gate_ref[q, pl.ds(r0, c), :]

    def body(ci, carry):
        r0 = pl.multiple_of(ci * c, c)
        for q in range(nseq):
            one_chunk(q, r0)
        return carry

    lax.fori_loop(0, tc // c, body, 0)
    o_ref[...] = obuf_ref[...].astype(BF16)
    s_ref[...] = st_ref[...]


def _wkv(z3, lw3, a3, gate3, s0t, bd, p, *, tc, chunk, nseq):
    nb, seq, _ = z3.shape
    col0 = RWKV_OFF // D_RWKV
    blk = lambda col: pl.BlockSpec((nseq, tc, D_RWKV), lambda b, t: (b, t, col))
    st_spec = pl.BlockSpec((nseq, RWKV_HEAD, D_RWKV), lambda b, t: (b, 0, 0))
    vec = pl.BlockSpec((1, D_RWKV), lambda b, t: (0, 0))
    return pl.pallas_call(
        functools.partial(_wkv_kernel, tc=tc, chunk=chunk, nseq=nseq),
        grid=(nb // nseq, seq // tc),
        in_specs=[blk(col0), blk(col0 + 1), blk(col0 + 2), blk(0), blk(0), blk(0), st_spec,
                  pl.BlockSpec(bd.shape, lambda b, t: (0, 0)), vec, vec, vec, vec, vec],
        out_specs=[blk(0), st_spec],
        out_shape=[jax.ShapeDtypeStruct((nb, seq, D_RWKV), BF16),
                   jax.ShapeDtypeStruct((nb, RWKV_HEAD, D_RWKV), F32)],
        scratch_shapes=[pltpu.VMEM((nseq, RWKV_HEAD, D_RWKV), F32),
                        pltpu.VMEM((nseq, tc, D_RWKV), F32)],
        compiler_params=_cparams(("arbitrary", "arbitrary")),
        name="wkv",
    )(z3, z3, z3, lw3, a3, gate3, s0t, bd, p['k_k'], p['k_a'], p['r_k'], p['ln_w'], p['ln_b'])


def _out_router_kernel(xp_ref, gp_ref, rp_ref, xs_ref, gs_ref, rs_ref, wo_ref, fg_ref, rw_ref, rb_ref,
                       out_ref, *, tm, tiles_p):
    is_p = pl.program_id(0) < tiles_p
    x = jnp.where(is_p, xp_ref[...], xs_ref[...])
    og = jnp.where(is_p, gp_ref[...], gs_ref[...])
    orr = jnp.where(is_p, rp_ref[...], rs_ref[...])
    mix = (jnp.dot(og, wo_ref[0], preferred_element_type=F32)
           + jnp.dot(orr, wo_ref[1], preferred_element_type=F32))
    x1 = x + mix
    out_ref[:, :D_MODEL] = x1
    h2 = _rms(x1, fg_ref[...])
    logits = _dot(h2, rw_ref[...]) + rb_ref[...]
    lane = lax.broadcasted_iota(jnp.int32, (tm, ROUTE_LANES), 1)
    neg = jnp.float32(-jnp.inf)
    big = jnp.int32(1 << 20)

    def first_argmax(vals):
        mx = jnp.max(vals, axis=-1, keepdims=True)
        idx = jnp.min(jnp.where(vals == mx, lane, big), axis=-1, keepdims=True)
        return mx, idx

    lg = jnp.where(lane < N_GROUPS, logits, neg)
    g_max, g_idx = first_argmax(lg)
    p_group = 1.0 / jnp.sum(jnp.exp(lg - g_max), axis=-1, keepdims=True)
    lo = N_GROUPS + g_idx * EXPERTS_PER_GROUP
    le = jnp.where((lane >= lo) & (lane < lo + EXPERTS_PER_GROUP), logits, neg)
    m1, i1 = first_argmax(le)
    m2, i2 = first_argmax(jnp.where(lane == i1, neg, le))
    e2 = jnp.exp(m2 - m1)
    w1 = 1.0 / (1.0 + e2)
    w2 = e2 / (1.0 + e2)
    route = jnp.where(lane == i1, w1, jnp.where(lane == i2, w2, 0.0)) * p_group
    out_ref[:, D_MODEL:] = jnp.where(lane == ROUTE_LANES - 1, g_idx.astype(F32), route)


def _out_router(xp, gp, rp, xs, gs, rs, wo, ffn_norm, rw, rb, *, tm):
    mp, ms = xp.shape[0], xs.shape[0]
    tiles_p, tiles_s = mp // tm, ms // tm
    rows_p = lambda w: pl.BlockSpec((tm, w), lambda i: (jnp.minimum(i, tiles_p - 1), 0))
    rows_s = lambda w: pl.BlockSpec((tm, w), lambda i: (jnp.maximum(i - tiles_p, 0), 0))
    full = lambda shape: pl.BlockSpec(shape, lambda i: (0,) * len(shape))
    return pl.pallas_call(
        functools.partial(_out_router_kernel, tm=tm, tiles_p=tiles_p),
        grid=(tiles_p + tiles_s,),
        in_specs=[rows_p(D_MODEL), rows_p(D_GLA), rows_p(D_RWKV),
                  rows_s(D_MODEL), rows_s(D_GLA), rows_s(D_RWKV),
                  full(wo.shape), full((1, D_MODEL)), full(rw.shape), full((1, ROUTE_LANES))],
        out_specs=pl.BlockSpec((tm, X1_WIDTH), lambda i: (i, 0)),
        out_shape=jax.ShapeDtypeStruct((mp + ms, X1_WIDTH), F32),
        compiler_params=_cparams(("arbitrary",)),
        name="out_router",
    )(xp, gp, rp, xs, gs, rs, wo, ffn_norm, rw, rb)


def _moe_kernel(tile_group_ref, tile_rows_ref, src_ref,
                x1_hbm, wg_ref, wu_ref, wd_ref, fg_ref, fn_ref, yp_hbm, ys_hbm,
                acc, hbuf, wgb, wub, wdb, gsem, ssem, *, tile, sub, rows_p):
    j = pl.program_id(0)
    e = pl.program_id(1)
    nrows = tile_rows_ref[j]
    base = j * tile
    n_sub = (nrows + sub - 1) // sub

    def gather_copy(rr):
        return pltpu.make_async_copy(x1_hbm.at[pl.ds(src_ref[base + rr], 1)], acc.at[pl.ds(rr, 1)], gsem)

    def scatter_start(rr):
        tok = src_ref[base + rr]
        row = acc.at[pl.ds(rr, 1), pl.ds(0, D_MODEL)]

        @pl.when(tok < rows_p)
        def _():
            pltpu.make_async_copy(row, yp_hbm.at[pl.ds(tok, 1)], ssem).start()

        @pl.when(tok >= rows_p)
        def _():
            pltpu.make_async_copy(row, ys_hbm.at[pl.ds(tok - rows_p, 1)], ssem).start()

    def scatter_wait(rr):
        pltpu.make_async_copy(acc.at[pl.ds(rr, 1), pl.ds(0, D_MODEL)], yp_hbm.at[pl.ds(0, 1)], ssem).wait()

    def for_rows(n_blocks, fn):
        def blk(bi, cy):
            for q in range(DMA_UNROLL):
                fn(bi * DMA_UNROLL + q)
            return cy
        lax.fori_loop(0, n_blocks, blk, 0)

    @pl.when((e == 0) & (nrows > 0))
    def _():
        n_blk = n_sub * (sub // DMA_UNROLL)
        for_rows(n_blk, lambda rr: gather_copy(rr).start())
        for_rows(n_blk, lambda rr: gather_copy(rr).wait())

        def norm(sb, cy):
            rs = pl.ds(pl.multiple_of(sb * sub, sub), sub)
            hbuf[rs, :] = _rms(acc[rs, :D_MODEL], fg_ref[...]).astype(BF16)
            return cy
        lax.fori_loop(0, n_sub, norm, 0)

    @pl.when(nrows > 0)
    def _():
        g = tile_group_ref[j]
        wgb[...] = wg_ref[0, 0].astype(BF16)
        wub[...] = wu_ref[0, 0].astype(BF16)
        wdb[...] = wd_ref[0, 0].astype(BF16)
        lane = lax.broadcasted_iota(jnp.int32, (sub, ROUTE_LANES), 1)
        sel = lane == N_GROUPS + g * EXPERTS_PER_GROUP + e

        def expert(sb, cy):
            rs = pl.ds(pl.multiple_of(sb * sub, sub), sub)
            wcol = jnp.sum(jnp.where(sel, acc[rs, D_MODEL:], 0.0), axis=-1, keepdims=True)
            h2 = hbuf[rs, :]
            gate = jnp.dot(h2, wgb[...], preferred_element_type=F32)
            up = jnp.dot(h2, wub[...], preferred_element_type=F32)
            hid = (gate * jax.nn.sigmoid(gate)) * up * wcol
            acc[rs, :D_MODEL] += jnp.dot(hid.astype(BF16), wdb[...], preferred_element_type=F32)
            return cy
        lax.fori_loop(0, n_sub, expert, 0)

    @pl.when((e == EXPERTS_PER_GROUP - 1) & (nrows > 0))
    def _():
        def final(sb, cy):
            rs = pl.ds(pl.multiple_of(sb * sub, sub), sub)
            acc[rs, :D_MODEL] = _rms(acc[rs, :D_MODEL], fn_ref[...])
            return cy
        lax.fori_loop(0, n_sub, final, 0)

        n_full = nrows // DMA_UNROLL
        for_rows(n_full, scatter_start)

        def rest_start(rr, cy):
            scatter_start(rr)
            return cy
        lax.fori_loop(n_full * DMA_UNROLL, nrows, rest_start, 0)
        for_rows(n_full, scatter_wait)

        def rest_wait(rr, cy):
            scatter_wait(rr)
            return cy
        lax.fori_loop(n_full * DMA_UNROLL, nrows, rest_wait, 0)


def _moe(x1e, tile_group, tile_rows, src, wg, wu, wd, ffn_norm, final_norm, *, tile, sub, rows_p):
    m = x1e.shape[0]
    n_tiles = tile_group.shape[0]

    def w_idx(j, e, tg, tr, sr):
        return (tg[j], jnp.where(tr[j] > 0, e, EXPERTS_PER_GROUP - 1), 0, 0)

    vec = pl.BlockSpec((1, D_MODEL), lambda j, e, tg, tr, sr: (0, 0))
    grid_spec = pltpu.PrefetchScalarGridSpec(
        num_scalar_prefetch=3,
        grid=(n_tiles, EXPERTS_PER_GROUP),
        in_specs=[pl.BlockSpec(memory_space=pl.ANY),
                  pl.BlockSpec((1, 1, D_MODEL, D_EXPERT), w_idx),
                  pl.BlockSpec((1, 1, D_MODEL, D_EXPERT), w_idx),
                  pl.BlockSpec((1, 1, D_EXPERT, D_MODEL), w_idx),
                  vec, vec],
        out_specs=[pl.BlockSpec(memory_space=pl.ANY), pl.BlockSpec(memory_space=pl.ANY)],
        scratch_shapes=[pltpu.VMEM((tile, X1_WIDTH), F32),
                        pltpu.VMEM((tile, D_MODEL), BF16),
                        pltpu.VMEM((D_MODEL, D_EXPERT), BF16),
                        pltpu.VMEM((D_MODEL, D_EXPERT), BF16),
                        pltpu.VMEM((D_EXPERT, D_MODEL), BF16),
                        pltpu.SemaphoreType.DMA(()),
                        pltpu.SemaphoreType.DMA(())],
    )
    return pl.pallas_call(
        functools.partial(_moe_kernel, tile=tile, sub=sub, rows_p=rows_p),
        grid_spec=grid_spec,
        out_shape=[jax.ShapeDtypeStruct((rows_p, D_MODEL), F32),
                   jax.ShapeDtypeStruct((m - rows_p, D_MODEL), F32)],
        compiler_params=_cparams(("arbitrary", "arbitrary")),
        name="moe",
    )(tile_group, tile_rows, src, x1e, wg, wu, wd, ffn_norm, final_norm)


def _moe_plan(group_id, tile):
    m = group_id.shape[0]
    n_tiles = m // tile + N_GROUPS
    order = jnp.argsort(group_id, stable=True).astype(jnp.int32)
    counts = jnp.sum(group_id[:, None] == jnp.arange(N_GROUPS, dtype=jnp.int32)[None, :], axis=0).astype(jnp.int32)
    tiles_per = (counts + tile - 1) // tile
    tile_end = jnp.cumsum(tiles_per)
    tile_start = tile_end - tiles_per
    row_start = jnp.cumsum(counts) - counts
    t = jnp.arange(n_tiles, dtype=jnp.int32)
    used = t < tile_end[-1]
    grp = jnp.minimum(jnp.sum(t[:, None] >= tile_end[None, :], axis=1), N_GROUPS - 1).astype(jnp.int32)
    last_group = jnp.max(jnp.where(counts > 0, jnp.arange(N_GROUPS, dtype=jnp.int32), 0))
    tile_group = jnp.where(used, grp, last_group).astype(jnp.int32)
    local = (t - tile_start[grp]) * tile
    tile_rows = jnp.where(used, jnp.clip(counts[grp] - local, 0, tile), 0).astype(jnp.int32)
    slot = jnp.arange(n_tiles * tile, dtype=jnp.int32)
    st = slot // tile
    within = local[st] + (slot - st * tile)
    g_of = tile_group[st]
    pos = row_start[g_of] + jnp.minimum(within, jnp.maximum(counts[g_of] - 1, 0))
    src = order[jnp.clip(pos, 0, m - 1)]
    return tile_group, tile_rows, src.astype(jnp.int32)


def _prep_params(mix_norm, w_in, gla_gate_w1, gla_gate_w2, gla_gate_b, gla_norm, rwkv_mu_rkv, rwkv_mu_wag,
                 rwkv_w0, rwkv_w1, rwkv_w2, rwkv_a0, rwkv_a1, rwkv_a2, rwkv_g1, rwkv_g2,
                 rwkv_k_k, rwkv_k_a, rwkv_r_k, rwkv_ln_w, rwkv_ln_b, w_out, ffn_norm,
                 router_group_w, router_group_b, router_expert_w, router_expert_b, final_norm):
    row = lambda v: v.reshape(1, -1).astype(F32)
    g_rank = 2 * LANE
    p = dict(
        mix_norm=row(mix_norm[0]),
        mu_wag=rwkv_mu_wag[0],
        w_in=w_in[0].astype(BF16),
        gw1=_pad_axis(gla_gate_w1[0], 1, LANE).astype(BF16),
        gw2=_pad_axis(gla_gate_w2[0], 0, LANE).astype(BF16),
        gb=row(gla_gate_b[0]),
        w1=_pad_axis(rwkv_w1[0], 1, LANE).astype(BF16),
        w2=_pad_axis(rwkv_w2[0], 0, LANE).astype(BF16),
        w0=row(rwkv_w0[0]),
        a1=_pad_axis(rwkv_a1[0], 1, LANE).astype(BF16),
        a2=_pad_axis(rwkv_a2[0], 0, LANE).astype(BF16),
        a0=row(rwkv_a0[0]),
        g1=_pad_axis(rwkv_g1[0], 1, g_rank).astype(BF16),
        g2=_pad_axis(rwkv_g2[0], 0, g_rank).astype(BF16),
        gla_norm=row(gla_norm[0]),
        mu_ext=jnp.concatenate([jnp.zeros((1, RWKV_OFF), F32), rwkv_mu_rkv[0].reshape(1, -1)], axis=1),
        k_k=row(rwkv_k_k[0]), k_a=row(rwkv_k_a[0]), r_k=row(rwkv_r_k[0]),
        ln_w=row(rwkv_ln_w[0]), ln_b=row(rwkv_ln_b[0]),
        wo=w_out[0].astype(BF16).reshape(2, D_GLA, D_MODEL),
        ffn_norm=row(ffn_norm[0]),
        router_w=_pad_axis(jnp.concatenate([router_group_w[0], router_expert_w[0]], axis=1), 1,
                           ROUTE_LANES).astype(BF16),
        router_b=_pad_axis(jnp.concatenate([router_group_b[0], router_expert_b[0]]).reshape(1, -1), 1,
                           ROUTE_LANES).astype(F32),
        final_norm=row(final_norm),
    )
    r = jnp.arange(MXU_TILE)
    p['gla_e'] = (r[:, None] // GLA_DK == jnp.arange(2 * GLA_DV)[None, :] // GLA_DV).astype(BF16)
    p['wkv_bd'] = (r[:, None] // RWKV_HEAD == r[None, :] // RWKV_HEAD).astype(BF16)
    return p


def _mixer(x, s_gla, s_wkv, s_shift, p, *, tm, tc):
    nb, seq, _ = x.shape
    m = nb * seq
    x2 = x.reshape(m, D_MODEL)
    multi_seq = tm > seq

    h_last = jnp.zeros((nb, D_MODEL), F32) if s_shift is None else s_shift
    zl = _plain_proj(_pad_axis(h_last, 0, -(-nb // 16) * 16).astype(BF16), p['w_in'], tn=1024)[:nb]
    if multi_seq:
        expand = lambda a: jnp.repeat(a, seq, axis=0)
        hprev, zprev = expand(h_last), expand(zl)
    else:
        hprev, zprev = h_last.reshape(nb, 1, D_MODEL), zl.reshape(nb, 1, D_IN)

    h_bf, la, lw, a_sig, gate, hlast = _norm_proj(x2, hprev, p, seq=seq, tm=tm)
    shift_new = hlast.reshape(nb, seq, D_MODEL)[:, -1] if multi_seq else hlast.reshape(nb, D_MODEL)
    z = _in_proj(h_bf, p['w_in'], p['mu_ext'], zprev, seq=seq, tm=tm, tn=1024)

    z3 = z.reshape(nb, seq, D_IN)
    r3 = lambda a: a.reshape(nb, seq, a.shape[-1])
    if s_gla is None:
        g0 = jnp.zeros((nb, GLA_DV, GLA_K), F32)
        w0 = jnp.zeros((nb, RWKV_HEAD, D_RWKV), F32)
    else:
        g0 = s_gla.transpose(0, 3, 1, 2).reshape(nb, GLA_DV, GLA_K)
        w0 = s_wkv.transpose(0, 2, 1, 3).reshape(nb, RWKV_HEAD, D_RWKV)
    og, g_new = _gla(z3, r3(la), g0, p['gla_e'], p['gla_norm'], tc=tc, chunk=min(GLA_CHUNK, seq))
    wkv_seqs = 4 if nb % 4 == 0 else 1
    orr, w_new = _wkv(z3, r3(lw), r3(a_sig), r3(gate), w0, p['wkv_bd'], p, tc=min(tc, 512 // wkv_seqs),
                      chunk=min(WKV_CHUNK, seq), nseq=wkv_seqs)
    gla_new = g_new.reshape(nb, GLA_DV, GLA_HEADS, GLA_DK).transpose(0, 2, 3, 1)
    wkv_new = w_new.reshape(nb, RWKV_HEAD, RWKV_HEADS, RWKV_HEAD).transpose(0, 2, 1, 3)
    return (x2, og.reshape(m, D_GLA), orr.reshape(m, D_RWKV)), (gla_new, wkv_new, shift_new)


def _ffn(rows_p, rows_s, p, wg, wu, wd, *, tm, moe_tile, moe_sub):
    mp = rows_p[0].shape[0]
    x1e = _out_router(*rows_p, *rows_s, p['wo'], p['ffn_norm'], p['router_w'], p['router_b'], tm=tm)
    group_id = x1e[:, X1_WIDTH - 1].astype(jnp.int32)
    tile_group, tile_rows, src = _moe_plan(group_id, moe_tile)
    return _moe(x1e, tile_group, tile_rows, src, wg, wu, wd, p['ffn_norm'], p['final_norm'],
                tile=moe_tile, sub=moe_sub, rows_p=mp)


def kernel(x_prompt, x_sample, state_gla, state_wkv, state_shift, mix_norm, w_in, gla_gate_w1, gla_gate_w2, gla_gate_b, gla_norm, rwkv_mu_rkv, rwkv_mu_wag, rwkv_w0, rwkv_w1, rwkv_w2, rwkv_a0, rwkv_a1, rwkv_a2, rwkv_g1, rwkv_g2, rwkv_k_k, rwkv_k_a, rwkv_r_k, rwkv_ln_w, rwkv_ln_b, w_out, ffn_norm, router_group_w, router_group_b, router_expert_w, router_expert_b, expert_w_gate, expert_w_up, expert_w_down, final_norm):
    p = _prep_params(mix_norm, w_in, gla_gate_w1, gla_gate_w2, gla_gate_b, gla_norm, rwkv_mu_rkv, rwkv_mu_wag,
                     rwkv_w0, rwkv_w1, rwkv_w2, rwkv_a0, rwkv_a1, rwkv_a2, rwkv_g1, rwkv_g2,
                     rwkv_k_k, rwkv_k_a, rwkv_r_k, rwkv_ln_w, rwkv_ln_b, w_out, ffn_norm,
                     router_group_w, router_group_b, router_expert_w, router_expert_b, final_norm)
    wg, wu, wd = expert_w_gate[0], expert_w_up[0], expert_w_down[0]
    nb_p, seq_p, _ = x_prompt.shape
    nb_s, seq_s, _ = x_sample.shape
    rows_p, (gla_p, wkv_p, shift_p) = _mixer(x_prompt, None, None, None, p, tm=256, tc=min(256, seq_p))
    rows_s, (gla_s, wkv_s, shift_s) = _mixer(x_sample, state_gla[0], state_wkv[0], state_shift[0], p,
                                             tm=nb_s * seq_s, tc=seq_s)
    y_p, y_s = _ffn(rows_p, rows_s, p, wg, wu, wd, tm=256, moe_tile=5 * MOE_SUB, moe_sub=MOE_SUB)
    return (y_p.reshape(nb_p, seq_p, D_MODEL), y_s.reshape(nb_s, seq_s, D_MODEL),
            gla_p[None], wkv_p[None], shift_p[None], gla_s[None], wkv_s[None], shift_s[None])
```

```python
import functools

import jax
import jax.numpy as jnp
from jax import lax
from jax.experimental import pallas as pl
from jax.experimental.pallas import tpu as pltpu

F32 = jnp.float32
BF16 = jnp.bfloat16

D_MODEL = 2048
D_GLA = 1024
D_RWKV = 1024
GLA_HEADS = 4
GLA_DV = 256
GLA_DK = 128
GLA_K = 512
GLA_GATE_TEMP = 16.0
RWKV_HEAD = 64
RWKV_HEADS = 16
N_GROUPS = 8
EXPERTS_PER_GROUP = 8
D_EXPERT = 512
NORM_EPS = 1e-6
GN_EPS = 64e-5
RWKV_OFF = 2 * GLA_K + 2 * D_GLA
D_IN = RWKV_OFF + 3 * D_RWKV

LANE = 128
SUBLANE = 8
MXU_TILE = 256
VMEM_LIMIT = 56 * 1024 * 1024

GLA_CHUNK = 16
WKV_CHUNK = 8
MOE_SUB = 256
ROUTE_LANES = 128
X1_WIDTH = D_MODEL + ROUTE_LANES
DMA_UNROLL = 8


def _cparams(sem):
    return pltpu.CompilerParams(dimension_semantics=sem, vmem_limit_bytes=VMEM_LIMIT)


def _rms(x, gain):
    return x * lax.rsqrt(jnp.mean(x * x, axis=-1, keepdims=True) + NORM_EPS) * gain


def _softplus(y):
    return jnp.maximum(y, 0.0) + jnp.log1p(jnp.exp(-jnp.abs(y)))


def _dot(a, b):
    return jnp.dot(a.astype(BF16), b.astype(BF16), preferred_element_type=F32)


def _dot_nt(a, b):
    return lax.dot_general(a.astype(BF16), b.astype(BF16), (((1,), (1,)), ((), ())),
                           preferred_element_type=F32)


def _dot_tn(a, b):
    return lax.dot_general(a.astype(BF16), b.astype(BF16), (((0,), (0,)), ((), ())),
                           preferred_element_type=F32)


def _norm_proj_kernel(x_ref, hprev_ref, gain_ref, mu_ref, gw1_ref, gw2_ref, gb_ref,
                      w1_ref, w2_ref, w0_ref, a1_ref, a2_ref, a0_ref, g1_ref, g2_ref,
                      h_ref, la_ref, lw_ref, a_ref, gate_ref, hlast_ref, carry_ref,
                      *, tm, seq, multi_seq):
    i = pl.program_id(0)
    h = _rms(x_ref[...], gain_ref[...])
    row = lax.broadcasted_iota(jnp.int32, (tm, 1), 0)
    rolled = pltpu.roll(h, 1, 0)
    if multi_seq:
        h_shift = jnp.where(row % seq == 0, hprev_ref[...], rolled)
        hlast_ref[...] = h
    else:
        tiles_per_seq = seq // tm
        first = (i % tiles_per_seq) == 0
        prev = jnp.where(first, hprev_ref[0], carry_ref[...])
        h_shift = jnp.where(row == 0, prev, rolled)
        carry_ref[...] = h[tm - 1:tm]
        hlast_ref[0] = h[tm - 1:tm]
    h_ref[...] = h.astype(BF16)

    dx = h_shift - h
    mu = mu_ref[...]
    xw = h + dx * mu[0:1]
    xa = h + dx * mu[1:2]
    xg = h + dx * mu[2:3]

    gl = _dot(_dot(h, gw1_ref[...]), gw2_ref[...]) + gb_ref[...]
    la_ref[...] = -_softplus(-gl) * (1.0 / GLA_GATE_TEMP)

    u = w0_ref[...] + _dot(jnp.tanh(_dot(xw, w1_ref[...])), w2_ref[...])
    w_log = -_softplus(-u) - 0.5
    lw_ref[...] = -jnp.exp(w_log)

    a_ref[...] = jax.nn.sigmoid(a0_ref[...] + _dot(_dot(xa, a1_ref[...]), a2_ref[...]))
    gate_ref[...] = _dot(jax.nn.sigmoid(_dot(xg, g1_ref[...])), g2_ref[...])


def _pad_axis(w, axis, to):
    pad = [(0, 0)] * w.ndim
    pad[axis] = (0, to - w.shape[axis])
    return jnp.pad(w, pad)


def _norm_proj(x2, hprev, p, *, seq, tm):
    m = x2.shape[0]
    multi_seq = tm > seq
    nb = m // seq
    full = lambda shape: pl.BlockSpec(shape, lambda i: (0,) * len(shape))
    rows = lambda w: pl.BlockSpec((tm, w), lambda i: (i, 0))
    if multi_seq:
        assert tm == m
        hprev_spec = rows(D_MODEL)
        hlast_shape = jax.ShapeDtypeStruct((m, D_MODEL), F32)
        hlast_spec = rows(D_MODEL)
    else:
        assert seq % tm == 0
        tps = seq // tm
        hprev_spec = pl.BlockSpec((1, 1, D_MODEL), lambda i: (i // tps, 0, 0))
        hlast_shape = jax.ShapeDtypeStruct((nb, 1, D_MODEL), F32)
        hlast_spec = pl.BlockSpec((1, 1, D_MODEL), lambda i: (i // tps, 0, 0))
    weights = [p['gw1'], p['gw2'], p['gb'], p['w1'], p['w2'], p['w0'], p['a1'], p['a2'], p['a0'],
               p['g1'], p['g2']]
    return pl.pallas_call(
        functools.partial(_norm_proj_kernel, tm=tm, seq=seq, multi_seq=multi_seq),
        grid=(m // tm,),
        in_specs=[rows(D_MODEL), hprev_spec, full((1, D_MODEL)), full((3, D_MODEL))]
                 + [full(w.shape) for w in weights],
        out_specs=[rows(D_MODEL), rows(GLA_K), rows(D_RWKV), rows(D_RWKV), rows(D_RWKV), hlast_spec],
        out_shape=[jax.ShapeDtypeStruct((m, D_MODEL), BF16),
                   jax.ShapeDtypeStruct((m, GLA_K), F32),
                   jax.ShapeDtypeStruct((m, D_RWKV), F32),
                   jax.ShapeDtypeStruct((m, D_RWKV), F32),
                   jax.ShapeDtypeStruct((m, D_RWKV), F32),
                   hlast_shape],
        scratch_shapes=[pltpu.VMEM((1, D_MODEL), F32)],
        compiler_params=_cparams(("arbitrary",)),
        name="norm_proj",
    )(x2, hprev, p['mix_norm'], p['mu_wag'], *weights)


def _in_proj_kernel(h_ref, w_ref, mu_ref, zprev_ref, z_ref, carry_ref, *, tm, seq, multi_seq):
    i = pl.program_id(1)
    z = jnp.dot(h_ref[...], w_ref[...], preferred_element_type=F32)
    row = lax.broadcasted_iota(jnp.int32, (tm, 1), 0)
    rolled = pltpu.roll(z, 1, 0)
    if multi_seq:
        z_prev = jnp.where(row % seq == 0, zprev_ref[...], rolled)
    else:
        tiles_per_seq = seq // tm
        first = (i % tiles_per_seq) == 0
        prev = jnp.where(first, zprev_ref[0], carry_ref[...])
        z_prev = jnp.where(row == 0, prev, rolled)
        carry_ref[...] = z[tm - 1:tm]
    z_ref[...] = z + mu_ref[...] * (z_prev - z)


def _in_proj(h_bf, w_in, mu_ext, zprev, *, seq, tm, tn):
    m = h_bf.shape[0]
    multi_seq = tm > seq
    if multi_seq:
        zprev_spec = pl.BlockSpec((tm, tn), lambda j, i: (i, j))
    else:
        tps = seq // tm
        zprev_spec = pl.BlockSpec((1, 1, tn), lambda j, i: (i // tps, 0, j))
    return pl.pallas_call(
        functools.partial(_in_proj_kernel, tm=tm, seq=seq, multi_seq=multi_seq),
        grid=(D_IN // tn, m // tm),
        in_specs=[pl.BlockSpec((tm, D_MODEL), lambda j, i: (i, 0)),
                  pl.BlockSpec((D_MODEL, tn), lambda j, i: (0, j)),
                  pl.BlockSpec((1, tn), lambda j, i: (0, j)),
                  zprev_spec],
        out_specs=pl.BlockSpec((tm, tn), lambda j, i: (i, j)),
        out_shape=jax.ShapeDtypeStruct((m, D_IN), F32),
        scratch_shapes=[pltpu.VMEM((1, tn), F32)],
        compiler_params=_cparams(("arbitrary", "arbitrary")),
        name="in_proj",
    )(h_bf, w_in, mu_ext, zprev)


def _plain_proj_kernel(h_ref, w_ref, z_ref):
    z_ref[...] = jnp.dot(h_ref[...], w_ref[...], preferred_element_type=F32)


def _plain_proj(h_bf, w_in, *, tn):
    m = h_bf.shape[0]
    return pl.pallas_call(
        _plain_proj_kernel,
        grid=(D_IN // tn,),
        in_specs=[pl.BlockSpec((m, D_MODEL), lambda j: (0, 0)),
                  pl.BlockSpec((D_MODEL, tn), lambda j: (0, j))],
        out_specs=pl.BlockSpec((m, tn), lambda j: (0, j)),
        out_shape=jax.ShapeDtypeStruct((m, D_IN), F32),
        compiler_params=_cparams(("arbitrary",)),
        name="prev_row_proj",
    )(h_bf, w_in)


def _gla_kernel(q_ref, k_ref, v_ref, g_ref, la_ref, s0_ref, e_ref, gn_ref,
                o_ref, s_ref, st_ref, obuf_ref, *, tc, chunk, nseq):
    tb = pl.program_id(1)
    c = chunk

    @pl.when(tb == 0)
    def _():
        for q in range(nseq):
            for hh in range(GLA_HEADS):
                st_ref[q, :, hh * GLA_DK:(hh + 1) * GLA_DK] = s0_ref[q, hh].T

    row = lax.broadcasted_iota(jnp.int32, (c, 1), 0)
    e_mat = e_ref[...]
    neg = jnp.float32(-1e30)

    def one_chunk(q, r0):
        qq = q_ref[q, pl.ds(r0, c), :] * (GLA_DK ** -0.5)
        k = k_ref[q, pl.ds(r0, c), :]
        v = v_ref[q, pl.ds(r0, c), :]
        la = la_ref[q, pl.ds(r0, c), :]
        b = jnp.zeros_like(la)
        for s in range(c):
            b = b + jnp.where(row >= s, la[s:s + 1], 0.0)
        b_last = b[c - 1:c]

        parts = []
        for s in range(c):
            dec = jnp.exp(jnp.where(row >= s, b - b[s:s + 1], neg))
            parts.append((qq * k[s:s + 1] * dec).astype(BF16))
        pcat = jnp.concatenate(parts, axis=0)
        sc = jnp.concatenate(
            [jnp.dot(pcat[:, hp * MXU_TILE:(hp + 1) * MXU_TILE], e_mat, preferred_element_type=F32)
             for hp in range(GLA_K // MXU_TILE)], axis=1)
        o = jnp.zeros((c, D_GLA), F32)
        for s in range(c):
            o = o + sc[s * c:(s + 1) * c] * v[s:s + 1]

        qd = qq * jnp.exp(b)
        kd = k * jnp.exp(b_last - b)
        a_last = jnp.exp(b_last)
        o_heads = []
        for hh in range(GLA_HEADS):
            dk = slice(hh * GLA_DK, (hh + 1) * GLA_DK)
            dv = slice(hh * GLA_DV, (hh + 1) * GLA_DV)
            st = st_ref[q, :, dk]
            oh = o[:, dv] + _dot_nt(qd[:, dk], st)
            st_ref[q, :, dk] = a_last[:, dk] * st + _dot_tn(v[:, dv], kd[:, dk])
            oh = oh * lax.rsqrt(jnp.mean(oh * oh, axis=-1, keepdims=True) + NORM_EPS)
            o_heads.append(oh)
        og = jnp.concatenate(o_heads, axis=1)
        g = g_ref[q, pl.ds(r0, c), :]
        obuf_ref[q, pl.ds(r0, c), :] = og * gn_ref[...] * (g * jax.nn.sigmoid(g))

    def body(ci, carry):
        r0 = pl.multiple_of(ci * c, c)
        for q in range(nseq):
            one_chunk(q, r0)
        return carry

    lax.fori_loop(0, tc // c, body, 0)
    o_ref[...] = obuf_ref[...].astype(BF16)

    @pl.when(tb == pl.num_programs(1) - 1)
    def _():
        for q in range(nseq):
            for hh in range(GLA_HEADS):
                s_ref[q, hh] = st_ref[q, :, hh * GLA_DK:(hh + 1) * GLA_DK].T


def _gla(z3, la3, s0, e_mat, gla_norm, *, tc, chunk, nseq):
    nb, seq, _ = z3.shape
    blk = lambda w, col: pl.BlockSpec((nseq, tc, w), lambda b, t: (b, t, col))
    st_spec = pl.BlockSpec((nseq, GLA_HEADS, GLA_DK, GLA_DV), lambda b, t: (b, 0, 0, 0))
    return pl.pallas_call(
        functools.partial(_gla_kernel, tc=tc, chunk=chunk, nseq=nseq),
        grid=(nb // nseq, seq // tc),
        in_specs=[blk(GLA_K, 0), blk(GLA_K, 1), blk(D_GLA, 1), blk(D_GLA, 2), blk(GLA_K, 0),
                  st_spec,
                  pl.BlockSpec(e_mat.shape, lambda b, t: (0, 0)),
                  pl.BlockSpec((1, D_GLA), lambda b, t: (0, 0))],
        out_specs=[blk(D_GLA, 0), st_spec],
        out_shape=[jax.ShapeDtypeStruct((nb, seq, D_GLA), BF16),
                   jax.ShapeDtypeStruct((nb, GLA_HEADS, GLA_DK, GLA_DV), F32)],
        scratch_shapes=[pltpu.VMEM((nseq, GLA_DV, GLA_K), F32), pltpu.VMEM((nseq, tc, D_GLA), F32)],
        compiler_params=_cparams(("arbitrary", "arbitrary")),
        name="gla",
    )(z3, z3, z3, z3, la3, s0, e_mat, gla_norm)


N_LANE_TILES = D_RWKV // MXU_TILE
HEADS_PER_TILE = MXU_TILE // RWKV_HEAD


def _stack_tiles(x):
    return jnp.concatenate([x[:, t * MXU_TILE:(t + 1) * MXU_TILE] for t in range(N_LANE_TILES)], axis=0)


def _unstack_tiles(y, r):
    return jnp.concatenate([y[t * r:(t + 1) * r] for t in range(N_LANE_TILES)], axis=1)


def _seg_sums(xs, bd):
    c = xs[0].shape[0]
    st = jnp.concatenate([_stack_tiles(x) for x in xs], axis=0)
    hi = st.astype(BF16)
    lo = (st - hi.astype(F32)).astype(BF16)
    res = jnp.dot(jnp.concatenate([hi, lo], axis=0), bd, preferred_element_type=F32)
    half = st.shape[0]
    res = res[:half] + res[half:]
    rows = N_LANE_TILES * c
    return [_unstack_tiles(res[n * rows:(n + 1) * rows], c) for n in range(len(xs))]


def _cumsum_rows(x, row, c):
    if c == SUBLANE:
        step = 1
        while step < c:
            x = x + jnp.where(row >= step, pltpu.roll(x, step, 0), 0.0)
            step *= 2
        return x
    out = jnp.zeros_like(x)
    for s in range(c):
        out = out + jnp.where(row >= s, x[s:s + 1], 0.0)
    return out


def _wkv_kernel(r_ref, k_ref, v_ref, lw_ref, a_ref, gate_ref, s0_ref, bd_ref,
                kk_ref, ka_ref, rk_ref, lnw_ref, lnb_ref,
                o_ref, s_ref, st_ref, obuf_ref, *, tc, chunk, nseq):
    tb = pl.program_id(1)
    c = chunk
    cc = c * c

    @pl.when(tb == 0)
    def _():
        for q in range(nseq):
            st_ref[q] = jnp.concatenate([s0_ref[q, hh] for hh in range(RWKV_HEADS)], axis=1)

    row = lax.broadcasted_iota(jnp.int32, (c, 1), 0)
    bd = bd_ref[...]
    lane_head = lax.broadcasted_iota(jnp.int32, (RWKV_HEAD, MXU_TILE), 1) // RWKV_HEAD
    neg = jnp.float32(-1e30)

    def one_chunk(q, r0):
        r = r_ref[q, pl.ds(r0, c), :]
        k = k_ref[q, pl.ds(r0, c), :]
        v = v_ref[q, pl.ds(r0, c), :]
        lw = lw_ref[q, pl.ds(r0, c), :]
        a_sig = a_ref[q, pl.ds(r0, c), :]

        kk = k * kk_ref[...]
        k_eff = k * (1.0 + (a_sig - 1.0) * ka_ref[...])
        kk_ss, bonus_dot = _seg_sums([kk * kk, r * k_eff * rk_ref[...]], bd)
        kk = kk / jnp.maximum(jnp.sqrt(kk_ss), 1e-12)
        a_vec = -kk
        b_vec = kk * a_sig

        lb = _cumsum_rows(lw, row, c)
        lbp = lb - lw
        lb_last = lb[c - 1:c]
        kinds = ([], [], [], [])
        for s in range(c):
            lbs = lb[s:s + 1]
            bs = b_vec[s:s + 1]
            ks = k_eff[s:s + 1]
            a_d = a_vec * jnp.exp(jnp.where(row > s, lbp - lbs, neg))
            r_d = r * jnp.exp(jnp.where(row >= s, lb - lbs, neg))
            kinds[0].append(a_d * bs)
            kinds[1].append(a_d * ks)
            kinds[2].append(r_d * bs)
            kinds[3].append(r_d * ks)
        prod = jnp.concatenate(
            [p[:, t * MXU_TILE:(t + 1) * MXU_TILE]
             for t in range(N_LANE_TILES) for kind in kinds for p in kind], axis=0)
        coef_t = jnp.dot(prod.astype(BF16), bd, preferred_element_type=F32)

        def coef(kind, s):
            return jnp.concatenate(
                [coef_t[(t * 4 + kind) * cc + s * c:(t * 4 + kind) * cc + (s + 1) * c]
                 for t in range(N_LANE_TILES)], axis=1)

        lhs = jnp.concatenate([a_vec * jnp.exp(lbp), r * jnp.exp(lb)], axis=0).astype(BF16)
        x0 = []
        for t in range(N_LANE_TILES):
            ln = slice(t * MXU_TILE, (t + 1) * MXU_TILE)
            st_t = st_ref[q, :, ln].astype(BF16)
            w_t = jnp.concatenate([st_t] * HEADS_PER_TILE, axis=0) * bd
            x0.append(lax.dot_general(lhs[:, ln], w_t, (((1,), (1,)), ((), ())),
                                      preferred_element_type=F32))
        x0 = jnp.concatenate(x0, axis=1)
        u = x0[:c]
        o = x0[c:]

        for s in range(c):
            u = u + coef(1, s) * v[s:s + 1]
        for s in range(c):
            u = u + coef(0, s) * u[s:s + 1]
        for s in range(c):
            o = o + coef(2, s) * u[s:s + 1] + coef(3, s) * v[s:s + 1]

        tail = jnp.exp(lb_last - lb)
        uv = jnp.concatenate([u, v], axis=0).astype(BF16)
        bk = jnp.concatenate([b_vec * tail, k_eff * tail], axis=0).astype(BF16)
        w_last = jnp.exp(lb_last)
        for t in range(N_LANE_TILES):
            ln = slice(t * MXU_TILE, (t + 1) * MXU_TILE)
            zt = _dot_tn(uv[:, ln], bk[:, ln])
            upd = zt[(HEADS_PER_TILE - 1) * RWKV_HEAD:]
            for hh in range(HEADS_PER_TILE - 2, -1, -1):
                upd = jnp.where(lane_head == hh, zt[hh * RWKV_HEAD:(hh + 1) * RWKV_HEAD], upd)
            st_ref[q, :, ln] = w_last[:, ln] * st_ref[q, :, ln] + upd

        inv_n = 1.0 / RWKV_HEAD
        mu = _seg_sums([o], bd)[0] * inv_n
        dev = o - mu
        var = _seg_sums([dev * dev], bd)[0] * inv_n
        yn = dev * lax.rsqrt(var + GN_EPS) * lnw_ref[...] + lnb_ref[...]
        obuf_ref[q, pl.ds(r0, c), :] = (yn + bonus_dot * v) * gate_ref[q, pl.ds(r0, c), :]

    def body(ci, carry):
        r0 = pl.multiple_of(ci * c, c)
        for q in range(nseq):
            one_chunk(q, r0)
        return carry

    lax.fori_loop(0, tc // c, body, 0)
    o_ref[...] = obuf_ref[...].astype(BF16)

    @pl.when(tb == pl.num_programs(1) - 1)
    def _():
        for q in range(nseq):
            for hh in range(RWKV_HEADS):
                s_ref[q, hh] = st_ref[q, :, hh * RWKV_HEAD:(hh + 1) * RWKV_HEAD]


def _wkv(z3, lw3, a3, gate3, s0, bd, p, *, tc, chunk, nseq):
    nb, seq, _ = z3.shape
    col0 = RWKV_OFF // D_RWKV
    blk = lambda col: pl.BlockSpec((nseq, tc, D_RWKV), lambda b, t: (b, t, col))
    st_spec = pl.BlockSpec((nseq, RWKV_HEADS, RWKV_HEAD, RWKV_HEAD), lambda b, t: (b, 0, 0, 0))
    vec = pl.BlockSpec((1, D_RWKV), lambda b, t: (0, 0))
    return pl.pallas_call(
        functools.partial(_wkv_kernel, tc=tc, chunk=chunk, nseq=nseq),
        grid=(nb // nseq, seq // tc),
        in_specs=[blk(col0), blk(col0 + 1), blk(col0 + 2), blk(0), blk(0), blk(0), st_spec,
                  pl.BlockSpec(bd.shape, lambda b, t: (0, 0)), vec, vec, vec, vec, vec],
        out_specs=[blk(0), st_spec],
        out_shape=[jax.ShapeDtypeStruct((nb, seq, D_RWKV), BF16),
                   jax.ShapeDtypeStruct((nb, RWKV_HEADS, RWKV_HEAD, RWKV_HEAD), F32)],
        scratch_shapes=[pltpu.VMEM((nseq, RWKV_HEAD, D_RWKV), F32),
                        pltpu.VMEM((nseq, tc, D_RWKV), F32)],
        compiler_params=_cparams(("arbitrary", "arbitrary")),
        name="wkv",
    )(z3, z3, z3, lw3, a3, gate3, s0, bd, p['k_k'], p['k_a'], p['r_k'], p['ln_w'], p['ln_b'])


def _out_router_kernel(xp_ref, gp_ref, rp_ref, xs_ref, gs_ref, rs_ref, wo_ref, fg_ref, rw_ref, rb_ref,
                       out_ref, *, tm, tiles_p):
    is_p = pl.program_id(0) < tiles_p
    x = jnp.where(is_p, xp_ref[...], xs_ref[...])
    og = jnp.where(is_p, gp_ref[...], gs_ref[...])
    orr = jnp.where(is_p, rp_ref[...], rs_ref[...])
    mix = (jnp.dot(og, wo_ref[0], preferred_element_type=F32)
           + jnp.dot(orr, wo_ref[1], preferred_element_type=F32))
    x1 = x + mix
    out_ref[:, :D_MODEL] = x1
    h2 = _rms(x1, fg_ref[...])
    logits = _dot(h2, rw_ref[...]) + rb_ref[...]
    lane = lax.broadcasted_iota(jnp.int32, (tm, ROUTE_LANES), 1)
    neg = jnp.float32(-jnp.inf)
    big = jnp.int32(1 << 20)

    def first_argmax(vals):
        mx = jnp.max(vals, axis=-1, keepdims=True)
        idx = jnp.min(jnp.where(vals == mx, lane, big), axis=-1, keepdims=True)
        return mx, idx

    lg = jnp.where(lane < N_GROUPS, logits, neg)
    g_max, g_idx = first_argmax(lg)
    p_group = 1.0 / jnp.sum(jnp.exp(lg - g_max), axis=-1, keepdims=True)
    lo = N_GROUPS + g_idx * EXPERTS_PER_GROUP
    le = jnp.where((lane >= lo) & (lane < lo + EXPERTS_PER_GROUP), logits, neg)
    m1, i1 = first_argmax(le)
    m2, i2 = first_argmax(jnp.where(lane == i1, neg, le))
    e2 = jnp.exp(m2 - m1)
    w1 = 1.0 / (1.0 + e2)
    w2 = e2 / (1.0 + e2)
    route = jnp.where(lane == i1, w1, jnp.where(lane == i2, w2, 0.0)) * p_group
    out_ref[:, D_MODEL:] = jnp.where(lane == ROUTE_LANES - 1, g_idx.astype(F32), route)


def _out_router(xp, gp, rp, xs, gs, rs, wo, ffn_norm, rw, rb, *, tm):
    mp, ms = xp.shape[0], xs.shape[0]
    tiles_p, tiles_s = mp // tm, ms // tm
    rows_p = lambda w: pl.BlockSpec((tm, w), lambda i: (jnp.minimum(i, tiles_p - 1), 0))
    rows_s = lambda w: pl.BlockSpec((tm, w), lambda i: (jnp.maximum(i - tiles_p, 0), 0))
    full = lambda shape: pl.BlockSpec(shape, lambda i: (0,) * len(shape))
    return pl.pallas_call(
        functools.partial(_out_router_kernel, tm=tm, tiles_p=tiles_p),
        grid=(tiles_p + tiles_s,),
        in_specs=[rows_p(D_MODEL), rows_p(D_GLA), rows_p(D_RWKV),
                  rows_s(D_MODEL), rows_s(D_GLA), rows_s(D_RWKV),
                  full(wo.shape), full((1, D_MODEL)), full(rw.shape), full((1, ROUTE_LANES))],
        out_specs=pl.BlockSpec((tm, X1_WIDTH), lambda i: (i, 0)),
        out_shape=jax.ShapeDtypeStruct((mp + ms, X1_WIDTH), F32),
        compiler_params=_cparams(("arbitrary",)),
        name="out_router",
    )(xp, gp, rp, xs, gs, rs, wo, ffn_norm, rw, rb)


def _moe_kernel(tile_group_ref, tile_rows_ref, tile_pos_ref, order_ref,
                x1_hbm, wg_ref, wu_ref, wd_ref, fg_ref, fn_ref, yp_hbm, ys_hbm,
                acc, hbuf, wgb, wub, wdb, gsem, ssem, *, sub, rows_p, n_tokens):
    j = pl.program_id(0)
    e = pl.program_id(1)
    nrows = tile_rows_ref[j]
    pos0 = tile_pos_ref[j]
    n_sub = (nrows + sub - 1) // sub

    def token(rr):
        return order_ref[jnp.minimum(pos0 + rr, n_tokens - 1)]

    def gather_copy(rr):
        return pltpu.make_async_copy(x1_hbm.at[pl.ds(token(rr), 1)], acc.at[pl.ds(rr, 1)], gsem)

    def scatter_start(rr):
        tok = token(rr)
        row = acc.at[pl.ds(rr, 1), pl.ds(0, D_MODEL)]

        @pl.when(tok < rows_p)
        def _():
            pltpu.make_async_copy(row, yp_hbm.at[pl.ds(tok, 1)], ssem).start()

        @pl.when(tok >= rows_p)
        def _():
            pltpu.make_async_copy(row, ys_hbm.at[pl.ds(tok - rows_p, 1)], ssem).start()

    def scatter_wait(rr):
        pltpu.make_async_copy(acc.at[pl.ds(rr, 1), pl.ds(0, D_MODEL)], yp_hbm.at[pl.ds(0, 1)], ssem).wait()

    def for_rows(n_blocks, fn):
        def blk(bi, cy):
            for q in range(DMA_UNROLL):
                fn(bi * DMA_UNROLL + q)
            return cy
        lax.fori_loop(0, n_blocks, blk, 0)

    @pl.when((e == 0) & (nrows > 0))
    def _():
        n_blk = n_sub * (sub // DMA_UNROLL)
        for_rows(n_blk, lambda rr: gather_copy(rr).start())
        for_rows(n_blk, lambda rr: gather_copy(rr).wait())

        def norm(sb, cy):
            rs = pl.ds(pl.multiple_of(sb * sub, sub), sub)
            hbuf[rs, :] = _rms(acc[rs, :D_MODEL], fg_ref[...]).astype(BF16)
            return cy
        lax.fori_loop(0, n_sub, norm, 0)

    @pl.when(nrows > 0)
    def _():
        g = tile_group_ref[j]
        wgb[...] = wg_ref[0, 0].astype(BF16)
        wub[...] = wu_ref[0, 0].astype(BF16)
        wdb[...] = wd_ref[0, 0].astype(BF16)
        lane = lax.broadcasted_iota(jnp.int32, (sub, ROUTE_LANES), 1)
        sel = lane == N_GROUPS + g * EXPERTS_PER_GROUP + e

        def expert(sb, cy):
            rs = pl.ds(pl.multiple_of(sb * sub, sub), sub)
            wcol = jnp.sum(jnp.where(sel, acc[rs, D_MODEL:], 0.0), axis=-1, keepdims=True)
            h2 = hbuf[rs, :]
            gate = jnp.dot(h2, wgb[...], preferred_element_type=F32)
            up = jnp.dot(h2, wub[...], preferred_element_type=F32)
            hid = (gate * jax.nn.sigmoid(gate)) * up * wcol
            acc[rs, :D_MODEL] += jnp.dot(hid.astype(BF16), wdb[...], preferred_element_type=F32)
            return cy
        lax.fori_loop(0, n_sub, expert, 0)

    @pl.when((e == EXPERTS_PER_GROUP - 1) & (nrows > 0))
    def _():
        def final(sb, cy):
            rs = pl.ds(pl.multiple_of(sb * sub, sub), sub)
            acc[rs, :D_MODEL] = _rms(acc[rs, :D_MODEL], fn_ref[...])
            return cy
        lax.fori_loop(0, n_sub, final, 0)

        n_full = nrows // DMA_UNROLL
        for_rows(n_full, scatter_start)

        def rest_start(rr, cy):
            scatter_start(rr)
            return cy
        lax.fori_loop(n_full * DMA_UNROLL, nrows, rest_start, 0)
        for_rows(n_full, scatter_wait)

        def rest_wait(rr, cy):
            scatter_wait(rr)
            return cy
        lax.fori_loop(n_full * DMA_UNROLL, nrows, rest_wait, 0)


def _moe(x1e, tile_group, tile_rows, tile_pos, order, wg, wu, wd, ffn_norm, final_norm, *, tile, sub, rows_p):
    m = x1e.shape[0]
    n_tiles = tile_group.shape[0]

    def w_idx(j, e, tg, tr, tp, od):
        return (tg[j], jnp.where(tr[j] > 0, e, EXPERTS_PER_GROUP - 1), 0, 0)

    vec = pl.BlockSpec((1, D_MODEL), lambda j, e, tg, tr, tp, od: (0, 0))
    grid_spec = pltpu.PrefetchScalarGridSpec(
        num_scalar_prefetch=4,
        grid=(n_tiles, EXPERTS_PER_GROUP),
        in_specs=[pl.BlockSpec(memory_space=pl.ANY),
                  pl.BlockSpec((1, 1, D_MODEL, D_EXPERT), w_idx),
                  pl.BlockSpec((1, 1, D_MODEL, D_EXPERT), w_idx),
                  pl.BlockSpec((1, 1, D_EXPERT, D_MODEL), w_idx),
                  vec, vec],
        out_specs=[pl.BlockSpec(memory_space=pl.ANY), pl.BlockSpec(memory_space=pl.ANY)],
        scratch_shapes=[pltpu.VMEM((tile, X1_WIDTH), F32),
                        pltpu.VMEM((tile, D_MODEL), BF16),
                        pltpu.VMEM((D_MODEL, D_EXPERT), BF16),
                        pltpu.VMEM((D_MODEL, D_EXPERT), BF16),
                        pltpu.VMEM((D_EXPERT, D_MODEL), BF16),
                        pltpu.SemaphoreType.DMA(()),
                        pltpu.SemaphoreType.DMA(())],
    )
    return pl.pallas_call(
        functools.partial(_moe_kernel, sub=sub, rows_p=rows_p, n_tokens=m),
        grid_spec=grid_spec,
        out_shape=[jax.ShapeDtypeStruct((rows_p, D_MODEL), F32),
                   jax.ShapeDtypeStruct((m - rows_p, D_MODEL), F32)],
        compiler_params=_cparams(("arbitrary", "arbitrary")),
        name="moe",
    )(tile_group, tile_rows, tile_pos, order, x1e, wg, wu, wd, ffn_norm, final_norm)


def _moe_plan(group_id, tile):
    m = group_id.shape[0]
    n_tiles = m // tile + N_GROUPS
    order = jnp.argsort(group_id, stable=True).astype(jnp.int32)
    counts = jnp.sum(group_id[:, None] == jnp.arange(N_GROUPS, dtype=jnp.int32)[None, :], axis=0).astype(jnp.int32)
    tiles_per = (counts + tile - 1) // tile
    tile_end = jnp.cumsum(tiles_per)
    tile_start = tile_end - tiles_per
    row_start = jnp.cumsum(counts) - counts
    t = jnp.arange(n_tiles, dtype=jnp.int32)
    used = t < tile_end[-1]
    grp = jnp.minimum(jnp.sum(t[:, None] >= tile_end[None, :], axis=1), N_GROUPS - 1).astype(jnp.int32)
    last_group = jnp.max(jnp.where(counts > 0, jnp.arange(N_GROUPS, dtype=jnp.int32), 0))
    tile_group = jnp.where(used, grp, last_group).astype(jnp.int32)
    local = (t - tile_start[grp]) * tile
    tile_rows = jnp.where(used, jnp.clip(counts[grp] - local, 0, tile), 0).astype(jnp.int32)
    tile_pos = jnp.where(used, row_start[grp] + local, 0).astype(jnp.int32)
    return tile_group, tile_rows, tile_pos, order


def _prep_params(mix_norm, w_in, gla_gate_w1, gla_gate_w2, gla_gate_b, gla_norm, rwkv_mu_rkv, rwkv_mu_wag,
                 rwkv_w0, rwkv_w1, rwkv_w2, rwkv_a0, rwkv_a1, rwkv_a2, rwkv_g1, rwkv_g2,
                 rwkv_k_k, rwkv_k_a, rwkv_r_k, rwkv_ln_w, rwkv_ln_b, w_out, ffn_norm,
                 router_group_w, router_group_b, router_expert_w, router_expert_b, final_norm):
    row = lambda v: v.reshape(1, -1).astype(F32)
    g_rank = 2 * LANE
    p = dict(
        mix_norm=row(mix_norm[0]),
        mu_wag=rwkv_mu_wag[0],
        w_in=w_in[0].astype(BF16),
        gw1=_pad_axis(gla_gate_w1[0], 1, LANE).astype(BF16),
        gw2=_pad_axis(gla_gate_w2[0], 0, LANE).astype(BF16),
        gb=row(gla_gate_b[0]),
        w1=_pad_axis(rwkv_w1[0], 1, LANE).astype(BF16),
        w2=_pad_axis(rwkv_w2[0], 0, LANE).astype(BF16),
        w0=row(rwkv_w0[0]),
        a1=_pad_axis(rwkv_a1[0], 1, LANE).astype(BF16),
        a2=_pad_axis(rwkv_a2[0], 0, LANE).astype(BF16),
        a0=row(rwkv_a0[0]),
        g1=_pad_axis(rwkv_g1[0], 1, g_rank).astype(BF16),
        g2=_pad_axis(rwkv_g2[0], 0, g_rank).astype(BF16),
        gla_norm=row(gla_norm[0]),
        mu_ext=jnp.concatenate([jnp.zeros((1, RWKV_OFF), F32), rwkv_mu_rkv[0].reshape(1, -1)], axis=1),
        k_k=row(rwkv_k_k[0]), k_a=row(rwkv_k_a[0]), r_k=row(rwkv_r_k[0]),
        ln_w=row(rwkv_ln_w[0]), ln_b=row(rwkv_ln_b[0]),
        wo=w_out[0].astype(BF16).reshape(2, D_GLA, D_MODEL),
        ffn_norm=row(ffn_norm[0]),
        router_w=_pad_axis(jnp.concatenate([router_group_w[0], router_expert_w[0]], axis=1), 1,
                           ROUTE_LANES).astype(BF16),
        router_b=_pad_axis(jnp.concatenate([router_group_b[0], router_expert_b[0]]).reshape(1, -1), 1,
                           ROUTE_LANES).astype(F32),
        final_norm=row(final_norm),
    )
    r = jnp.arange(MXU_TILE)
    p['gla_e'] = (r[:, None] // GLA_DK == jnp.arange(2 * GLA_DV)[None, :] // GLA_DV).astype(BF16)
    p['wkv_bd'] = (r[:, None] // RWKV_HEAD == r[None, :] // RWKV_HEAD).astype(BF16)
    return p


def _mixer(x, s_gla, s_wkv, s_shift, p, *, tm, tc):
    nb, seq, _ = x.shape
    m = nb * seq
    x2 = x.reshape(m, D_MODEL)
    multi_seq = tm > seq

    h_last = jnp.zeros((nb, D_MODEL), F32) if s_shift is None else s_shift
    zl = _plain_proj(_pad_axis(h_last, 0, -(-nb // 16) * 16).astype(BF16), p['w_in'], tn=1024)[:nb]
    if multi_seq:
        expand = lambda a: jnp.repeat(a, seq, axis=0)
        hprev, zprev = expand(h_last), expand(zl)
    else:
        hprev, zprev = h_last.reshape(nb, 1, D_MODEL), zl.reshape(nb, 1, D_IN)

    h_bf, la, lw, a_sig, gate, hlast = _norm_proj(x2, hprev, p, seq=seq, tm=tm)
    shift_new = hlast.reshape(nb, seq, D_MODEL)[:, -1] if multi_seq else hlast.reshape(nb, D_MODEL)
    z = _in_proj(h_bf, p['w_in'], p['mu_ext'], zprev, seq=seq, tm=tm if multi_seq else min(seq, 1024), tn=1024)

    z3 = z.reshape(nb, seq, D_IN)
    r3 = lambda a: a.reshape(nb, seq, a.shape[-1])
    if s_gla is None:
        s_gla = jnp.zeros((nb, GLA_HEADS, GLA_DK, GLA_DV), F32)
        s_wkv = jnp.zeros((nb, RWKV_HEADS, RWKV_HEAD, RWKV_HEAD), F32)
    og, gla_new = _gla(z3, r3(la), s_gla, p['gla_e'], p['gla_norm'], tc=tc, chunk=min(GLA_CHUNK, seq),
                       nseq=2 if nb % 2 == 0 else 1)
    wkv_seqs = 4 if nb % 4 == 0 else 1
    orr, wkv_new = _wkv(z3, r3(lw), r3(a_sig), r3(gate), s_wkv, p['wkv_bd'], p, tc=min(tc, 512 // wkv_seqs),
                        chunk=min(WKV_CHUNK, seq), nseq=wkv_seqs)
    return (x2, og.reshape(m, D_GLA), orr.reshape(m, D_RWKV)), (gla_new, wkv_new, shift_new)


def _ffn(rows_p, rows_s, p, wg, wu, wd, *, tm, moe_tile, moe_sub):
    mp = rows_p[0].shape[0]
    x1e = _out_router(*rows_p, *rows_s, p['wo'], p['ffn_norm'], p['router_w'], p['router_b'], tm=tm)
    group_id = x1e[:, X1_WIDTH - 1].astype(jnp.int32)
    tile_group, tile_rows, tile_pos, order = _moe_plan(group_id, moe_tile)
    return _moe(x1e, tile_group, tile_rows, tile_pos, order, wg, wu, wd, p['ffn_norm'], p['final_norm'],
                tile=moe_tile, sub=moe_sub, rows_p=mp)


def kernel(x_prompt, x_sample, state_gla, state_wkv, state_shift, mix_norm, w_in, gla_gate_w1, gla_gate_w2, gla_gate_b, gla_norm, rwkv_mu_rkv, rwkv_mu_wag, rwkv_w0, rwkv_w1, rwkv_w2, rwkv_a0, rwkv_a1, rwkv_a2, rwkv_g1, rwkv_g2, rwkv_k_k, rwkv_k_a, rwkv_r_k, rwkv_ln_w, rwkv_ln_b, w_out, ffn_norm, router_group_w, router_group_b, router_expert_w, router_expert_b, expert_w_gate, expert_w_up, expert_w_down, final_norm):
    p = _prep_params(mix_norm, w_in, gla_gate_w1, gla_gate_w2, gla_gate_b, gla_norm, rwkv_mu_rkv, rwkv_mu_wag,
                     rwkv_w0, rwkv_w1, rwkv_w2, rwkv_a0, rwkv_a1, rwkv_a2, rwkv_g1, rwkv_g2,
                     rwkv_k_k, rwkv_k_a, rwkv_r_k, rwkv_ln_w, rwkv_ln_b, w_out, ffn_norm,
                     router_group_w, router_group_b, router_expert_w, router_expert_b, final_norm)
    wg, wu, wd = expert_w_gate[0], expert_w_up[0], expert_w_down[0]
    nb_p, seq_p, _ = x_prompt.shape
    nb_s, seq_s, _ = x_sample.shape
    rows_p, (gla_p, wkv_p, shift_p) = _mixer(x_prompt, None, None, None, p, tm=256, tc=min(256, seq_p))
    rows_s, (gla_s, wkv_s, shift_s) = _mixer(x_sample, state_gla[0], state_wkv[0], state_shift[0], p,
                                             tm=nb_s * seq_s, tc=seq_s)
    y_p, y_s = _ffn(rows_p, rows_s, p, wg, wu, wd, tm=256, moe_tile=5 * MOE_SUB, moe_sub=MOE_SUB)
    return (y_p.reshape(nb_p, seq_p, D_MODEL), y_s.reshape(nb_s, seq_s, D_MODEL),
            gla_p[None], wkv_p[None], shift_p[None], gla_s[None], wkv_s[None], shift_s[None])
```

```python
import functools

import jax
import jax.numpy as jnp
from jax import lax
from jax.experimental import pallas as pl
from jax.experimental.pallas import tpu as pltpu

F32 = jnp.float32
BF16 = jnp.bfloat16

D_MODEL = 2048
D_GLA = 1024
D_RWKV = 1024
GLA_HEADS = 4
GLA_DV = 256
GLA_DK = 128
GLA_K = 512
GLA_GATE_TEMP = 16.0
RWKV_HEAD = 64
RWKV_HEADS = 16
N_GROUPS = 8
EXPERTS_PER_GROUP = 8
D_EXPERT = 512
NORM_EPS = 1e-6
GN_EPS = 64e-5
RWKV_OFF = 2 * GLA_K + 2 * D_GLA
D_IN = RWKV_OFF + 3 * D_RWKV

LANE = 128
SUBLANE = 8
MXU_TILE = 256
VMEM_LIMIT = 56 * 1024 * 1024

GLA_CHUNK = 16
WKV_CHUNK = 8
N_EXPERTS = N_GROUPS * EXPERTS_PER_GROUP
MOE_TILE = 256
MOE_SUB = 128
ROUTE_LANES = 128
DMA_UNROLL = 8


def _cparams(sem):
    return pltpu.CompilerParams(dimension_semantics=sem, vmem_limit_bytes=VMEM_LIMIT)


def _rms(x, gain):
    return x * lax.rsqrt(jnp.mean(x * x, axis=-1, keepdims=True) + NORM_EPS) * gain


def _softplus(y):
    return jnp.maximum(y, 0.0) + jnp.log1p(jnp.exp(-jnp.abs(y)))


def _dot(a, b):
    return jnp.dot(a.astype(BF16), b.astype(BF16), preferred_element_type=F32)


def _dot_nt(a, b):
    return lax.dot_general(a.astype(BF16), b.astype(BF16), (((1,), (1,)), ((), ())),
                           preferred_element_type=F32)


def _dot_tn(a, b):
    return lax.dot_general(a.astype(BF16), b.astype(BF16), (((0,), (0,)), ((), ())),
                           preferred_element_type=F32)


def _norm_proj_kernel(x_ref, hprev_ref, gain_ref, mu_ref, gw1_ref, gw2_ref, gb_ref,
                      w1_ref, w2_ref, w0_ref, a1_ref, a2_ref, a0_ref, g1_ref, g2_ref,
                      h_ref, la_ref, lw_ref, a_ref, gate_ref, hlast_ref, carry_ref,
                      *, tm, seq, multi_seq):
    i = pl.program_id(0)
    h = _rms(x_ref[...], gain_ref[...])
    row = lax.broadcasted_iota(jnp.int32, (tm, 1), 0)
    rolled = pltpu.roll(h, 1, 0)
    if multi_seq:
        h_shift = jnp.where(row % seq == 0, hprev_ref[...], rolled)
        hlast_ref[...] = h
    else:
        tiles_per_seq = seq // tm
        first = (i % tiles_per_seq) == 0
        prev = jnp.where(first, hprev_ref[0], carry_ref[...])
        h_shift = jnp.where(row == 0, prev, rolled)
        carry_ref[...] = h[tm - 1:tm]
        hlast_ref[0] = h[tm - 1:tm]
    h_ref[...] = h.astype(BF16)

    dx = h_shift - h
    mu = mu_ref[...]
    xw = h + dx * mu[0:1]
    xa = h + dx * mu[1:2]
    xg = h + dx * mu[2:3]

    gl = _dot(_dot(h, gw1_ref[...]), gw2_ref[...]) + gb_ref[...]
    la_ref[...] = -_softplus(-gl) * (1.0 / GLA_GATE_TEMP)

    u = w0_ref[...] + _dot(jnp.tanh(_dot(xw, w1_ref[...])), w2_ref[...])
    w_log = -_softplus(-u) - 0.5
    lw_ref[...] = -jnp.exp(w_log)

    a_ref[...] = jax.nn.sigmoid(a0_ref[...] + _dot(_dot(xa, a1_ref[...]), a2_ref[...]))
    gate_ref[...] = _dot(jax.nn.sigmoid(_dot(xg, g1_ref[...])), g2_ref[...])


def _pad_axis(w, axis, to):
    pad = [(0, 0)] * w.ndim
    pad[axis] = (0, to - w.shape[axis])
    return jnp.pad(w, pad)


def _norm_proj(x2, hprev, p, *, seq, tm):
    m = x2.shape[0]
    multi_seq = tm > seq
    nb = m // seq
    full = lambda shape: pl.BlockSpec(shape, lambda i: (0,) * len(shape))
    rows = lambda w: pl.BlockSpec((tm, w), lambda i: (i, 0))
    if multi_seq:
        assert tm == m
        hprev_spec = rows(D_MODEL)
        hlast_shape = jax.ShapeDtypeStruct((m, D_MODEL), F32)
        hlast_spec = rows(D_MODEL)
    else:
        assert seq % tm == 0
        tps = seq // tm
        hprev_spec = pl.BlockSpec((1, 1, D_MODEL), lambda i: (i // tps, 0, 0))
        hlast_shape = jax.ShapeDtypeStruct((nb, 1, D_MODEL), F32)
        hlast_spec = pl.BlockSpec((1, 1, D_MODEL), lambda i: (i // tps, 0, 0))
    weights = [p['gw1'], p['gw2'], p['gb'], p['w1'], p['w2'], p['w0'], p['a1'], p['a2'], p['a0'],
               p['g1'], p['g2']]
    return pl.pallas_call(
        functools.partial(_norm_proj_kernel, tm=tm, seq=seq, multi_seq=multi_seq),
        grid=(m // tm,),
        in_specs=[rows(D_MODEL), hprev_spec, full((1, D_MODEL)), full((3, D_MODEL))]
                 + [full(w.shape) for w in weights],
        out_specs=[rows(D_MODEL), rows(GLA_K), rows(D_RWKV), rows(D_RWKV), rows(D_RWKV), hlast_spec],
        out_shape=[jax.ShapeDtypeStruct((m, D_MODEL), BF16),
                   jax.ShapeDtypeStruct((m, GLA_K), F32),
                   jax.ShapeDtypeStruct((m, D_RWKV), F32),
                   jax.ShapeDtypeStruct((m, D_RWKV), F32),
                   jax.ShapeDtypeStruct((m, D_RWKV), F32),
                   hlast_shape],
        scratch_shapes=[pltpu.VMEM((1, D_MODEL), F32)],
        compiler_params=_cparams(("arbitrary",)),
        name="norm_proj",
    )(x2, hprev, p['mix_norm'], p['mu_wag'], *weights)


def _in_proj_kernel(h_ref, w_ref, mu_ref, zprev_ref, z_ref, carry_ref, *, tm, seq, multi_seq):
    i = pl.program_id(1)
    z = jnp.dot(h_ref[...], w_ref[...], preferred_element_type=F32)
    row = lax.broadcasted_iota(jnp.int32, (tm, 1), 0)
    rolled = pltpu.roll(z, 1, 0)
    if multi_seq:
        z_prev = jnp.where(row % seq == 0, zprev_ref[...], rolled)
    else:
        tiles_per_seq = seq // tm
        first = (i % tiles_per_seq) == 0
        prev = jnp.where(first, zprev_ref[0], carry_ref[...])
        z_prev = jnp.where(row == 0, prev, rolled)
        carry_ref[...] = z[tm - 1:tm]
    z_ref[...] = z + mu_ref[...] * (z_prev - z)


def _in_proj(h_bf, w_in, mu_ext, zprev, *, seq, tm, tn):
    m = h_bf.shape[0]
    multi_seq = tm > seq
    if multi_seq:
        zprev_spec = pl.BlockSpec((tm, tn), lambda j, i: (i, j))
    else:
        tps = seq // tm
        zprev_spec = pl.BlockSpec((1, 1, tn), lambda j, i: (i // tps, 0, j))
    return pl.pallas_call(
        functools.partial(_in_proj_kernel, tm=tm, seq=seq, multi_seq=multi_seq),
        grid=(D_IN // tn, m // tm),
        in_specs=[pl.BlockSpec((tm, D_MODEL), lambda j, i: (i, 0)),
                  pl.BlockSpec((D_MODEL, tn), lambda j, i: (0, j)),
                  pl.BlockSpec((1, tn), lambda j, i: (0, j)),
                  zprev_spec],
        out_specs=pl.BlockSpec((tm, tn), lambda j, i: (i, j)),
        out_shape=jax.ShapeDtypeStruct((m, D_IN), F32),
        scratch_shapes=[pltpu.VMEM((1, tn), F32)],
        compiler_params=_cparams(("arbitrary", "arbitrary")),
        name="in_proj",
    )(h_bf, w_in, mu_ext, zprev)


def _plain_proj_kernel(h_ref, w_ref, z_ref):
    z_ref[...] = jnp.dot(h_ref[...], w_ref[...], preferred_element_type=F32)


def _plain_proj(h_bf, w_in, *, tn):
    m = h_bf.shape[0]
    return pl.pallas_call(
        _plain_proj_kernel,
        grid=(D_IN // tn,),
        in_specs=[pl.BlockSpec((m, D_MODEL), lambda j: (0, 0)),
                  pl.BlockSpec((D_MODEL, tn), lambda j: (0, j))],
        out_specs=pl.BlockSpec((m, tn), lambda j: (0, j)),
        out_shape=jax.ShapeDtypeStruct((m, D_IN), F32),
        compiler_params=_cparams(("arbitrary",)),
        name="prev_row_proj",
    )(h_bf, w_in)


def _gla_kernel(q_ref, k_ref, v_ref, g_ref, la_ref, s0_ref, e_ref, gn_ref,
                o_ref, s_ref, st_ref, obuf_ref, *, tc, chunk, nseq):
    tb = pl.program_id(1)
    c = chunk

    @pl.when(tb == 0)
    def _():
        for q in range(nseq):
            for hh in range(GLA_HEADS):
                st_ref[q, :, hh * GLA_DK:(hh + 1) * GLA_DK] = s0_ref[q, hh].T

    row = lax.broadcasted_iota(jnp.int32, (c, 1), 0)
    e_mat = e_ref[...]
    neg = jnp.float32(-1e30)

    def one_chunk(q, r0):
        qq = q_ref[q, pl.ds(r0, c), :] * (GLA_DK ** -0.5)
        k = k_ref[q, pl.ds(r0, c), :]
        v = v_ref[q, pl.ds(r0, c), :]
        la = la_ref[q, pl.ds(r0, c), :]
        b = jnp.zeros_like(la)
        for s in range(c):
            b = b + jnp.where(row >= s, la[s:s + 1], 0.0)
        b_last = b[c - 1:c]

        parts = []
        for s in range(c):
            dec = jnp.exp(jnp.where(row >= s, b - b[s:s + 1], neg))
            parts.append((qq * k[s:s + 1] * dec).astype(BF16))
        pcat = jnp.concatenate(parts, axis=0)
        sc = jnp.concatenate(
            [jnp.dot(pcat[:, hp * MXU_TILE:(hp + 1) * MXU_TILE], e_mat, preferred_element_type=F32)
             for hp in range(GLA_K // MXU_TILE)], axis=1)
        o = jnp.zeros((c, D_GLA), F32)
        for s in range(c):
            o = o + sc[s * c:(s + 1) * c] * v[s:s + 1]

        qd = qq * jnp.exp(b)
        kd = k * jnp.exp(b_last - b)
        a_last = jnp.exp(b_last)
        o_heads = []
        for hh in range(GLA_HEADS):
            dk = slice(hh * GLA_DK, (hh + 1) * GLA_DK)
            dv = slice(hh * GLA_DV, (hh + 1) * GLA_DV)
            st = st_ref[q, :, dk]
            oh = o[:, dv] + _dot_nt(qd[:, dk], st)
            st_ref[q, :, dk] = a_last[:, dk] * st + _dot_tn(v[:, dv], kd[:, dk])
            oh = oh * lax.rsqrt(jnp.mean(oh * oh, axis=-1, keepdims=True) + NORM_EPS)
            o_heads.append(oh)
        og = jnp.concatenate(o_heads, axis=1)
        g = g_ref[q, pl.ds(r0, c), :]
        obuf_ref[q, pl.ds(r0, c), :] = og * gn_ref[...] * (g * jax.nn.sigmoid(g))

    def body(ci, carry):
        r0 = pl.multiple_of(ci * c, c)
        for q in range(nseq):
            one_chunk(q, r0)
        return carry

    lax.fori_loop(0, tc // c, body, 0)
    o_ref[...] = obuf_ref[...].astype(BF16)

    @pl.when(tb == pl.num_programs(1) - 1)
    def _():
        for q in range(nseq):
            for hh in range(GLA_HEADS):
                s_ref[q, hh] = st_ref[q, :, hh * GLA_DK:(hh + 1) * GLA_DK].T


def _gla(z3, la3, s0, e_mat, gla_norm, *, tc, chunk, nseq):
    nb, seq, _ = z3.shape
    blk = lambda w, col: pl.BlockSpec((nseq, tc, w), lambda b, t: (b, t, col))
    st_spec = pl.BlockSpec((nseq, GLA_HEADS, GLA_DK, GLA_DV), lambda b, t: (b, 0, 0, 0))
    return pl.pallas_call(
        functools.partial(_gla_kernel, tc=tc, chunk=chunk, nseq=nseq),
        grid=(nb // nseq, seq // tc),
        in_specs=[blk(GLA_K, 0), blk(GLA_K, 1), blk(D_GLA, 1), blk(D_GLA, 2), blk(GLA_K, 0),
                  st_spec,
                  pl.BlockSpec(e_mat.shape, lambda b, t: (0, 0)),
                  pl.BlockSpec((1, D_GLA), lambda b, t: (0, 0))],
        out_specs=[blk(D_GLA, 0), st_spec],
        out_shape=[jax.ShapeDtypeStruct((nb, seq, D_GLA), BF16),
                   jax.ShapeDtypeStruct((nb, GLA_HEADS, GLA_DK, GLA_DV), F32)],
        scratch_shapes=[pltpu.VMEM((nseq, GLA_DV, GLA_K), F32), pltpu.VMEM((nseq, tc, D_GLA), F32)],
        compiler_params=_cparams(("arbitrary", "arbitrary")),
        name="gla",
    )(z3, z3, z3, z3, la3, s0, e_mat, gla_norm)


N_LANE_TILES = D_RWKV // MXU_TILE
HEADS_PER_TILE = MXU_TILE // RWKV_HEAD


def _stack_tiles(x):
    return jnp.concatenate([x[:, t * MXU_TILE:(t + 1) * MXU_TILE] for t in range(N_LANE_TILES)], axis=0)


def _unstack_tiles(y, r):
    return jnp.concatenate([y[t * r:(t + 1) * r] for t in range(N_LANE_TILES)], axis=1)


def _seg_sums(xs, bd):
    c = xs[0].shape[0]
    st = jnp.concatenate([_stack_tiles(x) for x in xs], axis=0)
    hi = st.astype(BF16)
    lo = (st - hi.astype(F32)).astype(BF16)
    res = jnp.dot(jnp.concatenate([hi, lo], axis=0), bd, preferred_element_type=F32)
    half = st.shape[0]
    res = res[:half] + res[half:]
    rows = N_LANE_TILES * c
    return [_unstack_tiles(res[n * rows:(n + 1) * rows], c) for n in range(len(xs))]


def _cumsum_rows(x, row, c):
    if c == SUBLANE:
        step = 1
        while step < c:
            x = x + jnp.where(row >= step, pltpu.roll(x, step, 0), 0.0)
            step *= 2
        return x
    out = jnp.zeros_like(x)
    for s in range(c):
        out = out + jnp.where(row >= s, x[s:s + 1], 0.0)
    return out


def _wkv_kernel(r_ref, k_ref, v_ref, lw_ref, a_ref, gate_ref, s0_ref, bd_ref,
                kk_ref, ka_ref, rk_ref, lnw_ref, lnb_ref,
                o_ref, s_ref, st_ref, obuf_ref, *, tc, chunk, nseq):
    tb = pl.program_id(1)
    c = chunk
    cc = c * c

    @pl.when(tb == 0)
    def _():
        for q in range(nseq):
            st_ref[q] = jnp.concatenate([s0_ref[q, hh] for hh in range(RWKV_HEADS)], axis=1)

    row = lax.broadcasted_iota(jnp.int32, (c, 1), 0)
    bd = bd_ref[...]
    lane_head = lax.broadcasted_iota(jnp.int32, (RWKV_HEAD, MXU_TILE), 1) // RWKV_HEAD
    neg = jnp.float32(-1e30)

    def one_chunk(q, r0):
        r = r_ref[q, pl.ds(r0, c), :]
        k = k_ref[q, pl.ds(r0, c), :]
        v = v_ref[q, pl.ds(r0, c), :]
        lw = lw_ref[q, pl.ds(r0, c), :]
        a_sig = a_ref[q, pl.ds(r0, c), :]

        kk = k * kk_ref[...]
        k_eff = k * (1.0 + (a_sig - 1.0) * ka_ref[...])
        kk_ss, bonus_dot = _seg_sums([kk * kk, r * k_eff * rk_ref[...]], bd)
        kk = kk / jnp.maximum(jnp.sqrt(kk_ss), 1e-12)
        a_vec = -kk
        b_vec = kk * a_sig

        lb = _cumsum_rows(lw, row, c)
        lbp = lb - lw
        lb_last = lb[c - 1:c]
        kinds = ([], [], [], [])
        for s in range(c):
            lbs = lb[s:s + 1]
            bs = b_vec[s:s + 1]
            ks = k_eff[s:s + 1]
            a_d = a_vec * jnp.exp(jnp.where(row > s, lbp - lbs, neg))
            r_d = r * jnp.exp(jnp.where(row >= s, lb - lbs, neg))
            kinds[0].append(a_d * bs)
            kinds[1].append(a_d * ks)
            kinds[2].append(r_d * bs)
            kinds[3].append(r_d * ks)
        prod = jnp.concatenate(
            [p[:, t * MXU_TILE:(t + 1) * MXU_TILE]
             for t in range(N_LANE_TILES) for kind in kinds for p in kind], axis=0)
        coef_t = jnp.dot(prod.astype(BF16), bd, preferred_element_type=F32)

        def coef(kind, s):
            return jnp.concatenate(
                [coef_t[(t * 4 + kind) * cc + s * c:(t * 4 + kind) * cc + (s + 1) * c]
                 for t in range(N_LANE_TILES)], axis=1)

        lhs = jnp.concatenate([a_vec * jnp.exp(lbp), r * jnp.exp(lb)], axis=0).astype(BF16)
        x0 = []
        for t in range(N_LANE_TILES):
            ln = slice(t * MXU_TILE, (t + 1) * MXU_TILE)
            st_t = st_ref[q, :, ln].astype(BF16)
            w_t = jnp.concatenate([st_t] * HEADS_PER_TILE, axis=0) * bd
            x0.append(lax.dot_general(lhs[:, ln], w_t, (((1,), (1,)), ((), ())),
                                      preferred_element_type=F32))
        x0 = jnp.concatenate(x0, axis=1)
        u = x0[:c]
        o = x0[c:]

        for s in range(c):
            u = u + coef(1, s) * v[s:s + 1]
        for s in range(c):
            u = u + coef(0, s) * u[s:s + 1]
        for s in range(c):
            o = o + coef(2, s) * u[s:s + 1] + coef(3, s) * v[s:s + 1]

        tail = jnp.exp(lb_last - lb)
        uv = jnp.concatenate([u, v], axis=0).astype(BF16)
        bk = jnp.concatenate([b_vec * tail, k_eff * tail], axis=0).astype(BF16)
        w_last = jnp.exp(lb_last)
        for t in range(N_LANE_TILES):
            ln = slice(t * MXU_TILE, (t + 1) * MXU_TILE)
            zt = _dot_tn(uv[:, ln], bk[:, ln])
            upd = zt[(HEADS_PER_TILE - 1) * RWKV_HEAD:]
            for hh in range(HEADS_PER_TILE - 2, -1, -1):
                upd = jnp.where(lane_head == hh, zt[hh * RWKV_HEAD:(hh + 1) * RWKV_HEAD], upd)
            st_ref[q, :, ln] = w_last[:, ln] * st_ref[q, :, ln] + upd

        inv_n = 1.0 / RWKV_HEAD
        mu = _seg_sums([o], bd)[0] * inv_n
        dev = o - mu
        var = _seg_sums([dev * dev], bd)[0] * inv_n
        yn = dev * lax.rsqrt(var + GN_EPS) * lnw_ref[...] + lnb_ref[...]
        obuf_ref[q, pl.ds(r0, c), :] = (yn + bonus_dot * v) * gate_ref[q, pl.ds(r0, c), :]

    def body(ci, carry):
        r0 = pl.multiple_of(ci * c, c)
        for q in range(nseq):
            one_chunk(q, r0)
        return carry

    lax.fori_loop(0, tc // c, body, 0)
    o_ref[...] = obuf_ref[...].astype(BF16)

    @pl.when(tb == pl.num_programs(1) - 1)
    def _():
        for q in range(nseq):
            for hh in range(RWKV_HEADS):
                s_ref[q, hh] = st_ref[q, :, hh * RWKV_HEAD:(hh + 1) * RWKV_HEAD]


def _wkv(z3, lw3, a3, gate3, s0, bd, p, *, tc, chunk, nseq):
    nb, seq, _ = z3.shape
    col0 = RWKV_OFF // D_RWKV
    blk = lambda col: pl.BlockSpec((nseq, tc, D_RWKV), lambda b, t: (b, t, col))
    st_spec = pl.BlockSpec((nseq, RWKV_HEADS, RWKV_HEAD, RWKV_HEAD), lambda b, t: (b, 0, 0, 0))
    vec = pl.BlockSpec((1, D_RWKV), lambda b, t: (0, 0))
    return pl.pallas_call(
        functools.partial(_wkv_kernel, tc=tc, chunk=chunk, nseq=nseq),
        grid=(nb // nseq, seq // tc),
        in_specs=[blk(col0), blk(col0 + 1), blk(col0 + 2), blk(0), blk(0), blk(0), st_spec,
                  pl.BlockSpec(bd.shape, lambda b, t: (0, 0)), vec, vec, vec, vec, vec],
        out_specs=[blk(0), st_spec],
        out_shape=[jax.ShapeDtypeStruct((nb, seq, D_RWKV), BF16),
                   jax.ShapeDtypeStruct((nb, RWKV_HEADS, RWKV_HEAD, RWKV_HEAD), F32)],
        scratch_shapes=[pltpu.VMEM((nseq, RWKV_HEAD, D_RWKV), F32),
                        pltpu.VMEM((nseq, tc, D_RWKV), F32)],
        compiler_params=_cparams(("arbitrary", "arbitrary")),
        name="wkv",
    )(z3, z3, z3, lw3, a3, gate3, s0, bd, p['k_k'], p['k_a'], p['r_k'], p['ln_w'], p['ln_b'])


def _out_router_kernel(xp_ref, gp_ref, rp_ref, xs_ref, gs_ref, rs_ref, wo_ref, fg_ref, rw_ref, rb_ref,
                       x1_ref, h2_ref, info_ref, *, tm, tiles_p):
    is_p = pl.program_id(0) < tiles_p
    x = jnp.where(is_p, xp_ref[...], xs_ref[...])
    og = jnp.where(is_p, gp_ref[...], gs_ref[...])
    orr = jnp.where(is_p, rp_ref[...], rs_ref[...])
    mix = (jnp.dot(og, wo_ref[0], preferred_element_type=F32)
           + jnp.dot(orr, wo_ref[1], preferred_element_type=F32))
    x1 = x + mix
    x1_ref[...] = x1
    h2 = _rms(x1, fg_ref[...])
    h2_ref[...] = h2
    logits = _dot(h2, rw_ref[...]) + rb_ref[...]
    lane = lax.broadcasted_iota(jnp.int32, (tm, ROUTE_LANES), 1)
    neg = jnp.float32(-jnp.inf)
    big = jnp.int32(1 << 20)

    def first_argmax(vals):
        mx = jnp.max(vals, axis=-1, keepdims=True)
        idx = jnp.min(jnp.where(vals == mx, lane, big), axis=-1, keepdims=True)
        return mx, idx

    lg = jnp.where(lane < N_GROUPS, logits, neg)
    g_max, g_idx = first_argmax(lg)
    p_group = 1.0 / jnp.sum(jnp.exp(lg - g_max), axis=-1, keepdims=True)
    lo = N_GROUPS + g_idx * EXPERTS_PER_GROUP
    le = jnp.where((lane >= lo) & (lane < lo + EXPERTS_PER_GROUP), logits, neg)
    m1, i1 = first_argmax(le)
    m2, i2 = first_argmax(jnp.where(lane == i1, neg, le))
    e2 = jnp.exp(m2 - m1)
    w1 = p_group / (1.0 + e2)
    w2 = p_group * e2 / (1.0 + e2)
    info_ref[...] = jnp.where(lane == 0, (i1 - N_GROUPS).astype(F32),
                              jnp.where(lane == 1, (i2 - N_GROUPS).astype(F32),
                                        jnp.where(lane == 2, w1, jnp.where(lane == 3, w2, 0.0))))


def _out_router(xp, gp, rp, xs, gs, rs, wo, ffn_norm, rw, rb, *, tm):
    mp, ms = xp.shape[0], xs.shape[0]
    tiles_p, tiles_s = mp // tm, ms // tm
    rows_p = lambda w: pl.BlockSpec((tm, w), lambda i: (jnp.minimum(i, tiles_p - 1), 0))
    rows_s = lambda w: pl.BlockSpec((tm, w), lambda i: (jnp.maximum(i - tiles_p, 0), 0))
    rows = lambda w: pl.BlockSpec((tm, w), lambda i: (i, 0))
    full = lambda shape: pl.BlockSpec(shape, lambda i: (0,) * len(shape))
    m = mp + ms
    return pl.pallas_call(
        functools.partial(_out_router_kernel, tm=tm, tiles_p=tiles_p),
        grid=(tiles_p + tiles_s,),
        in_specs=[rows_p(D_MODEL), rows_p(D_GLA), rows_p(D_RWKV),
                  rows_s(D_MODEL), rows_s(D_GLA), rows_s(D_RWKV),
                  full(wo.shape), full((1, D_MODEL)), full(rw.shape), full((1, ROUTE_LANES))],
        out_specs=[rows(D_MODEL), rows(D_MODEL), rows(ROUTE_LANES)],
        out_shape=[jax.ShapeDtypeStruct((m, D_MODEL), F32),
                   jax.ShapeDtypeStruct((m, D_MODEL), F32),
                   jax.ShapeDtypeStruct((m, ROUTE_LANES), F32)],
        compiler_params=_cparams(("arbitrary",)),
        name="out_router",
    )(xp, gp, rp, xs, gs, rs, wo, ffn_norm, rw, rb)


def _for_rows(n_blocks, fn):
    def blk(bi, cy):
        for q in range(DMA_UNROLL):
            fn(bi * DMA_UNROLL + q)
        return cy
    lax.fori_loop(0, n_blocks, blk, 0)


def _expert_kernel(tile_expert_ref, tile_rows_ref, tile_pos_ref, order_ref,
                   h2_hbm, wg_ref, wu_ref, wd_ref, out_ref,
                   hb, wgb, wub, wdb, gsem, *, tile, sub, n_pairs):
    j = pl.program_id(0)
    nrows = tile_rows_ref[j]
    pos0 = tile_pos_ref[j]
    n_sub = (nrows + sub - 1) // sub

    def gather_copy(rr):
        pair = order_ref[jnp.minimum(pos0 + rr, n_pairs - 1)]
        return pltpu.make_async_copy(h2_hbm.at[pl.ds(pair >> 1, 1)], hb.at[pl.ds(rr, 1)], gsem)

    @pl.when(nrows > 0)
    def _():
        n_blk = n_sub * (sub // DMA_UNROLL)
        _for_rows(n_blk, lambda rr: gather_copy(rr).start())

        @pl.when((j == 0) | (tile_expert_ref[j] != tile_expert_ref[jnp.maximum(j - 1, 0)]))
        def _():
            wgb[...] = wg_ref[0, 0].astype(BF16)
            wub[...] = wu_ref[0, 0].astype(BF16)
            wdb[...] = wd_ref[0, 0].astype(BF16)

        _for_rows(n_blk, lambda rr: gather_copy(rr).wait())
        for sb in range(tile // sub):
            rs = slice(sb * sub, (sb + 1) * sub)

            @pl.when(sb * sub < nrows)
            def _():
                h2 = hb[rs, :].astype(BF16)
                gate = jnp.dot(h2, wgb[...], preferred_element_type=F32)
                up = jnp.dot(h2, wub[...], preferred_element_type=F32)
                hid = (gate * jax.nn.sigmoid(gate)) * up
                out_ref[rs, :] = jnp.dot(hid.astype(BF16), wdb[...], preferred_element_type=F32)

            @pl.when(sb * sub >= nrows)
            def _():
                out_ref[rs, :] = jnp.zeros((sub, D_MODEL), F32)

    @pl.when(nrows == 0)
    def _():
        out_ref[...] = jnp.zeros((tile, D_MODEL), F32)


def _experts(h2, plan, wg, wu, wd, *, tile, sub):
    tile_expert, tile_rows, tile_pos, order = plan
    n_tiles = tile_expert.shape[0]

    def w_idx(j, te, tr, tp, od):
        return (te[j] // EXPERTS_PER_GROUP, te[j] % EXPERTS_PER_GROUP, 0, 0)

    grid_spec = pltpu.PrefetchScalarGridSpec(
        num_scalar_prefetch=4,
        grid=(n_tiles,),
        in_specs=[pl.BlockSpec(memory_space=pl.ANY),
                  pl.BlockSpec((1, 1, D_MODEL, D_EXPERT), w_idx),
                  pl.BlockSpec((1, 1, D_MODEL, D_EXPERT), w_idx),
                  pl.BlockSpec((1, 1, D_EXPERT, D_MODEL), w_idx)],
        out_specs=pl.BlockSpec((tile, D_MODEL), lambda j, te, tr, tp, od: (j, 0)),
        scratch_shapes=[pltpu.VMEM((tile, D_MODEL), F32),
                        pltpu.VMEM((D_MODEL, D_EXPERT), BF16),
                        pltpu.VMEM((D_MODEL, D_EXPERT), BF16),
                        pltpu.VMEM((D_EXPERT, D_MODEL), BF16),
                        pltpu.SemaphoreType.DMA(())],
    )
    return pl.pallas_call(
        functools.partial(_expert_kernel, tile=tile, sub=sub, n_pairs=order.shape[0]),
        grid_spec=grid_spec,
        out_shape=jax.ShapeDtypeStruct((n_tiles * tile, D_MODEL), F32),
        compiler_params=_cparams(("arbitrary",)),
        name="moe_experts",
    )(tile_expert, tile_rows, tile_pos, order, h2, wg, wu, wd)


def _combine_kernel(slot_ref, x1_ref, info_ref, fn_ref, contrib_hbm, yp_ref, ys_ref, cbuf, gsem, *, tm, tiles_p):
    i = pl.program_id(0)

    def gather_copy(k, rr):
        pair = 2 * (i * tm + rr) + k
        return pltpu.make_async_copy(contrib_hbm.at[pl.ds(slot_ref[pair], 1)], cbuf.at[pl.ds(k * tm + rr, 1)], gsem)

    n_blk = tm // DMA_UNROLL
    for k in range(2):
        _for_rows(n_blk, lambda rr: gather_copy(k, rr).start())
    for k in range(2):
        _for_rows(n_blk, lambda rr: gather_copy(k, rr).wait())
    info = info_ref[...]
    x2 = x1_ref[...] + info[:, 2:3] * cbuf[:tm, :] + info[:, 3:4] * cbuf[tm:, :]
    y = _rms(x2, fn_ref[...])

    @pl.when(i < tiles_p)
    def _():
        yp_ref[...] = y

    @pl.when(i >= tiles_p)
    def _():
        ys_ref[...] = y


def _combine(slot, x1, info, final_norm, contrib, *, tm, rows_p):
    m = x1.shape[0]
    tiles_p = rows_p // tm
    n_tiles = m // tm
    grid_spec = pltpu.PrefetchScalarGridSpec(
        num_scalar_prefetch=1,
        grid=(n_tiles,),
        in_specs=[pl.BlockSpec((tm, D_MODEL), lambda i, sl: (i, 0)),
                  pl.BlockSpec((tm, ROUTE_LANES), lambda i, sl: (i, 0)),
                  pl.BlockSpec((1, D_MODEL), lambda i, sl: (0, 0)),
                  pl.BlockSpec(memory_space=pl.ANY)],
        out_specs=[pl.BlockSpec((tm, D_MODEL), lambda i, sl: (jnp.minimum(i, tiles_p - 1), 0)),
                   pl.BlockSpec((tm, D_MODEL), lambda i, sl: (jnp.maximum(i - tiles_p, 0), 0))],
        scratch_shapes=[pltpu.VMEM((2 * tm, D_MODEL), F32), pltpu.SemaphoreType.DMA(())],
    )
    return pl.pallas_call(
        functools.partial(_combine_kernel, tm=tm, tiles_p=tiles_p),
        grid_spec=grid_spec,
        out_shape=[jax.ShapeDtypeStruct((rows_p, D_MODEL), F32),
                   jax.ShapeDtypeStruct((m - rows_p, D_MODEL), F32)],
        compiler_params=_cparams(("arbitrary",)),
        name="moe_combine",
    )(slot, x1, info, final_norm, contrib)


def _pair_plan(expert_of_pair, tile):
    n = expert_of_pair.shape[0]
    n_tiles = n // tile + N_EXPERTS
    ids = jnp.arange(n, dtype=jnp.int32)
    _, order = lax.sort((expert_of_pair, ids), num_keys=1)
    experts = jnp.arange(N_EXPERTS, dtype=jnp.int32)
    onehot = expert_of_pair[:, None] == experts[None, :]
    counts = jnp.sum(onehot, axis=0).astype(jnp.int32)
    tiles_per = (counts + tile - 1) // tile
    tile_end = jnp.cumsum(tiles_per)
    tile_start = tile_end - tiles_per
    row_start = jnp.cumsum(counts) - counts
    t = jnp.arange(n_tiles, dtype=jnp.int32)
    used = t < tile_end[-1]
    exp_t = jnp.minimum(jnp.sum(t[:, None] >= tile_end[None, :], axis=1), N_EXPERTS - 1).astype(jnp.int32)
    last_expert = jnp.max(jnp.where(counts > 0, experts, 0))
    tile_expert = jnp.where(used, exp_t, last_expert).astype(jnp.int32)
    local = (t - tile_start[exp_t]) * tile
    tile_rows = jnp.where(used, jnp.clip(counts[exp_t] - local, 0, tile), 0).astype(jnp.int32)
    tile_pos = jnp.where(used, row_start[exp_t] + local, 0).astype(jnp.int32)
    _, rank = lax.sort((order, ids), num_keys=1)
    shift = tile_start * tile - row_start
    slot = rank + jnp.sum(jnp.where(onehot, shift[None, :], 0), axis=1)
    plan = (tile_expert, tile_rows, tile_pos, order)
    return plan, slot.astype(jnp.int32)


def _prep_params(mix_norm, w_in, gla_gate_w1, gla_gate_w2, gla_gate_b, gla_norm, rwkv_mu_rkv, rwkv_mu_wag,
                 rwkv_w0, rwkv_w1, rwkv_w2, rwkv_a0, rwkv_a1, rwkv_a2, rwkv_g1, rwkv_g2,
                 rwkv_k_k, rwkv_k_a, rwkv_r_k, rwkv_ln_w, rwkv_ln_b, w_out, ffn_norm,
                 router_group_w, router_group_b, router_expert_w, router_expert_b, final_norm):
    row = lambda v: v.reshape(1, -1).astype(F32)
    g_rank = 2 * LANE
    p = dict(
        mix_norm=row(mix_norm[0]),
        mu_wag=rwkv_mu_wag[0],
        w_in=w_in[0].astype(BF16),
        gw1=_pad_axis(gla_gate_w1[0], 1, LANE).astype(BF16),
        gw2=_pad_axis(gla_gate_w2[0], 0, LANE).astype(BF16),
        gb=row(gla_gate_b[0]),
        w1=_pad_axis(rwkv_w1[0], 1, LANE).astype(BF16),
        w2=_pad_axis(rwkv_w2[0], 0, LANE).astype(BF16),
        w0=row(rwkv_w0[0]),
        a1=_pad_axis(rwkv_a1[0], 1, LANE).astype(BF16),
        a2=_pad_axis(rwkv_a2[0], 0, LANE).astype(BF16),
        a0=row(rwkv_a0[0]),
        g1=_pad_axis(rwkv_g1[0], 1, g_rank).astype(BF16),
        g2=_pad_axis(rwkv_g2[0], 0, g_rank).astype(BF16),
        gla_norm=row(gla_norm[0]),
        mu_ext=jnp.concatenate([jnp.zeros((1, RWKV_OFF), F32), rwkv_mu_rkv[0].reshape(1, -1)], axis=1),
        k_k=row(rwkv_k_k[0]), k_a=row(rwkv_k_a[0]), r_k=row(rwkv_r_k[0]),
        ln_w=row(rwkv_ln_w[0]), ln_b=row(rwkv_ln_b[0]),
        wo=w_out[0].astype(BF16).reshape(2, D_GLA, D_MODEL),
        ffn_norm=row(ffn_norm[0]),
        router_w=_pad_axis(jnp.concatenate([router_group_w[0], router_expert_w[0]], axis=1), 1,
                           ROUTE_LANES).astype(BF16),
        router_b=_pad_axis(jnp.concatenate([router_group_b[0], router_expert_b[0]]).reshape(1, -1), 1,
                           ROUTE_LANES).astype(F32),
        final_norm=row(final_norm),
    )
    r = jnp.arange(MXU_TILE)
    p['gla_e'] = (r[:, None] // GLA_DK == jnp.arange(2 * GLA_DV)[None, :] // GLA_DV).astype(BF16)
    p['wkv_bd'] = (r[:, None] // RWKV_HEAD == r[None, :] // RWKV_HEAD).astype(BF16)
    return p


def _mixer(x, s_gla, s_wkv, s_shift, p, *, tm, tc):
    nb, seq, _ = x.shape
    m = nb * seq
    x2 = x.reshape(m, D_MODEL)
    multi_seq = tm > seq

    h_last = jnp.zeros((nb, D_MODEL), F32) if s_shift is None else s_shift
    zl = _plain_proj(_pad_axis(h_last, 0, -(-nb // 16) * 16).astype(BF16), p['w_in'], tn=1024)[:nb]
    if multi_seq:
        expand = lambda a: jnp.repeat(a, seq, axis=0)
        hprev, zprev = expand(h_last), expand(zl)
    else:
        hprev, zprev = h_last.reshape(nb, 1, D_MODEL), zl.reshape(nb, 1, D_IN)

    h_bf, la, lw, a_sig, gate, hlast = _norm_proj(x2, hprev, p, seq=seq, tm=tm)
    shift_new = hlast.reshape(nb, seq, D_MODEL)[:, -1] if multi_seq else hlast.reshape(nb, D_MODEL)
    z = _in_proj(h_bf, p['w_in'], p['mu_ext'], zprev, seq=seq, tm=tm if multi_seq else min(seq, 1024), tn=1024)

    z3 = z.reshape(nb, seq, D_IN)
    r3 = lambda a: a.reshape(nb, seq, a.shape[-1])
    if s_gla is None:
        s_gla = jnp.zeros((nb, GLA_HEADS, GLA_DK, GLA_DV), F32)
        s_wkv = jnp.zeros((nb, RWKV_HEADS, RWKV_HEAD, RWKV_HEAD), F32)
    og, gla_new = _gla(z3, r3(la), s_gla, p['gla_e'], p['gla_norm'], tc=tc, chunk=min(GLA_CHUNK, seq),
                       nseq=2 if nb % 2 == 0 else 1)
    wkv_seqs = 4 if nb % 4 == 0 else 1
    orr, wkv_new = _wkv(z3, r3(lw), r3(a_sig), r3(gate), s_wkv, p['wkv_bd'], p, tc=min(tc, 512 // wkv_seqs),
                        chunk=min(WKV_CHUNK, seq), nseq=wkv_seqs)
    return (x2, og.reshape(m, D_GLA), orr.reshape(m, D_RWKV)), (gla_new, wkv_new, shift_new)


def _ffn(rows_p, rows_s, p, wg, wu, wd, *, tm, moe_tile, moe_sub):
    mp = rows_p[0].shape[0]
    x1, h2, info = _out_router(*rows_p, *rows_s, p['wo'], p['ffn_norm'], p['router_w'], p['router_b'], tm=tm)
    expert_of_pair = info[:, :2].astype(jnp.int32).reshape(-1)
    plan, slot = _pair_plan(expert_of_pair, moe_tile)
    contrib = _experts(h2, plan, wg, wu, wd, tile=moe_tile, sub=moe_sub)
    return _combine(slot, x1, info, p['final_norm'], contrib, tm=tm, rows_p=mp)


def kernel(x_prompt, x_sample, state_gla, state_wkv, state_shift, mix_norm, w_in, gla_gate_w1, gla_gate_w2, gla_gate_b, gla_norm, rwkv_mu_rkv, rwkv_mu_wag, rwkv_w0, rwkv_w1, rwkv_w2, rwkv_a0, rwkv_a1, rwkv_a2, rwkv_g1, rwkv_g2, rwkv_k_k, rwkv_k_a, rwkv_r_k, rwkv_ln_w, rwkv_ln_b, w_out, ffn_norm, router_group_w, router_group_b, router_expert_w, router_expert_b, expert_w_gate, expert_w_up, expert_w_down, final_norm):
    p = _prep_params(mix_norm, w_in, gla_gate_w1, gla_gate_w2, gla_gate_b, gla_norm, rwkv_mu_rkv, rwkv_mu_wag,
                     rwkv_w0, rwkv_w1, rwkv_w2, rwkv_a0, rwkv_a1, rwkv_a2, rwkv_g1, rwkv_g2,
                     rwkv_k_k, rwkv_k_a, rwkv_r_k, rwkv_ln_w, rwkv_ln_b, w_out, ffn_norm,
                     router_group_w, router_group_b, router_expert_w, router_expert_b, final_norm)
    wg, wu, wd = (w.reshape(w.shape[1:]) for w in (expert_w_gate, expert_w_up, expert_w_down))
    nb_p, seq_p, _ = x_prompt.shape
    nb_s, seq_s, _ = x_sample.shape
    rows_p, (gla_p, wkv_p, shift_p) = _mixer(x_prompt, None, None, None, p, tm=256, tc=min(256, seq_p))
    first = lambda a: a.reshape(a.shape[1:])
    rows_s, (gla_s, wkv_s, shift_s) = _mixer(x_sample, first(state_gla), first(state_wkv), first(state_shift), p,
                                             tm=nb_s * seq_s, tc=seq_s)
    y_p, y_s = _ffn(rows_p, rows_s, p, wg, wu, wd, tm=256, moe_tile=MOE_TILE, moe_sub=MOE_SUB)
    return (y_p.reshape(nb_p, seq_p, D_MODEL), y_s.reshape(nb_s, seq_s, D_MODEL),
            gla_p[None], wkv_p[None], shift_p[None], gla_s[None], wkv_s[None], shift_s[None])
```

```python
import functools

import jax
import jax.numpy as jnp
from jax import lax
from jax.experimental import pallas as pl
from jax.experimental.pallas import tpu as pltpu

F32 = jnp.float32
BF16 = jnp.bfloat16

D_MODEL = 2048
D_GLA = 1024
D_RWKV = 1024
GLA_HEADS = 4
GLA_DV = 256
GLA_DK = 128
GLA_K = 512
GLA_GATE_TEMP = 16.0
RWKV_HEAD = 64
RWKV_HEADS = 16
N_GROUPS = 8
EXPERTS_PER_GROUP = 8
D_EXPERT = 512
NORM_EPS = 1e-6
GN_EPS = 64e-5
RWKV_OFF = 2 * GLA_K + 2 * D_GLA
D_IN = RWKV_OFF + 3 * D_RWKV

LANE = 128
SUBLANE = 8
MXU_TILE = 256
VMEM_LIMIT = 56 * 1024 * 1024

GLA_CHUNK = 16
WKV_CHUNK = 8
N_EXPERTS = N_GROUPS * EXPERTS_PER_GROUP
MOE_TILE = 128
ROUTE_LANES = 128


def _cparams(sem):
    return pltpu.CompilerParams(dimension_semantics=sem, vmem_limit_bytes=VMEM_LIMIT)


def _rms(x, gain):
    return x * lax.rsqrt(jnp.mean(x * x, axis=-1, keepdims=True) + NORM_EPS) * gain


def _softplus(y):
    return jnp.maximum(y, 0.0) + jnp.log1p(jnp.exp(-jnp.abs(y)))


def _dot(a, b):
    return jnp.dot(a.astype(BF16), b.astype(BF16), preferred_element_type=F32)


def _dot_nt(a, b):
    return lax.dot_general(a.astype(BF16), b.astype(BF16), (((1,), (1,)), ((), ())),
                           preferred_element_type=F32)


def _dot_tn(a, b):
    return lax.dot_general(a.astype(BF16), b.astype(BF16), (((0,), (0,)), ((), ())),
                           preferred_element_type=F32)


def _norm_proj_kernel(x_ref, hprev_ref, gain_ref, mu_ref, gw1_ref, gw2_ref, gb_ref,
                      w1_ref, w2_ref, w0_ref, a1_ref, a2_ref, a0_ref, g1_ref, g2_ref,
                      h_ref, la_ref, lw_ref, a_ref, gate_ref, hlast_ref, carry_ref,
                      *, tm, seq, multi_seq):
    i = pl.program_id(0)
    h = _rms(x_ref[...], gain_ref[...])
    row = lax.broadcasted_iota(jnp.int32, (tm, 1), 0)
    rolled = pltpu.roll(h, 1, 0)
    if multi_seq:
        h_shift = jnp.where(row % seq == 0, hprev_ref[...], rolled)
        hlast_ref[...] = h
    else:
        tiles_per_seq = seq // tm
        first = (i % tiles_per_seq) == 0
        prev = jnp.where(first, hprev_ref[0], carry_ref[...])
        h_shift = jnp.where(row == 0, prev, rolled)
        carry_ref[...] = h[tm - 1:tm]
        hlast_ref[0] = h[tm - 1:tm]
    h_ref[...] = h.astype(BF16)

    dx = h_shift - h
    mu = mu_ref[...]
    xw = h + dx * mu[0:1]
    xa = h + dx * mu[1:2]
    xg = h + dx * mu[2:3]

    gl = _dot(_dot(h, gw1_ref[...]), gw2_ref[...]) + gb_ref[...]
    la_ref[...] = -_softplus(-gl) * (1.0 / GLA_GATE_TEMP)

    u = w0_ref[...] + _dot(jnp.tanh(_dot(xw, w1_ref[...])), w2_ref[...])
    w_log = -_softplus(-u) - 0.5
    lw_ref[...] = -jnp.exp(w_log)

    a_ref[...] = jax.nn.sigmoid(a0_ref[...] + _dot(_dot(xa, a1_ref[...]), a2_ref[...]))
    gate_ref[...] = _dot(jax.nn.sigmoid(_dot(xg, g1_ref[...])), g2_ref[...])


def _pad_axis(w, axis, to):
    pad = [(0, 0)] * w.ndim
    pad[axis] = (0, to - w.shape[axis])
    return jnp.pad(w, pad)


def _norm_proj(x2, hprev, p, *, seq, tm):
    m = x2.shape[0]
    multi_seq = tm > seq
    nb = m // seq
    full = lambda shape: pl.BlockSpec(shape, lambda i: (0,) * len(shape))
    rows = lambda w: pl.BlockSpec((tm, w), lambda i: (i, 0))
    if multi_seq:
        assert tm == m
        hprev_spec = rows(D_MODEL)
        hlast_shape = jax.ShapeDtypeStruct((m, D_MODEL), F32)
        hlast_spec = rows(D_MODEL)
    else:
        assert seq % tm == 0
        tps = seq // tm
        hprev_spec = pl.BlockSpec((1, 1, D_MODEL), lambda i: (i // tps, 0, 0))
        hlast_shape = jax.ShapeDtypeStruct((nb, 1, D_MODEL), F32)
        hlast_spec = pl.BlockSpec((1, 1, D_MODEL), lambda i: (i // tps, 0, 0))
    weights = [p['gw1'], p['gw2'], p['gb'], p['w1'], p['w2'], p['w0'], p['a1'], p['a2'], p['a0'],
               p['g1'], p['g2']]
    return pl.pallas_call(
        functools.partial(_norm_proj_kernel, tm=tm, seq=seq, multi_seq=multi_seq),
        grid=(m // tm,),
        in_specs=[rows(D_MODEL), hprev_spec, full((1, D_MODEL)), full((3, D_MODEL))]
                 + [full(w.shape) for w in weights],
        out_specs=[rows(D_MODEL), rows(GLA_K), rows(D_RWKV), rows(D_RWKV), rows(D_RWKV), hlast_spec],
        out_shape=[jax.ShapeDtypeStruct((m, D_MODEL), BF16),
                   jax.ShapeDtypeStruct((m, GLA_K), F32),
                   jax.ShapeDtypeStruct((m, D_RWKV), F32),
                   jax.ShapeDtypeStruct((m, D_RWKV), F32),
                   jax.ShapeDtypeStruct((m, D_RWKV), F32),
                   hlast_shape],
        scratch_shapes=[pltpu.VMEM((1, D_MODEL), F32)],
        compiler_params=_cparams(("arbitrary",)),
        name="norm_proj",
    )(x2, hprev, p['mix_norm'], p['mu_wag'], *weights)


def _in_proj_kernel(h_ref, w_ref, mu_ref, zprev_ref, z_ref, carry_ref, *, tm, seq, multi_seq):
    i = pl.program_id(1)
    z = jnp.dot(h_ref[...], w_ref[...], preferred_element_type=F32)
    row = lax.broadcasted_iota(jnp.int32, (tm, 1), 0)
    rolled = pltpu.roll(z, 1, 0)
    if multi_seq:
        z_prev = jnp.where(row % seq == 0, zprev_ref[...], rolled)
    else:
        tiles_per_seq = seq // tm
        first = (i % tiles_per_seq) == 0
        prev = jnp.where(first, zprev_ref[0], carry_ref[...])
        z_prev = jnp.where(row == 0, prev, rolled)
        carry_ref[...] = z[tm - 1:tm]
    z_ref[...] = z + mu_ref[...] * (z_prev - z)


def _in_proj(h_bf, w_in, mu_ext, zprev, *, seq, tm, tn):
    m = h_bf.shape[0]
    multi_seq = tm > seq
    if multi_seq:
        zprev_spec = pl.BlockSpec((tm, tn), lambda j, i: (i, j))
    else:
        tps = seq // tm
        zprev_spec = pl.BlockSpec((1, 1, tn), lambda j, i: (i // tps, 0, j))
    return pl.pallas_call(
        functools.partial(_in_proj_kernel, tm=tm, seq=seq, multi_seq=multi_seq),
        grid=(D_IN // tn, m // tm),
        in_specs=[pl.BlockSpec((tm, D_MODEL), lambda j, i: (i, 0)),
                  pl.BlockSpec((D_MODEL, tn), lambda j, i: (0, j)),
                  pl.BlockSpec((1, tn), lambda j, i: (0, j)),
                  zprev_spec],
        out_specs=pl.BlockSpec((tm, tn), lambda j, i: (i, j)),
        out_shape=jax.ShapeDtypeStruct((m, D_IN), F32),
        scratch_shapes=[pltpu.VMEM((1, tn), F32)],
        compiler_params=_cparams(("arbitrary", "arbitrary")),
        name="in_proj",
    )(h_bf, w_in, mu_ext, zprev)


def _plain_proj_kernel(h_ref, w_ref, z_ref):
    z_ref[...] = jnp.dot(h_ref[...], w_ref[...], preferred_element_type=F32)


def _plain_proj(h_bf, w_in, *, tn):
    m = h_bf.shape[0]
    return pl.pallas_call(
        _plain_proj_kernel,
        grid=(D_IN // tn,),
        in_specs=[pl.BlockSpec((m, D_MODEL), lambda j: (0, 0)),
                  pl.BlockSpec((D_MODEL, tn), lambda j: (0, j))],
        out_specs=pl.BlockSpec((m, tn), lambda j: (0, j)),
        out_shape=jax.ShapeDtypeStruct((m, D_IN), F32),
        compiler_params=_cparams(("arbitrary",)),
        name="prev_row_proj",
    )(h_bf, w_in)


def _gla_kernel(q_ref, k_ref, v_ref, g_ref, la_ref, s0_ref, e_ref, gn_ref,
                o_ref, s_ref, st_ref, obuf_ref, *, tc, chunk, nseq):
    tb = pl.program_id(1)
    c = chunk

    @pl.when(tb == 0)
    def _():
        for q in range(nseq):
            for hh in range(GLA_HEADS):
                st_ref[q, :, hh * GLA_DK:(hh + 1) * GLA_DK] = s0_ref[q, hh].T

    row = lax.broadcasted_iota(jnp.int32, (c, 1), 0)
    e_mat = e_ref[...]
    neg = jnp.float32(-1e30)

    def one_chunk(q, r0):
        qq = q_ref[q, pl.ds(r0, c), :] * (GLA_DK ** -0.5)
        k = k_ref[q, pl.ds(r0, c), :]
        v = v_ref[q, pl.ds(r0, c), :]
        la = la_ref[q, pl.ds(r0, c), :]
        b = jnp.zeros_like(la)
        for s in range(c):
            b = b + jnp.where(row >= s, la[s:s + 1], 0.0)
        b_last = b[c - 1:c]

        parts = []
        for s in range(c):
            dec = jnp.exp(jnp.where(row >= s, b - b[s:s + 1], neg))
            parts.append((qq * k[s:s + 1] * dec).astype(BF16))
        pcat = jnp.concatenate(parts, axis=0)
        sc = jnp.concatenate(
            [jnp.dot(pcat[:, hp * MXU_TILE:(hp + 1) * MXU_TILE], e_mat, preferred_element_type=F32)
             for hp in range(GLA_K // MXU_TILE)], axis=1)
        o = jnp.zeros((c, D_GLA), F32)
        for s in range(c):
            o = o + sc[s * c:(s + 1) * c] * v[s:s + 1]

        qd = qq * jnp.exp(b)
        kd = k * jnp.exp(b_last - b)
        a_last = jnp.exp(b_last)
        o_heads = []
        for hh in range(GLA_HEADS):
            dk = slice(hh * GLA_DK, (hh + 1) * GLA_DK)
            dv = slice(hh * GLA_DV, (hh + 1) * GLA_DV)
            st = st_ref[q, :, dk]
            oh = o[:, dv] + _dot_nt(qd[:, dk], st)
            st_ref[q, :, dk] = a_last[:, dk] * st + _dot_tn(v[:, dv], kd[:, dk])
            oh = oh * lax.rsqrt(jnp.mean(oh * oh, axis=-1, keepdims=True) + NORM_EPS)
            o_heads.append(oh)
        og = jnp.concatenate(o_heads, axis=1)
        g = g_ref[q, pl.ds(r0, c), :]
        obuf_ref[q, pl.ds(r0, c), :] = og * gn_ref[...] * (g * jax.nn.sigmoid(g))

    def body(ci, carry):
        r0 = pl.multiple_of(ci * c, c)
        for q in range(nseq):
            one_chunk(q, r0)
        return carry

    lax.fori_loop(0, tc // c, body, 0)
    o_ref[...] = obuf_ref[...].astype(BF16)

    @pl.when(tb == pl.num_programs(1) - 1)
    def _():
        for q in range(nseq):
            for hh in range(GLA_HEADS):
                s_ref[q, hh] = st_ref[q, :, hh * GLA_DK:(hh + 1) * GLA_DK].T


def _gla(z3, la3, s0, e_mat, gla_norm, *, tc, chunk, nseq):
    nb, seq, _ = z3.shape
    blk = lambda w, col: pl.BlockSpec((nseq, tc, w), lambda b, t: (b, t, col))
    st_spec = pl.BlockSpec((nseq, GLA_HEADS, GLA_DK, GLA_DV), lambda b, t: (b, 0, 0, 0))
    return pl.pallas_call(
        functools.partial(_gla_kernel, tc=tc, chunk=chunk, nseq=nseq),
        grid=(nb // nseq, seq // tc),
        in_specs=[blk(GLA_K, 0), blk(GLA_K, 1), blk(D_GLA, 1), blk(D_GLA, 2), blk(GLA_K, 0),
                  st_spec,
                  pl.BlockSpec(e_mat.shape, lambda b, t: (0, 0)),
                  pl.BlockSpec((1, D_GLA), lambda b, t: (0, 0))],
        out_specs=[blk(D_GLA, 0), st_spec],
        out_shape=[jax.ShapeDtypeStruct((nb, seq, D_GLA), BF16),
                   jax.ShapeDtypeStruct((nb, GLA_HEADS, GLA_DK, GLA_DV), F32)],
        scratch_shapes=[pltpu.VMEM((nseq, GLA_DV, GLA_K), F32), pltpu.VMEM((nseq, tc, D_GLA), F32)],
        compiler_params=_cparams(("arbitrary", "arbitrary")),
        name="gla",
    )(z3, z3, z3, z3, la3, s0, e_mat, gla_norm)


N_LANE_TILES = D_RWKV // MXU_TILE
HEADS_PER_TILE = MXU_TILE // RWKV_HEAD


def _stack_tiles(x):
    return jnp.concatenate([x[:, t * MXU_TILE:(t + 1) * MXU_TILE] for t in range(N_LANE_TILES)], axis=0)


def _unstack_tiles(y, r):
    return jnp.concatenate([y[t * r:(t + 1) * r] for t in range(N_LANE_TILES)], axis=1)


def _seg_sums(xs, bd):
    c = xs[0].shape[0]
    st = jnp.concatenate([_stack_tiles(x) for x in xs], axis=0)
    hi = st.astype(BF16)
    lo = (st - hi.astype(F32)).astype(BF16)
    res = jnp.dot(jnp.concatenate([hi, lo], axis=0), bd, preferred_element_type=F32)
    half = st.shape[0]
    res = res[:half] + res[half:]
    rows = N_LANE_TILES * c
    return [_unstack_tiles(res[n * rows:(n + 1) * rows], c) for n in range(len(xs))]


def _cumsum_rows(x, row, c):
    if c == SUBLANE:
        step = 1
        while step < c:
            x = x + jnp.where(row >= step, pltpu.roll(x, step, 0), 0.0)
            step *= 2
        return x
    out = jnp.zeros_like(x)
    for s in range(c):
        out = out + jnp.where(row >= s, x[s:s + 1], 0.0)
    return out


def _wkv_kernel(r_ref, k_ref, v_ref, lw_ref, a_ref, gate_ref, s0_ref, bd_ref,
                kk_ref, ka_ref, rk_ref, lnw_ref, lnb_ref,
                o_ref, s_ref, st_ref, obuf_ref, *, tc, chunk, nseq):
    tb = pl.program_id(1)
    c = chunk
    cc = c * c

    @pl.when(tb == 0)
    def _():
        for q in range(nseq):
            st_ref[q] = jnp.concatenate([s0_ref[q, hh] for hh in range(RWKV_HEADS)], axis=1)

    row = lax.broadcasted_iota(jnp.int32, (c, 1), 0)
    bd = bd_ref[...]
    lane_head = lax.broadcasted_iota(jnp.int32, (RWKV_HEAD, MXU_TILE), 1) // RWKV_HEAD
    neg = jnp.float32(-1e30)

    def one_chunk(q, r0):
        r = r_ref[q, pl.ds(r0, c), :]
        k = k_ref[q, pl.ds(r0, c), :]
        v = v_ref[q, pl.ds(r0, c), :]
        lw = lw_ref[q, pl.ds(r0, c), :]
        a_sig = a_ref[q, pl.ds(r0, c), :]

        kk = k * kk_ref[...]
        k_eff = k * (1.0 + (a_sig - 1.0) * ka_ref[...])
        kk_ss, bonus_dot = _seg_sums([kk * kk, r * k_eff * rk_ref[...]], bd)
        kk = kk / jnp.maximum(jnp.sqrt(kk_ss), 1e-12)
        a_vec = -kk
        b_vec = kk * a_sig

        lb = _cumsum_rows(lw, row, c)
        lbp = lb - lw
        lb_last = lb[c - 1:c]
        kinds = ([], [], [], [])
        for s in range(c):
            lbs = lb[s:s + 1]
            bs = b_vec[s:s + 1]
            ks = k_eff[s:s + 1]
            a_d = a_vec * jnp.exp(jnp.where(row > s, lbp - lbs, neg))
            r_d = r * jnp.exp(jnp.where(row >= s, lb - lbs, neg))
            kinds[0].append(a_d * bs)
            kinds[1].append(a_d * ks)
            kinds[2].append(r_d * bs)
            kinds[3].append(r_d * ks)
        prod = jnp.concatenate(
            [p[:, t * MXU_TILE:(t + 1) * MXU_TILE]
             for t in range(N_LANE_TILES) for kind in kinds for p in kind], axis=0)
        coef_t = jnp.dot(prod.astype(BF16), bd, preferred_element_type=F32)

        def coef(kind, s):
            return jnp.concatenate(
                [coef_t[(t * 4 + kind) * cc + s * c:(t * 4 + kind) * cc + (s + 1) * c]
                 for t in range(N_LANE_TILES)], axis=1)

        lhs = jnp.concatenate([a_vec * jnp.exp(lbp), r * jnp.exp(lb)], axis=0).astype(BF16)
        x0 = []
        for t in range(N_LANE_TILES):
            ln = slice(t * MXU_TILE, (t + 1) * MXU_TILE)
            st_t = st_ref[q, :, ln].astype(BF16)
            w_t = jnp.concatenate([st_t] * HEADS_PER_TILE, axis=0) * bd
            x0.append(lax.dot_general(lhs[:, ln], w_t, (((1,), (1,)), ((), ())),
                                      preferred_element_type=F32))
        x0 = jnp.concatenate(x0, axis=1)
        u = x0[:c]
        o = x0[c:]

        for s in range(c):
            u = u + coef(1, s) * v[s:s + 1]
        for s in range(c):
            u = u + coef(0, s) * u[s:s + 1]
        for s in range(c):
            o = o + coef(2, s) * u[s:s + 1] + coef(3, s) * v[s:s + 1]

        tail = jnp.exp(lb_last - lb)
        uv = jnp.concatenate([u, v], axis=0).astype(BF16)
        bk = jnp.concatenate([b_vec * tail, k_eff * tail], axis=0).astype(BF16)
        w_last = jnp.exp(lb_last)
        for t in range(N_LANE_TILES):
            ln = slice(t * MXU_TILE, (t + 1) * MXU_TILE)
            zt = _dot_tn(uv[:, ln], bk[:, ln])
            upd = zt[(HEADS_PER_TILE - 1) * RWKV_HEAD:]
            for hh in range(HEADS_PER_TILE - 2, -1, -1):
                upd = jnp.where(lane_head == hh, zt[hh * RWKV_HEAD:(hh + 1) * RWKV_HEAD], upd)
            st_ref[q, :, ln] = w_last[:, ln] * st_ref[q, :, ln] + upd

        inv_n = 1.0 / RWKV_HEAD
        mu = _seg_sums([o], bd)[0] * inv_n
        dev = o - mu
        var = _seg_sums([dev * dev], bd)[0] * inv_n
        yn = dev * lax.rsqrt(var + GN_EPS) * lnw_ref[...] + lnb_ref[...]
        obuf_ref[q, pl.ds(r0, c), :] = (yn + bonus_dot * v) * gate_ref[q, pl.ds(r0, c), :]

    def body(ci, carry):
        r0 = pl.multiple_of(ci * c, c)
        for q in range(nseq):
            one_chunk(q, r0)
        return carry

    lax.fori_loop(0, tc // c, body, 0)
    o_ref[...] = obuf_ref[...].astype(BF16)

    @pl.when(tb == pl.num_programs(1) - 1)
    def _():
        for q in range(nseq):
            for hh in range(RWKV_HEADS):
                s_ref[q, hh] = st_ref[q, :, hh * RWKV_HEAD:(hh + 1) * RWKV_HEAD]


def _wkv(z3, lw3, a3, gate3, s0, bd, p, *, tc, chunk, nseq):
    nb, seq, _ = z3.shape
    col0 = RWKV_OFF // D_RWKV
    blk = lambda col: pl.BlockSpec((nseq, tc, D_RWKV), lambda b, t: (b, t, col))
    st_spec = pl.BlockSpec((nseq, RWKV_HEADS, RWKV_HEAD, RWKV_HEAD), lambda b, t: (b, 0, 0, 0))
    vec = pl.BlockSpec((1, D_RWKV), lambda b, t: (0, 0))
    return pl.pallas_call(
        functools.partial(_wkv_kernel, tc=tc, chunk=chunk, nseq=nseq),
        grid=(nb // nseq, seq // tc),
        in_specs=[blk(col0), blk(col0 + 1), blk(col0 + 2), blk(0), blk(0), blk(0), st_spec,
                  pl.BlockSpec(bd.shape, lambda b, t: (0, 0)), vec, vec, vec, vec, vec],
        out_specs=[blk(0), st_spec],
        out_shape=[jax.ShapeDtypeStruct((nb, seq, D_RWKV), BF16),
                   jax.ShapeDtypeStruct((nb, RWKV_HEADS, RWKV_HEAD, RWKV_HEAD), F32)],
        scratch_shapes=[pltpu.VMEM((nseq, RWKV_HEAD, D_RWKV), F32),
                        pltpu.VMEM((nseq, tc, D_RWKV), F32)],
        compiler_params=_cparams(("arbitrary", "arbitrary")),
        name="wkv",
    )(z3, z3, z3, lw3, a3, gate3, s0, bd, p['k_k'], p['k_a'], p['r_k'], p['ln_w'], p['ln_b'])


def _out_router_kernel(xp_ref, gp_ref, rp_ref, xs_ref, gs_ref, rs_ref, wo_ref, fg_ref, rw_ref, rb_ref,
                       x1_ref, h2_ref, info_ref, *, tm, tiles_p):
    is_p = pl.program_id(0) < tiles_p
    x = jnp.where(is_p, xp_ref[...], xs_ref[...])
    og = jnp.where(is_p, gp_ref[...], gs_ref[...])
    orr = jnp.where(is_p, rp_ref[...], rs_ref[...])
    mix = (jnp.dot(og, wo_ref[0], preferred_element_type=F32)
           + jnp.dot(orr, wo_ref[1], preferred_element_type=F32))
    x1 = x + mix
    x1_ref[...] = x1
    h2 = _rms(x1, fg_ref[...])
    h2_ref[...] = h2
    logits = _dot(h2, rw_ref[...]) + rb_ref[...]
    lane = lax.broadcasted_iota(jnp.int32, (tm, ROUTE_LANES), 1)
    neg = jnp.float32(-jnp.inf)
    big = jnp.int32(1 << 20)

    def first_argmax(vals):
        mx = jnp.max(vals, axis=-1, keepdims=True)
        idx = jnp.min(jnp.where(vals == mx, lane, big), axis=-1, keepdims=True)
        return mx, idx

    lg = jnp.where(lane < N_GROUPS, logits, neg)
    g_max, g_idx = first_argmax(lg)
    p_group = 1.0 / jnp.sum(jnp.exp(lg - g_max), axis=-1, keepdims=True)
    lo = N_GROUPS + g_idx * EXPERTS_PER_GROUP
    le = jnp.where((lane >= lo) & (lane < lo + EXPERTS_PER_GROUP), logits, neg)
    m1, i1 = first_argmax(le)
    m2, i2 = first_argmax(jnp.where(lane == i1, neg, le))
    e2 = jnp.exp(m2 - m1)
    w1 = p_group / (1.0 + e2)
    w2 = p_group * e2 / (1.0 + e2)
    info_ref[...] = jnp.where(lane == 0, (i1 - N_GROUPS).astype(F32),
                              jnp.where(lane == 1, (i2 - N_GROUPS).astype(F32),
                                        jnp.where(lane == 2, w1, jnp.where(lane == 3, w2, 0.0))))


def _out_router(xp, gp, rp, xs, gs, rs, wo, ffn_norm, rw, rb, *, tm):
    mp, ms = xp.shape[0], xs.shape[0]
    tiles_p, tiles_s = mp // tm, ms // tm
    rows_p = lambda w: pl.BlockSpec((tm, w), lambda i: (jnp.minimum(i, tiles_p - 1), 0))
    rows_s = lambda w: pl.BlockSpec((tm, w), lambda i: (jnp.maximum(i - tiles_p, 0), 0))
    rows = lambda w: pl.BlockSpec((tm, w), lambda i: (i, 0))
    full = lambda shape: pl.BlockSpec(shape, lambda i: (0,) * len(shape))
    m = mp + ms
    return pl.pallas_call(
        functools.partial(_out_router_kernel, tm=tm, tiles_p=tiles_p),
        grid=(tiles_p + tiles_s,),
        in_specs=[rows_p(D_MODEL), rows_p(D_GLA), rows_p(D_RWKV),
                  rows_s(D_MODEL), rows_s(D_GLA), rows_s(D_RWKV),
                  full(wo.shape), full((1, D_MODEL)), full(rw.shape), full((1, ROUTE_LANES))],
        out_specs=[rows(D_MODEL), rows(D_MODEL), rows(ROUTE_LANES)],
        out_shape=[jax.ShapeDtypeStruct((m, D_MODEL), F32),
                   jax.ShapeDtypeStruct((m, D_MODEL), F32),
                   jax.ShapeDtypeStruct((m, ROUTE_LANES), F32)],
        compiler_params=_cparams(("arbitrary",)),
        name="out_router",
    )(xp, gp, rp, xs, gs, rs, wo, ffn_norm, rw, rb)


def _expert_kernel(tile_expert_ref, tile_rows_ref, tile_pos_ref, order_ref,
                   h2_hbm, wg_ref, wu_ref, wd_ref, out_ref,
                   hb, wgb, wub, wdb, gsem, *, tile, n_pairs):
    j = pl.program_id(0)
    nrows = tile_rows_ref[j]
    buf = j % 2

    def gather_copy(t, b, rr):
        pair = order_ref[jnp.minimum(tile_pos_ref[t] + rr, n_pairs - 1)]
        return pltpu.make_async_copy(h2_hbm.at[pl.ds(pair >> 1, 1)], hb.at[b, pl.ds(rr, 1)], gsem.at[b])

    @pl.when(j == 0)
    def _():
        for rr in range(tile):
            gather_copy(0, 0, rr).start()

    @pl.when((j == 0) | (tile_rows_ref[jnp.maximum(j - 1, 0)] > 0))
    def _():
        pltpu.make_async_copy(h2_hbm.at[pl.ds(0, tile)], hb.at[buf], gsem.at[buf]).wait()

    @pl.when(nrows > 0)
    def _():
        @pl.when((j == 0) | (tile_expert_ref[j] != tile_expert_ref[jnp.maximum(j - 1, 0)]))
        def _():
            wgb[...] = wg_ref[0, 0].astype(BF16)
            wub[...] = wu_ref[0, 0].astype(BF16)
            wdb[...] = wd_ref[0, 0].astype(BF16)

        for rr in range(tile):
            gather_copy(j + 1, 1 - buf, rr).start()
        h2 = hb[buf].astype(BF16)
        gate = jnp.dot(h2, wgb[...], preferred_element_type=F32)
        up = jnp.dot(h2, wub[...], preferred_element_type=F32)
        hid = (gate * jax.nn.sigmoid(gate)) * up
        out_ref[...] = jnp.dot(hid.astype(BF16), wdb[...], preferred_element_type=F32)

    @pl.when(nrows == 0)
    def _():
        out_ref[...] = jnp.zeros((tile, D_MODEL), F32)


def _experts(h2, plan, wg, wu, wd, *, tile):
    tile_expert, tile_rows, tile_pos, order = plan
    n_tiles = tile_expert.shape[0]

    def w_idx(j, te, tr, tp, od):
        return (te[j] // EXPERTS_PER_GROUP, te[j] % EXPERTS_PER_GROUP, 0, 0)

    grid_spec = pltpu.PrefetchScalarGridSpec(
        num_scalar_prefetch=4,
        grid=(n_tiles,),
        in_specs=[pl.BlockSpec(memory_space=pl.ANY),
                  pl.BlockSpec((1, 1, D_MODEL, D_EXPERT), w_idx),
                  pl.BlockSpec((1, 1, D_MODEL, D_EXPERT), w_idx),
                  pl.BlockSpec((1, 1, D_EXPERT, D_MODEL), w_idx)],
        out_specs=pl.BlockSpec((tile, D_MODEL), lambda j, te, tr, tp, od: (j, 0)),
        scratch_shapes=[pltpu.VMEM((2, tile, D_MODEL), F32),
                        pltpu.VMEM((D_MODEL, D_EXPERT), BF16),
                        pltpu.VMEM((D_MODEL, D_EXPERT), BF16),
                        pltpu.VMEM((D_EXPERT, D_MODEL), BF16),
                        pltpu.SemaphoreType.DMA((2,))],
    )
    return pl.pallas_call(
        functools.partial(_expert_kernel, tile=tile, n_pairs=order.shape[0]),
        grid_spec=grid_spec,
        out_shape=jax.ShapeDtypeStruct((n_tiles * tile, D_MODEL), F32),
        compiler_params=_cparams(("arbitrary",)),
        name="moe_experts",
    )(tile_expert, tile_rows, tile_pos, order, h2, wg, wu, wd)


def _combine_kernel(slot_ref, x1_ref, info_ref, fn_ref, contrib_hbm, yp_ref, ys_ref, cbuf, gsem, *, tm, tiles_p):
    i = pl.program_id(0)

    def gather_copy(k, rr):
        pair = 2 * (i * tm + rr) + k
        return pltpu.make_async_copy(contrib_hbm.at[pl.ds(slot_ref[pair], 1)], cbuf.at[pl.ds(k * tm + rr, 1)], gsem)

    for k in range(2):
        for rr in range(tm):
            gather_copy(k, rr).start()
    pltpu.make_async_copy(contrib_hbm.at[pl.ds(0, 2 * tm)], cbuf, gsem).wait()
    info = info_ref[...]
    x2 = x1_ref[...] + info[:, 2:3] * cbuf[:tm, :] + info[:, 3:4] * cbuf[tm:, :]
    y = _rms(x2, fn_ref[...])

    @pl.when(i < tiles_p)
    def _():
        yp_ref[...] = y

    @pl.when(i >= tiles_p)
    def _():
        ys_ref[...] = y


def _combine(slot, x1, info, final_norm, contrib, *, tm, rows_p):
    m = x1.shape[0]
    tiles_p = rows_p // tm
    n_tiles = m // tm
    grid_spec = pltpu.PrefetchScalarGridSpec(
        num_scalar_prefetch=1,
        grid=(n_tiles,),
        in_specs=[pl.BlockSpec((tm, D_MODEL), lambda i, sl: (i, 0)),
                  pl.BlockSpec((tm, ROUTE_LANES), lambda i, sl: (i, 0)),
                  pl.BlockSpec((1, D_MODEL), lambda i, sl: (0, 0)),
                  pl.BlockSpec(memory_space=pl.ANY)],
        out_specs=[pl.BlockSpec((tm, D_MODEL), lambda i, sl: (jnp.minimum(i, tiles_p - 1), 0)),
                   pl.BlockSpec((tm, D_MODEL), lambda i, sl: (jnp.maximum(i - tiles_p, 0), 0))],
        scratch_shapes=[pltpu.VMEM((2 * tm, D_MODEL), F32), pltpu.SemaphoreType.DMA(())],
    )
    return pl.pallas_call(
        functools.partial(_combine_kernel, tm=tm, tiles_p=tiles_p),
        grid_spec=grid_spec,
        out_shape=[jax.ShapeDtypeStruct((rows_p, D_MODEL), F32),
                   jax.ShapeDtypeStruct((m - rows_p, D_MODEL), F32)],
        compiler_params=_cparams(("arbitrary",)),
        name="moe_combine",
    )(slot, x1, info, final_norm, contrib)


def _pair_plan(expert_of_pair, tile):
    n = expert_of_pair.shape[0]
    n_tiles = n // tile + N_EXPERTS + 1
    ids = jnp.arange(n, dtype=jnp.int32)
    _, order = lax.sort((expert_of_pair, ids), num_keys=1)
    experts = jnp.arange(N_EXPERTS, dtype=jnp.int32)
    onehot = expert_of_pair[:, None] == experts[None, :]
    counts = jnp.sum(onehot, axis=0).astype(jnp.int32)
    tiles_per = (counts + tile - 1) // tile
    tile_end = jnp.cumsum(tiles_per)
    tile_start = tile_end - tiles_per
    row_start = jnp.cumsum(counts) - counts
    t = jnp.arange(n_tiles, dtype=jnp.int32)
    used = t < tile_end[-1]
    exp_t = jnp.minimum(jnp.sum(t[:, None] >= tile_end[None, :], axis=1), N_EXPERTS - 1).astype(jnp.int32)
    last_expert = jnp.max(jnp.where(counts > 0, experts, 0))
    tile_expert = jnp.where(used, exp_t, last_expert).astype(jnp.int32)
    local = (t - tile_start[exp_t]) * tile
    tile_rows = jnp.where(used, jnp.clip(counts[exp_t] - local, 0, tile), 0).astype(jnp.int32)
    tile_pos = jnp.where(used, row_start[exp_t] + local, 0).astype(jnp.int32)
    _, rank = lax.sort((order, ids), num_keys=1)
    shift = tile_start * tile - row_start
    slot = rank + jnp.sum(jnp.where(onehot, shift[None, :], 0), axis=1)
    plan = (tile_expert, tile_rows, tile_pos, order)
    return plan, slot.astype(jnp.int32)


def _prep_params(mix_norm, w_in, gla_gate_w1, gla_gate_w2, gla_gate_b, gla_norm, rwkv_mu_rkv, rwkv_mu_wag,
                 rwkv_w0, rwkv_w1, rwkv_w2, rwkv_a0, rwkv_a1, rwkv_a2, rwkv_g1, rwkv_g2,
                 rwkv_k_k, rwkv_k_a, rwkv_r_k, rwkv_ln_w, rwkv_ln_b, w_out, ffn_norm,
                 router_group_w, router_group_b, router_expert_w, router_expert_b, final_norm):
    row = lambda v: v.reshape(1, -1).astype(F32)
    g_rank = 2 * LANE
    p = dict(
        mix_norm=row(mix_norm[0]),
        mu_wag=rwkv_mu_wag[0],
        w_in=w_in[0].astype(BF16),
        gw1=_pad_axis(gla_gate_w1[0], 1, LANE).astype(BF16),
        gw2=_pad_axis(gla_gate_w2[0], 0, LANE).astype(BF16),
        gb=row(gla_gate_b[0]),
        w1=_pad_axis(rwkv_w1[0], 1, LANE).astype(BF16),
        w2=_pad_axis(rwkv_w2[0], 0, LANE).astype(BF16),
        w0=row(rwkv_w0[0]),
        a1=_pad_axis(rwkv_a1[0], 1, LANE).astype(BF16),
        a2=_pad_axis(rwkv_a2[0], 0, LANE).astype(BF16),
        a0=row(rwkv_a0[0]),
        g1=_pad_axis(rwkv_g1[0], 1, g_rank).astype(BF16),
        g2=_pad_axis(rwkv_g2[0], 0, g_rank).astype(BF16),
        gla_norm=row(gla_norm[0]),
        mu_ext=jnp.concatenate([jnp.zeros((1, RWKV_OFF), F32), rwkv_mu_rkv[0].reshape(1, -1)], axis=1),
        k_k=row(rwkv_k_k[0]), k_a=row(rwkv_k_a[0]), r_k=row(rwkv_r_k[0]),
        ln_w=row(rwkv_ln_w[0]), ln_b=row(rwkv_ln_b[0]),
        wo=w_out[0].astype(BF16).reshape(2, D_GLA, D_MODEL),
        ffn_norm=row(ffn_norm[0]),
        router_w=_pad_axis(jnp.concatenate([router_group_w[0], router_expert_w[0]], axis=1), 1,
                           ROUTE_LANES).astype(BF16),
        router_b=_pad_axis(jnp.concatenate([router_group_b[0], router_expert_b[0]]).reshape(1, -1), 1,
                           ROUTE_LANES).astype(F32),
        final_norm=row(final_norm),
    )
    r = jnp.arange(MXU_TILE)
    p['gla_e'] = (r[:, None] // GLA_DK == jnp.arange(2 * GLA_DV)[None, :] // GLA_DV).astype(BF16)
    p['wkv_bd'] = (r[:, None] // RWKV_HEAD == r[None, :] // RWKV_HEAD).astype(BF16)
    return p


def _mixer(x, s_gla, s_wkv, s_shift, p, *, tm, tc):
    nb, seq, _ = x.shape
    m = nb * seq
    x2 = x.reshape(m, D_MODEL)
    multi_seq = tm > seq

    h_last = jnp.zeros((nb, D_MODEL), F32) if s_shift is None else s_shift
    zl = _plain_proj(_pad_axis(h_last, 0, -(-nb // 16) * 16).astype(BF16), p['w_in'], tn=1024)[:nb]
    if multi_seq:
        expand = lambda a: jnp.repeat(a, seq, axis=0)
        hprev, zprev = expand(h_last), expand(zl)
    else:
        hprev, zprev = h_last.reshape(nb, 1, D_MODEL), zl.reshape(nb, 1, D_IN)

    h_bf, la, lw, a_sig, gate, hlast = _norm_proj(x2, hprev, p, seq=seq, tm=tm)
    shift_new = hlast.reshape(nb, seq, D_MODEL)[:, -1] if multi_seq else hlast.reshape(nb, D_MODEL)
    z = _in_proj(h_bf, p['w_in'], p['mu_ext'], zprev, seq=seq, tm=tm if multi_seq else min(seq, 1024), tn=1024)

    z3 = z.reshape(nb, seq, D_IN)
    r3 = lambda a: a.reshape(nb, seq, a.shape[-1])
    if s_gla is None:
        s_gla = jnp.zeros((nb, GLA_HEADS, GLA_DK, GLA_DV), F32)
        s_wkv = jnp.zeros((nb, RWKV_HEADS, RWKV_HEAD, RWKV_HEAD), F32)
    og, gla_new = _gla(z3, r3(la), s_gla, p['gla_e'], p['gla_norm'], tc=tc, chunk=min(GLA_CHUNK, seq),
                       nseq=2 if nb % 2 == 0 else 1)
    wkv_seqs = 4 if nb % 4 == 0 else 1
    orr, wkv_new = _wkv(z3, r3(lw), r3(a_sig), r3(gate), s_wkv, p['wkv_bd'], p, tc=min(tc, 512 // wkv_seqs),
                        chunk=min(WKV_CHUNK, seq), nseq=wkv_seqs)
    return (x2, og.reshape(m, D_GLA), orr.reshape(m, D_RWKV)), (gla_new, wkv_new, shift_new)


def _ffn(rows_p, rows_s, p, wg, wu, wd, *, tm, moe_tile):
    mp = rows_p[0].shape[0]
    x1, h2, info = _out_router(*rows_p, *rows_s, p['wo'], p['ffn_norm'], p['router_w'], p['router_b'], tm=tm)
    expert_of_pair = info[:, :2].astype(jnp.int32).reshape(-1)
    plan, slot = _pair_plan(expert_of_pair, moe_tile)
    contrib = _experts(h2, plan, wg, wu, wd, tile=moe_tile)
    return _combine(slot, x1, info, p['final_norm'], contrib, tm=tm, rows_p=mp)


def kernel(x_prompt, x_sample, state_gla, state_wkv, state_shift, mix_norm, w_in, gla_gate_w1, gla_gate_w2, gla_gate_b, gla_norm, rwkv_mu_rkv, rwkv_mu_wag, rwkv_w0, rwkv_w1, rwkv_w2, rwkv_a0, rwkv_a1, rwkv_a2, rwkv_g1, rwkv_g2, rwkv_k_k, rwkv_k_a, rwkv_r_k, rwkv_ln_w, rwkv_ln_b, w_out, ffn_norm, router_group_w, router_group_b, router_expert_w, router_expert_b, expert_w_gate, expert_w_up, expert_w_down, final_norm):
    p = _prep_params(mix_norm, w_in, gla_gate_w1, gla_gate_w2, gla_gate_b, gla_norm, rwkv_mu_rkv, rwkv_mu_wag,
                     rwkv_w0, rwkv_w1, rwkv_w2, rwkv_a0, rwkv_a1, rwkv_a2, rwkv_g1, rwkv_g2,
                     rwkv_k_k, rwkv_k_a, rwkv_r_k, rwkv_ln_w, rwkv_ln_b, w_out, ffn_norm,
                     router_group_w, router_group_b, router_expert_w, router_expert_b, final_norm)
    wg, wu, wd = (w.reshape(w.shape[1:]) for w in (expert_w_gate, expert_w_up, expert_w_down))
    nb_p, seq_p, _ = x_prompt.shape
    nb_s, seq_s, _ = x_sample.shape
    rows_p, (gla_p, wkv_p, shift_p) = _mixer(x_prompt, None, None, None, p, tm=256, tc=min(256, seq_p))
    first = lambda a: a.reshape(a.shape[1:])
    rows_s, (gla_s, wkv_s, shift_s) = _mixer(x_sample, first(state_gla), first(state_wkv), first(state_shift), p,
                                             tm=nb_s * seq_s, tc=seq_s)
    y_p, y_s = _ffn(rows_p, rows_s, p, wg, wu, wd, tm=256, moe_tile=MOE_TILE)
    return (y_p.reshape(nb_p, seq_p, D_MODEL), y_s.reshape(nb_s, seq_s, D_MODEL),
            gla_p[None], wkv_p[None], shift_p[None], gla_s[None], wkv_s[None], shift_s[None])
```

```python
import functools

import jax
import jax.numpy as jnp
from jax import lax
from jax.experimental import pallas as pl
from jax.experimental.pallas import tpu as pltpu

F32 = jnp.float32
BF16 = jnp.bfloat16

D_MODEL = 2048
D_GLA = 1024
D_RWKV = 1024
GLA_HEADS = 4
GLA_DV = 256
GLA_DK = 128
GLA_K = 512
GLA_GATE_TEMP = 16.0
RWKV_HEAD = 64
RWKV_HEADS = 16
N_GROUPS = 8
EXPERTS_PER_GROUP = 8
D_EXPERT = 512
NORM_EPS = 1e-6
GN_EPS = 64e-5
RWKV_OFF = 2 * GLA_K + 2 * D_GLA
D_IN = RWKV_OFF + 3 * D_RWKV

LANE = 128
SUBLANE = 8
MXU_TILE = 256
VMEM_LIMIT = 56 * 1024 * 1024

GLA_CHUNK = 16
WKV_CHUNK = 8
N_EXPERTS = N_GROUPS * EXPERTS_PER_GROUP
MOE_TILE = 128
ROUTE_LANES = 128


def _cparams(sem):
    return pltpu.CompilerParams(dimension_semantics=sem, vmem_limit_bytes=VMEM_LIMIT)


def _rms(x, gain):
    return x * lax.rsqrt(jnp.mean(x * x, axis=-1, keepdims=True) + NORM_EPS) * gain


def _softplus(y):
    return jnp.maximum(y, 0.0) + jnp.log1p(jnp.exp(-jnp.abs(y)))


def _dot(a, b):
    return jnp.dot(a.astype(BF16), b.astype(BF16), preferred_element_type=F32)


SLABS = D_MODEL // LANE


def _store_slabs(ref, idx, val):
    for c in range(SLABS):
        ref[idx + (slice(None), c, slice(None))] = val[:, c * LANE:(c + 1) * LANE]


def _load_slabs(ref, idx, rows):
    return jnp.concatenate([ref[idx + (rows, c, slice(None))] for c in range(SLABS)], axis=1)


def _dot_nt(a, b):
    return lax.dot_general(a.astype(BF16), b.astype(BF16), (((1,), (1,)), ((), ())),
                           preferred_element_type=F32)


def _dot_tn(a, b):
    return lax.dot_general(a.astype(BF16), b.astype(BF16), (((0,), (0,)), ((), ())),
                           preferred_element_type=F32)


def _norm_proj_kernel(x_ref, hprev_ref, gain_ref, mu_ref, gw1_ref, gw2_ref, gb_ref,
                      w1_ref, w2_ref, w0_ref, a1_ref, a2_ref, a0_ref, g1_ref, g2_ref,
                      h_ref, la_ref, lw_ref, a_ref, gate_ref, hlast_ref, carry_ref,
                      *, tm, seq, multi_seq):
    i = pl.program_id(0)
    h = _rms(x_ref[...], gain_ref[...])
    row = lax.broadcasted_iota(jnp.int32, (tm, 1), 0)
    rolled = pltpu.roll(h, 1, 0)
    if multi_seq:
        h_shift = jnp.where(row % seq == 0, hprev_ref[...], rolled)
        hlast_ref[...] = h
    else:
        tiles_per_seq = seq // tm
        first = (i % tiles_per_seq) == 0
        prev = jnp.where(first, hprev_ref[0], carry_ref[...])
        h_shift = jnp.where(row == 0, prev, rolled)
        carry_ref[...] = h[tm - 1:tm]
        hlast_ref[0] = h[tm - 1:tm]
    h_ref[...] = h.astype(BF16)

    dx = h_shift - h
    mu = mu_ref[...]
    xw = h + dx * mu[0:1]
    xa = h + dx * mu[1:2]
    xg = h + dx * mu[2:3]

    gl = _dot(_dot(h, gw1_ref[...]), gw2_ref[...]) + gb_ref[...]
    la_ref[...] = -_softplus(-gl) * (1.0 / GLA_GATE_TEMP)

    u = w0_ref[...] + _dot(jnp.tanh(_dot(xw, w1_ref[...])), w2_ref[...])
    w_log = -_softplus(-u) - 0.5
    lw_ref[...] = -jnp.exp(w_log)

    a_ref[...] = jax.nn.sigmoid(a0_ref[...] + _dot(_dot(xa, a1_ref[...]), a2_ref[...]))
    gate_ref[...] = _dot(jax.nn.sigmoid(_dot(xg, g1_ref[...])), g2_ref[...])


def _pad_axis(w, axis, to):
    pad = [(0, 0)] * w.ndim
    pad[axis] = (0, to - w.shape[axis])
    return jnp.pad(w, pad)


def _norm_proj(x2, hprev, p, *, seq, tm):
    m = x2.shape[0]
    multi_seq = tm > seq
    nb = m // seq
    full = lambda shape: pl.BlockSpec(shape, lambda i: (0,) * len(shape))
    rows = lambda w: pl.BlockSpec((tm, w), lambda i: (i, 0))
    if multi_seq:
        assert tm == m
        hprev_spec = rows(D_MODEL)
        hlast_shape = jax.ShapeDtypeStruct((m, D_MODEL), F32)
        hlast_spec = rows(D_MODEL)
    else:
        assert seq % tm == 0
        tps = seq // tm
        hprev_spec = pl.BlockSpec((1, 1, D_MODEL), lambda i: (i // tps, 0, 0))
        hlast_shape = jax.ShapeDtypeStruct((nb, 1, D_MODEL), F32)
        hlast_spec = pl.BlockSpec((1, 1, D_MODEL), lambda i: (i // tps, 0, 0))
    weights = [p['gw1'], p['gw2'], p['gb'], p['w1'], p['w2'], p['w0'], p['a1'], p['a2'], p['a0'],
               p['g1'], p['g2']]
    return pl.pallas_call(
        functools.partial(_norm_proj_kernel, tm=tm, seq=seq, multi_seq=multi_seq),
        grid=(m // tm,),
        in_specs=[rows(D_MODEL), hprev_spec, full((1, D_MODEL)), full((3, D_MODEL))]
                 + [full(w.shape) for w in weights],
        out_specs=[rows(D_MODEL), rows(GLA_K), rows(D_RWKV), rows(D_RWKV), rows(D_RWKV), hlast_spec],
        out_shape=[jax.ShapeDtypeStruct((m, D_MODEL), BF16),
                   jax.ShapeDtypeStruct((m, GLA_K), F32),
                   jax.ShapeDtypeStruct((m, D_RWKV), F32),
                   jax.ShapeDtypeStruct((m, D_RWKV), F32),
                   jax.ShapeDtypeStruct((m, D_RWKV), F32),
                   hlast_shape],
        scratch_shapes=[pltpu.VMEM((1, D_MODEL), F32)],
        compiler_params=_cparams(("arbitrary",)),
        name="norm_proj",
    )(x2, hprev, p['mix_norm'], p['mu_wag'], *weights)


def _in_proj_kernel(h_ref, w_ref, mu_ref, zprev_ref, z_ref, carry_ref, *, tm, seq, multi_seq):
    i = pl.program_id(1)
    z = jnp.dot(h_ref[...], w_ref[...], preferred_element_type=F32)
    row = lax.broadcasted_iota(jnp.int32, (tm, 1), 0)
    rolled = pltpu.roll(z, 1, 0)
    if multi_seq:
        z_prev = jnp.where(row % seq == 0, zprev_ref[...], rolled)
    else:
        tiles_per_seq = seq // tm
        first = (i % tiles_per_seq) == 0
        prev = jnp.where(first, zprev_ref[0], carry_ref[...])
        z_prev = jnp.where(row == 0, prev, rolled)
        carry_ref[...] = z[tm - 1:tm]
    z_ref[...] = z + mu_ref[...] * (z_prev - z)


def _in_proj(h_bf, w_in, mu_ext, zprev, *, seq, tm, tn):
    m = h_bf.shape[0]
    multi_seq = tm > seq
    if multi_seq:
        zprev_spec = pl.BlockSpec((tm, tn), lambda j, i: (i, j))
    else:
        tps = seq // tm
        zprev_spec = pl.BlockSpec((1, 1, tn), lambda j, i: (i // tps, 0, j))
    return pl.pallas_call(
        functools.partial(_in_proj_kernel, tm=tm, seq=seq, multi_seq=multi_seq),
        grid=(D_IN // tn, m // tm),
        in_specs=[pl.BlockSpec((tm, D_MODEL), lambda j, i: (i, 0)),
                  pl.BlockSpec((D_MODEL, tn), lambda j, i: (0, j)),
                  pl.BlockSpec((1, tn), lambda j, i: (0, j)),
                  zprev_spec],
        out_specs=pl.BlockSpec((tm, tn), lambda j, i: (i, j)),
        out_shape=jax.ShapeDtypeStruct((m, D_IN), F32),
        scratch_shapes=[pltpu.VMEM((1, tn), F32)],
        compiler_params=_cparams(("arbitrary", "arbitrary")),
        name="in_proj",
    )(h_bf, w_in, mu_ext, zprev)


def _plain_proj_kernel(h_ref, w_ref, z_ref):
    z_ref[...] = jnp.dot(h_ref[...], w_ref[...], preferred_element_type=F32)


def _plain_proj(h_bf, w_in, *, tn):
    m = h_bf.shape[0]
    return pl.pallas_call(
        _plain_proj_kernel,
        grid=(D_IN // tn,),
        in_specs=[pl.BlockSpec((m, D_MODEL), lambda j: (0, 0)),
                  pl.BlockSpec((D_MODEL, tn), lambda j: (0, j))],
        out_specs=pl.BlockSpec((m, tn), lambda j: (0, j)),
        out_shape=jax.ShapeDtypeStruct((m, D_IN), F32),
        compiler_params=_cparams(("arbitrary",)),
        name="prev_row_proj",
    )(h_bf, w_in)


def _gla_kernel(q_ref, k_ref, v_ref, g_ref, la_ref, s0_ref, e_ref, gn_ref,
                o_ref, s_ref, st_ref, obuf_ref, *, tc, chunk, nseq):
    tb = pl.program_id(1)
    c = chunk

    @pl.when(tb == 0)
    def _():
        for q in range(nseq):
            for hh in range(GLA_HEADS):
                st_ref[q, :, hh * GLA_DK:(hh + 1) * GLA_DK] = s0_ref[q, hh].T

    row = lax.broadcasted_iota(jnp.int32, (c, 1), 0)
    e_mat = e_ref[...]
    neg = jnp.float32(-1e30)

    def one_chunk(q, r0):
        qq = q_ref[q, pl.ds(r0, c), :] * (GLA_DK ** -0.5)
        k = k_ref[q, pl.ds(r0, c), :]
        v = v_ref[q, pl.ds(r0, c), :]
        la = la_ref[q, pl.ds(r0, c), :]
        b = jnp.zeros_like(la)
        for s in range(c):
            b = b + jnp.where(row >= s, la[s:s + 1], 0.0)
        b_last = b[c - 1:c]

        parts = []
        for s in range(c):
            dec = jnp.exp(jnp.where(row >= s, b - b[s:s + 1], neg))
            parts.append((qq * k[s:s + 1] * dec).astype(BF16))
        pcat = jnp.concatenate(parts, axis=0)
        sc = jnp.concatenate(
            [jnp.dot(pcat[:, hp * MXU_TILE:(hp + 1) * MXU_TILE], e_mat, preferred_element_type=F32)
             for hp in range(GLA_K // MXU_TILE)], axis=1)
        o = jnp.zeros((c, D_GLA), F32)
        for s in range(c):
            o = o + sc[s * c:(s + 1) * c] * v[s:s + 1]

        qd = qq * jnp.exp(b)
        kd = k * jnp.exp(b_last - b)
        a_last = jnp.exp(b_last)
        o_heads = []
        for hh in range(GLA_HEADS):
            dk = slice(hh * GLA_DK, (hh + 1) * GLA_DK)
            dv = slice(hh * GLA_DV, (hh + 1) * GLA_DV)
            st = st_ref[q, :, dk]
            oh = o[:, dv] + _dot_nt(qd[:, dk], st)
            st_ref[q, :, dk] = a_last[:, dk] * st + _dot_tn(v[:, dv], kd[:, dk])
            oh = oh * lax.rsqrt(jnp.mean(oh * oh, axis=-1, keepdims=True) + NORM_EPS)
            o_heads.append(oh)
        og = jnp.concatenate(o_heads, axis=1)
        g = g_ref[q, pl.ds(r0, c), :]
        obuf_ref[q, pl.ds(r0, c), :] = og * gn_ref[...] * (g * jax.nn.sigmoid(g))

    def body(ci, carry):
        r0 = pl.multiple_of(ci * c, c)
        for q in range(nseq):
            one_chunk(q, r0)
        return carry

    lax.fori_loop(0, tc // c, body, 0)
    o_ref[...] = obuf_ref[...].astype(BF16)

    @pl.when(tb == pl.num_programs(1) - 1)
    def _():
        for q in range(nseq):
            for hh in range(GLA_HEADS):
                s_ref[q, hh] = st_ref[q, :, hh * GLA_DK:(hh + 1) * GLA_DK].T


def _gla(z3, la3, s0, e_mat, gla_norm, *, tc, chunk, nseq):
    nb, seq, _ = z3.shape
    blk = lambda w, col: pl.BlockSpec((nseq, tc, w), lambda b, t: (b, t, col))
    st_spec = pl.BlockSpec((nseq, GLA_HEADS, GLA_DK, GLA_DV), lambda b, t: (b, 0, 0, 0))
    return pl.pallas_call(
        functools.partial(_gla_kernel, tc=tc, chunk=chunk, nseq=nseq),
        grid=(nb // nseq, seq // tc),
        in_specs=[blk(GLA_K, 0), blk(GLA_K, 1), blk(D_GLA, 1), blk(D_GLA, 2), blk(GLA_K, 0),
                  st_spec,
                  pl.BlockSpec(e_mat.shape, lambda b, t: (0, 0)),
                  pl.BlockSpec((1, D_GLA), lambda b, t: (0, 0))],
        out_specs=[blk(D_GLA, 0), st_spec],
        out_shape=[jax.ShapeDtypeStruct((nb, seq, D_GLA), BF16),
                   jax.ShapeDtypeStruct((nb, GLA_HEADS, GLA_DK, GLA_DV), F32)],
        scratch_shapes=[pltpu.VMEM((nseq, GLA_DV, GLA_K), F32), pltpu.VMEM((nseq, tc, D_GLA), F32)],
        compiler_params=_cparams(("arbitrary", "arbitrary")),
        name="gla",
    )(z3, z3, z3, z3, la3, s0, e_mat, gla_norm)


N_LANE_TILES = D_RWKV // MXU_TILE
HEADS_PER_TILE = MXU_TILE // RWKV_HEAD


def _stack_tiles(x):
    return jnp.concatenate([x[:, t * MXU_TILE:(t + 1) * MXU_TILE] for t in range(N_LANE_TILES)], axis=0)


def _unstack_tiles(y, r):
    return jnp.concatenate([y[t * r:(t + 1) * r] for t in range(N_LANE_TILES)], axis=1)


def _seg_sums(xs, bd):
    c = xs[0].shape[0]
    st = jnp.concatenate([_stack_tiles(x) for x in xs], axis=0)
    hi = st.astype(BF16)
    lo = (st - hi.astype(F32)).astype(BF16)
    res = jnp.dot(jnp.concatenate([hi, lo], axis=0), bd, preferred_element_type=F32)
    half = st.shape[0]
    res = res[:half] + res[half:]
    rows = N_LANE_TILES * c
    return [_unstack_tiles(res[n * rows:(n + 1) * rows], c) for n in range(len(xs))]


def _cumsum_rows(x, row, c):
    if c == SUBLANE:
        step = 1
        while step < c:
            x = x + jnp.where(row >= step, pltpu.roll(x, step, 0), 0.0)
            step *= 2
        return x
    out = jnp.zeros_like(x)
    for s in range(c):
        out = out + jnp.where(row >= s, x[s:s + 1], 0.0)
    return out


def _wkv_kernel(r_ref, k_ref, v_ref, lw_ref, a_ref, gate_ref, s0_ref, bd_ref,
                kk_ref, ka_ref, rk_ref, lnw_ref, lnb_ref,
                o_ref, s_ref, st_ref, obuf_ref, *, tc, chunk, nseq):
    tb = pl.program_id(1)
    c = chunk
    cc = c * c

    @pl.when(tb == 0)
    def _():
        for q in range(nseq):
            st_ref[q] = jnp.concatenate([s0_ref[q, hh] for hh in range(RWKV_HEADS)], axis=1)

    row = lax.broadcasted_iota(jnp.int32, (c, 1), 0)
    bd = bd_ref[...]
    lane_head = lax.broadcasted_iota(jnp.int32, (RWKV_HEAD, MXU_TILE), 1) // RWKV_HEAD
    neg = jnp.float32(-1e30)

    def one_chunk(q, r0):
        r = r_ref[q, pl.ds(r0, c), :]
        k = k_ref[q, pl.ds(r0, c), :]
        v = v_ref[q, pl.ds(r0, c), :]
        lw = lw_ref[q, pl.ds(r0, c), :]
        a_sig = a_ref[q, pl.ds(r0, c), :]

        kk = k * kk_ref[...]
        k_eff = k * (1.0 + (a_sig - 1.0) * ka_ref[...])
        kk_ss, bonus_dot = _seg_sums([kk * kk, r * k_eff * rk_ref[...]], bd)
        kk = kk / jnp.maximum(jnp.sqrt(kk_ss), 1e-12)
        a_vec = -kk
        b_vec = kk * a_sig

        lb = _cumsum_rows(lw, row, c)
        lbp = lb - lw
        lb_last = lb[c - 1:c]
        kinds = ([], [], [], [])
        for s in range(c):
            lbs = lb[s:s + 1]
            bs = b_vec[s:s + 1]
            ks = k_eff[s:s + 1]
            a_d = a_vec * jnp.exp(jnp.where(row > s, lbp - lbs, neg))
            r_d = r * jnp.exp(jnp.where(row >= s, lb - lbs, neg))
            kinds[0].append(a_d * bs)
            kinds[1].append(a_d * ks)
            kinds[2].append(r_d * bs)
            kinds[3].append(r_d * ks)
        prod = jnp.concatenate(
            [p[:, t * MXU_TILE:(t + 1) * MXU_TILE]
             for t in range(N_LANE_TILES) for kind in kinds for p in kind], axis=0)
        coef_t = jnp.dot(prod.astype(BF16), bd, preferred_element_type=F32)

        def coef(kind, s):
            return jnp.concatenate(
                [coef_t[(t * 4 + kind) * cc + s * c:(t * 4 + kind) * cc + (s + 1) * c]
                 for t in range(N_LANE_TILES)], axis=1)

        lhs = jnp.concatenate([a_vec * jnp.exp(lbp), r * jnp.exp(lb)], axis=0).astype(BF16)
        x0 = []
        for t in range(N_LANE_TILES):
            ln = slice(t * MXU_TILE, (t + 1) * MXU_TILE)
            st_t = st_ref[q, :, ln].astype(BF16)
            w_t = jnp.concatenate([st_t] * HEADS_PER_TILE, axis=0) * bd
            x0.append(lax.dot_general(lhs[:, ln], w_t, (((1,), (1,)), ((), ())),
                                      preferred_element_type=F32))
        x0 = jnp.concatenate(x0, axis=1)
        u = x0[:c]
        o = x0[c:]

        for s in range(c):
            u = u + coef(1, s) * v[s:s + 1]
        for s in range(c):
            u = u + coef(0, s) * u[s:s + 1]
        for s in range(c):
            o = o + coef(2, s) * u[s:s + 1] + coef(3, s) * v[s:s + 1]

        tail = jnp.exp(lb_last - lb)
        uv = jnp.concatenate([u, v], axis=0).astype(BF16)
        bk = jnp.concatenate([b_vec * tail, k_eff * tail], axis=0).astype(BF16)
        w_last = jnp.exp(lb_last)
        for t in range(N_LANE_TILES):
            ln = slice(t * MXU_TILE, (t + 1) * MXU_TILE)
            zt = _dot_tn(uv[:, ln], bk[:, ln])
            upd = zt[(HEADS_PER_TILE - 1) * RWKV_HEAD:]
            for hh in range(HEADS_PER_TILE - 2, -1, -1):
                upd = jnp.where(lane_head == hh, zt[hh * RWKV_HEAD:(hh + 1) * RWKV_HEAD], upd)
            st_ref[q, :, ln] = w_last[:, ln] * st_ref[q, :, ln] + upd

        inv_n = 1.0 / RWKV_HEAD
        mu = _seg_sums([o], bd)[0] * inv_n
        dev = o - mu
        var = _seg_sums([dev * dev], bd)[0] * inv_n
        yn = dev * lax.rsqrt(var + GN_EPS) * lnw_ref[...] + lnb_ref[...]
        obuf_ref[q, pl.ds(r0, c), :] = (yn + bonus_dot * v) * gate_ref[q, pl.ds(r0, c), :]

    def body(ci, carry):
        r0 = pl.multiple_of(ci * c, c)
        for q in range(nseq):
            one_chunk(q, r0)
        return carry

    lax.fori_loop(0, tc // c, body, 0)
    o_ref[...] = obuf_ref[...].astype(BF16)

    @pl.when(tb == pl.num_programs(1) - 1)
    def _():
        for q in range(nseq):
            for hh in range(RWKV_HEADS):
                s_ref[q, hh] = st_ref[q, :, hh * RWKV_HEAD:(hh + 1) * RWKV_HEAD]


def _wkv(z3, lw3, a3, gate3, s0, bd, p, *, tc, chunk, nseq):
    nb, seq, _ = z3.shape
    col0 = RWKV_OFF // D_RWKV
    blk = lambda col: pl.BlockSpec((nseq, tc, D_RWKV), lambda b, t: (b, t, col))
    st_spec = pl.BlockSpec((nseq, RWKV_HEADS, RWKV_HEAD, RWKV_HEAD), lambda b, t: (b, 0, 0, 0))
    vec = pl.BlockSpec((1, D_RWKV), lambda b, t: (0, 0))
    return pl.pallas_call(
        functools.partial(_wkv_kernel, tc=tc, chunk=chunk, nseq=nseq),
        grid=(nb // nseq, seq // tc),
        in_specs=[blk(col0), blk(col0 + 1), blk(col0 + 2), blk(0), blk(0), blk(0), st_spec,
                  pl.BlockSpec(bd.shape, lambda b, t: (0, 0)), vec, vec, vec, vec, vec],
        out_specs=[blk(0), st_spec],
        out_shape=[jax.ShapeDtypeStruct((nb, seq, D_RWKV), BF16),
                   jax.ShapeDtypeStruct((nb, RWKV_HEADS, RWKV_HEAD, RWKV_HEAD), F32)],
        scratch_shapes=[pltpu.VMEM((nseq, RWKV_HEAD, D_RWKV), F32),
                        pltpu.VMEM((nseq, tc, D_RWKV), F32)],
        compiler_params=_cparams(("arbitrary", "arbitrary")),
        name="wkv",
    )(z3, z3, z3, lw3, a3, gate3, s0, bd, p['k_k'], p['k_a'], p['r_k'], p['ln_w'], p['ln_b'])


def _out_router_kernel(xp_ref, gp_ref, rp_ref, xs_ref, gs_ref, rs_ref, wo_ref, fg_ref, rw_ref, rb_ref,
                       x1_ref, h2_ref, info_ref, *, tm, tiles_p):
    is_p = pl.program_id(0) < tiles_p
    x = jnp.where(is_p, xp_ref[...], xs_ref[...])
    og = jnp.where(is_p, gp_ref[...], gs_ref[...])
    orr = jnp.where(is_p, rp_ref[...], rs_ref[...])
    mix = (jnp.dot(og, wo_ref[0], preferred_element_type=F32)
           + jnp.dot(orr, wo_ref[1], preferred_element_type=F32))
    x1 = x + mix
    x1_ref[...] = x1
    h2 = _rms(x1, fg_ref[...])
    _store_slabs(h2_ref, (), h2)
    logits = _dot(h2, rw_ref[...]) + rb_ref[...]
    lane = lax.broadcasted_iota(jnp.int32, (tm, ROUTE_LANES), 1)
    neg = jnp.float32(-jnp.inf)
    big = jnp.int32(1 << 20)

    def first_argmax(vals):
        mx = jnp.max(vals, axis=-1, keepdims=True)
        idx = jnp.min(jnp.where(vals == mx, lane, big), axis=-1, keepdims=True)
        return mx, idx

    lg = jnp.where(lane < N_GROUPS, logits, neg)
    g_max, g_idx = first_argmax(lg)
    p_group = 1.0 / jnp.sum(jnp.exp(lg - g_max), axis=-1, keepdims=True)
    lo = N_GROUPS + g_idx * EXPERTS_PER_GROUP
    le = jnp.where((lane >= lo) & (lane < lo + EXPERTS_PER_GROUP), logits, neg)
    m1, i1 = first_argmax(le)
    m2, i2 = first_argmax(jnp.where(lane == i1, neg, le))
    e2 = jnp.exp(m2 - m1)
    w1 = p_group / (1.0 + e2)
    w2 = p_group * e2 / (1.0 + e2)
    info_ref[...] = jnp.where(lane == 0, (i1 - N_GROUPS).astype(F32),
                              jnp.where(lane == 1, (i2 - N_GROUPS).astype(F32),
                                        jnp.where(lane == 2, w1, jnp.where(lane == 3, w2, 0.0))))


def _out_router(xp, gp, rp, xs, gs, rs, wo, ffn_norm, rw, rb, *, tm):
    mp, ms = xp.shape[0], xs.shape[0]
    tiles_p, tiles_s = mp // tm, ms // tm
    rows_p = lambda w: pl.BlockSpec((tm, w), lambda i: (jnp.minimum(i, tiles_p - 1), 0))
    rows_s = lambda w: pl.BlockSpec((tm, w), lambda i: (jnp.maximum(i - tiles_p, 0), 0))
    rows = lambda w: pl.BlockSpec((tm, w), lambda i: (i, 0))
    full = lambda shape: pl.BlockSpec(shape, lambda i: (0,) * len(shape))
    m = mp + ms
    return pl.pallas_call(
        functools.partial(_out_router_kernel, tm=tm, tiles_p=tiles_p),
        grid=(tiles_p + tiles_s,),
        in_specs=[rows_p(D_MODEL), rows_p(D_GLA), rows_p(D_RWKV),
                  rows_s(D_MODEL), rows_s(D_GLA), rows_s(D_RWKV),
                  full(wo.shape), full((1, D_MODEL)), full(rw.shape), full((1, ROUTE_LANES))],
        out_specs=[rows(D_MODEL), pl.BlockSpec((tm, SLABS, LANE), lambda i: (i, 0, 0)), rows(ROUTE_LANES)],
        out_shape=[jax.ShapeDtypeStruct((m, D_MODEL), F32),
                   jax.ShapeDtypeStruct((m, SLABS, LANE), F32),
                   jax.ShapeDtypeStruct((m, ROUTE_LANES), F32)],
        compiler_params=_cparams(("arbitrary",)),
        name="out_router",
    )(xp, gp, rp, xs, gs, rs, wo, ffn_norm, rw, rb)


def _expert_kernel(tile_expert_ref, tile_rows_ref, tile_pos_ref, order_ref,
                   h2_hbm, wg_ref, wu_ref, wd_ref, out_ref,
                   hb, wgb, wub, wdb, gsem, *, tile, n_pairs):
    j = pl.program_id(0)
    nrows = tile_rows_ref[j]
    buf = j % 2

    def gather_copy(t, b, rr):
        pair = order_ref[jnp.minimum(tile_pos_ref[t] + rr, n_pairs - 1)]
        return pltpu.make_async_copy(h2_hbm.at[pl.ds(pair >> 1, 1)], hb.at[b, pl.ds(rr, 1)], gsem.at[b])

    @pl.when(j == 0)
    def _():
        for rr in range(tile):
            gather_copy(0, 0, rr).start()

    @pl.when((j == 0) | (tile_rows_ref[jnp.maximum(j - 1, 0)] > 0))
    def _():
        pltpu.make_async_copy(h2_hbm.at[pl.ds(0, tile)], hb.at[buf], gsem.at[buf]).wait()

    @pl.when(nrows > 0)
    def _():
        @pl.when((j == 0) | (tile_expert_ref[j] != tile_expert_ref[jnp.maximum(j - 1, 0)]))
        def _():
            wgb[...] = wg_ref[0, 0].astype(BF16)
            wub[...] = wu_ref[0, 0].astype(BF16)
            wdb[...] = wd_ref[0, 0].astype(BF16)

        for rr in range(tile):
            gather_copy(j + 1, 1 - buf, rr).start()
        h2 = _load_slabs(hb, (buf,), slice(None)).astype(BF16)
        gate = jnp.dot(h2, wgb[...], preferred_element_type=F32)
        up = jnp.dot(h2, wub[...], preferred_element_type=F32)
        hid = (gate * jax.nn.sigmoid(gate)) * up
        _store_slabs(out_ref, (), jnp.dot(hid.astype(BF16), wdb[...], preferred_element_type=F32))

    @pl.when(nrows == 0)
    def _():
        out_ref[...] = jnp.zeros((tile, SLABS, LANE), F32)


def _experts(h2, plan, wg, wu, wd, *, tile):
    tile_expert, tile_rows, tile_pos, order = plan
    n_tiles = tile_expert.shape[0]

    def w_idx(j, te, tr, tp, od):
        return (te[j] // EXPERTS_PER_GROUP, te[j] % EXPERTS_PER_GROUP, 0, 0)

    grid_spec = pltpu.PrefetchScalarGridSpec(
        num_scalar_prefetch=4,
        grid=(n_tiles,),
        in_specs=[pl.BlockSpec(memory_space=pl.ANY),
                  pl.BlockSpec((1, 1, D_MODEL, D_EXPERT), w_idx),
                  pl.BlockSpec((1, 1, D_MODEL, D_EXPERT), w_idx),
                  pl.BlockSpec((1, 1, D_EXPERT, D_MODEL), w_idx)],
        out_specs=pl.BlockSpec((tile, SLABS, LANE), lambda j, te, tr, tp, od: (j, 0, 0)),
        scratch_shapes=[pltpu.VMEM((2, tile, SLABS, LANE), F32),
                        pltpu.VMEM((D_MODEL, D_EXPERT), BF16),
                        pltpu.VMEM((D_MODEL, D_EXPERT), BF16),
                        pltpu.VMEM((D_EXPERT, D_MODEL), BF16),
                        pltpu.SemaphoreType.DMA((2,))],
    )
    return pl.pallas_call(
        functools.partial(_expert_kernel, tile=tile, n_pairs=order.shape[0]),
        grid_spec=grid_spec,
        out_shape=jax.ShapeDtypeStruct((n_tiles * tile, SLABS, LANE), F32),
        compiler_params=_cparams(("arbitrary",)),
        name="moe_experts",
    )(tile_expert, tile_rows, tile_pos, order, h2, wg, wu, wd)


def _combine_kernel(slot_ref, x1_ref, info_ref, fn_ref, contrib_hbm, yp_ref, ys_ref, cbuf, gsem, *, tm, tiles_p):
    i = pl.program_id(0)

    def gather_copy(k, rr):
        pair = 2 * (i * tm + rr) + k
        return pltpu.make_async_copy(contrib_hbm.at[pl.ds(slot_ref[pair], 1)], cbuf.at[pl.ds(k * tm + rr, 1)], gsem)

    for k in range(2):
        for rr in range(tm):
            gather_copy(k, rr).start()
    pltpu.make_async_copy(contrib_hbm.at[pl.ds(0, 2 * tm)], cbuf, gsem).wait()
    info = info_ref[...]
    x2 = (x1_ref[...] + info[:, 2:3] * _load_slabs(cbuf, (), slice(0, tm))
          + info[:, 3:4] * _load_slabs(cbuf, (), slice(tm, 2 * tm)))
    y = _rms(x2, fn_ref[...])

    @pl.when(i < tiles_p)
    def _():
        yp_ref[...] = y

    @pl.when(i >= tiles_p)
    def _():
        ys_ref[...] = y


def _combine(slot, x1, info, final_norm, contrib, *, tm, rows_p):
    m = x1.shape[0]
    tiles_p = rows_p // tm
    n_tiles = m // tm
    grid_spec = pltpu.PrefetchScalarGridSpec(
        num_scalar_prefetch=1,
        grid=(n_tiles,),
        in_specs=[pl.BlockSpec((tm, D_MODEL), lambda i, sl: (i, 0)),
                  pl.BlockSpec((tm, ROUTE_LANES), lambda i, sl: (i, 0)),
                  pl.BlockSpec((1, D_MODEL), lambda i, sl: (0, 0)),
                  pl.BlockSpec(memory_space=pl.ANY)],
        out_specs=[pl.BlockSpec((tm, D_MODEL), lambda i, sl: (jnp.minimum(i, tiles_p - 1), 0)),
                   pl.BlockSpec((tm, D_MODEL), lambda i, sl: (jnp.maximum(i - tiles_p, 0), 0))],
        scratch_shapes=[pltpu.VMEM((2 * tm, SLABS, LANE), F32), pltpu.SemaphoreType.DMA(())],
    )
    return pl.pallas_call(
        functools.partial(_combine_kernel, tm=tm, tiles_p=tiles_p),
        grid_spec=grid_spec,
        out_shape=[jax.ShapeDtypeStruct((rows_p, D_MODEL), F32),
                   jax.ShapeDtypeStruct((m - rows_p, D_MODEL), F32)],
        compiler_params=_cparams(("arbitrary",)),
        name="moe_combine",
    )(slot, x1, info, final_norm, contrib)


def _pair_plan(expert_of_pair, tile):
    n = expert_of_pair.shape[0]
    n_tiles = n // tile + N_EXPERTS + 1
    ids = jnp.arange(n, dtype=jnp.int32)
    _, order = lax.sort((expert_of_pair, ids), num_keys=1)
    experts = jnp.arange(N_EXPERTS, dtype=jnp.int32)
    onehot = expert_of_pair[:, None] == experts[None, :]
    counts = jnp.sum(onehot, axis=0).astype(jnp.int32)
    tiles_per = (counts + tile - 1) // tile
    tile_end = jnp.cumsum(tiles_per)
    tile_start = tile_end - tiles_per
    row_start = jnp.cumsum(counts) - counts
    t = jnp.arange(n_tiles, dtype=jnp.int32)
    used = t < tile_end[-1]
    exp_t = jnp.minimum(jnp.sum(t[:, None] >= tile_end[None, :], axis=1), N_EXPERTS - 1).astype(jnp.int32)
    last_expert = jnp.max(jnp.where(counts > 0, experts, 0))
    tile_expert = jnp.where(used, exp_t, last_expert).astype(jnp.int32)
    local = (t - tile_start[exp_t]) * tile
    tile_rows = jnp.where(used, jnp.clip(counts[exp_t] - local, 0, tile), 0).astype(jnp.int32)
    tile_pos = jnp.where(used, row_start[exp_t] + local, 0).astype(jnp.int32)
    _, rank = lax.sort((order, ids), num_keys=1)
    shift = tile_start * tile - row_start
    slot = rank + jnp.sum(jnp.where(onehot, shift[None, :], 0), axis=1)
    plan = (tile_expert, tile_rows, tile_pos, order)
    return plan, slot.astype(jnp.int32)


def _prep_params(mix_norm, w_in, gla_gate_w1, gla_gate_w2, gla_gate_b, gla_norm, rwkv_mu_rkv, rwkv_mu_wag,
                 rwkv_w0, rwkv_w1, rwkv_w2, rwkv_a0, rwkv_a1, rwkv_a2, rwkv_g1, rwkv_g2,
                 rwkv_k_k, rwkv_k_a, rwkv_r_k, rwkv_ln_w, rwkv_ln_b, w_out, ffn_norm,
                 router_group_w, router_group_b, router_expert_w, router_expert_b, final_norm):
    row = lambda v: v.reshape(1, -1).astype(F32)
    g_rank = 2 * LANE
    p = dict(
        mix_norm=row(mix_norm[0]),
        mu_wag=rwkv_mu_wag[0],
        w_in=w_in[0].astype(BF16),
        gw1=_pad_axis(gla_gate_w1[0], 1, LANE).astype(BF16),
        gw2=_pad_axis(gla_gate_w2[0], 0, LANE).astype(BF16),
        gb=row(gla_gate_b[0]),
        w1=_pad_axis(rwkv_w1[0], 1, LANE).astype(BF16),
        w2=_pad_axis(rwkv_w2[0], 0, LANE).astype(BF16),
        w0=row(rwkv_w0[0]),
        a1=_pad_axis(rwkv_a1[0], 1, LANE).astype(BF16),
        a2=_pad_axis(rwkv_a2[0], 0, LANE).astype(BF16),
        a0=row(rwkv_a0[0]),
        g1=_pad_axis(rwkv_g1[0], 1, g_rank).astype(BF16),
        g2=_pad_axis(rwkv_g2[0], 0, g_rank).astype(BF16),
        gla_norm=row(gla_norm[0]),
        mu_ext=jnp.concatenate([jnp.zeros((1, RWKV_OFF), F32), rwkv_mu_rkv[0].reshape(1, -1)], axis=1),
        k_k=row(rwkv_k_k[0]), k_a=row(rwkv_k_a[0]), r_k=row(rwkv_r_k[0]),
        ln_w=row(rwkv_ln_w[0]), ln_b=row(rwkv_ln_b[0]),
        wo=w_out[0].astype(BF16).reshape(2, D_GLA, D_MODEL),
        ffn_norm=row(ffn_norm[0]),
        router_w=_pad_axis(jnp.concatenate([router_group_w[0], router_expert_w[0]], axis=1), 1,
                           ROUTE_LANES).astype(BF16),
        router_b=_pad_axis(jnp.concatenate([router_group_b[0], router_expert_b[0]]).reshape(1, -1), 1,
                           ROUTE_LANES).astype(F32),
        final_norm=row(final_norm),
    )
    r = jnp.arange(MXU_TILE)
    p['gla_e'] = (r[:, None] // GLA_DK == jnp.arange(2 * GLA_DV)[None, :] // GLA_DV).astype(BF16)
    p['wkv_bd'] = (r[:, None] // RWKV_HEAD == r[None, :] // RWKV_HEAD).astype(BF16)
    return p


def _mixer(x, s_gla, s_wkv, s_shift, p, *, tm, tc):
    nb, seq, _ = x.shape
    m = nb * seq
    x2 = x.reshape(m, D_MODEL)
    multi_seq = tm > seq

    h_last = jnp.zeros((nb, D_MODEL), F32) if s_shift is None else s_shift
    zl = _plain_proj(_pad_axis(h_last, 0, -(-nb // 16) * 16).astype(BF16), p['w_in'], tn=1024)[:nb]
    if multi_seq:
        expand = lambda a: jnp.repeat(a, seq, axis=0)
        hprev, zprev = expand(h_last), expand(zl)
    else:
        hprev, zprev = h_last.reshape(nb, 1, D_MODEL), zl.reshape(nb, 1, D_IN)

    h_bf, la, lw, a_sig, gate, hlast = _norm_proj(x2, hprev, p, seq=seq, tm=tm)
    shift_new = hlast.reshape(nb, seq, D_MODEL)[:, -1] if multi_seq else hlast.reshape(nb, D_MODEL)
    z = _in_proj(h_bf, p['w_in'], p['mu_ext'], zprev, seq=seq, tm=tm if multi_seq else min(seq, 1024), tn=1024)

    z3 = z.reshape(nb, seq, D_IN)
    r3 = lambda a: a.reshape(nb, seq, a.shape[-1])
    if s_gla is None:
        s_gla = jnp.zeros((nb, GLA_HEADS, GLA_DK, GLA_DV), F32)
        s_wkv = jnp.zeros((nb, RWKV_HEADS, RWKV_HEAD, RWKV_HEAD), F32)
    og, gla_new = _gla(z3, r3(la), s_gla, p['gla_e'], p['gla_norm'], tc=tc, chunk=min(GLA_CHUNK, seq),
                       nseq=2 if nb % 2 == 0 else 1)
    wkv_seqs = 4 if nb % 4 == 0 else 1
    orr, wkv_new = _wkv(z3, r3(lw), r3(a_sig), r3(gate), s_wkv, p['wkv_bd'], p, tc=min(tc, 512 // wkv_seqs),
                        chunk=min(WKV_CHUNK, seq), nseq=wkv_seqs)
    return (x2, og.reshape(m, D_GLA), orr.reshape(m, D_RWKV)), (gla_new, wkv_new, shift_new)


def _ffn(rows_p, rows_s, p, wg, wu, wd, *, tm, moe_tile):
    mp = rows_p[0].shape[0]
    x1, h2, info = _out_router(*rows_p, *rows_s, p['wo'], p['ffn_norm'], p['router_w'], p['router_b'], tm=tm)
    expert_of_pair = info[:, :2].astype(jnp.int32).reshape(-1)
    plan, slot = _pair_plan(expert_of_pair, moe_tile)
    contrib = _experts(h2, plan, wg, wu, wd, tile=moe_tile)
    return _combine(slot, x1, info, p['final_norm'], contrib, tm=tm, rows_p=mp)


def kernel(x_prompt, x_sample, state_gla, state_wkv, state_shift, mix_norm, w_in, gla_gate_w1, gla_gate_w2, gla_gate_b, gla_norm, rwkv_mu_rkv, rwkv_mu_wag, rwkv_w0, rwkv_w1, rwkv_w2, rwkv_a0, rwkv_a1, rwkv_a2, rwkv_g1, rwkv_g2, rwkv_k_k, rwkv_k_a, rwkv_r_k, rwkv_ln_w, rwkv_ln_b, w_out, ffn_norm, router_group_w, router_group_b, router_expert_w, router_expert_b, expert_w_gate, expert_w_up, expert_w_down, final_norm):
    p = _prep_params(mix_norm, w_in, gla_gate_w1, gla_gate_w2, gla_gate_b, gla_norm, rwkv_mu_rkv, rwkv_mu_wag,
                     rwkv_w0, rwkv_w1, rwkv_w2, rwkv_a0, rwkv_a1, rwkv_a2, rwkv_g1, rwkv_g2,
                     rwkv_k_k, rwkv_k_a, rwkv_r_k, rwkv_ln_w, rwkv_ln_b, w_out, ffn_norm,
                     router_group_w, router_group_b, router_expert_w, router_expert_b, final_norm)
    wg, wu, wd = (w.reshape(w.shape[1:]) for w in (expert_w_gate, expert_w_up, expert_w_down))
    nb_p, seq_p, _ = x_prompt.shape
    nb_s, seq_s, _ = x_sample.shape
    rows_p, (gla_p, wkv_p, shift_p) = _mixer(x_prompt, None, None, None, p, tm=256, tc=min(256, seq_p))
    first = lambda a: a.reshape(a.shape[1:])
    rows_s, (gla_s, wkv_s, shift_s) = _mixer(x_sample, first(state_gla), first(state_wkv), first(state_shift), p,
                                             tm=nb_s * seq_s, tc=seq_s)
    y_p, y_s = _ffn(rows_p, rows_s, p, wg, wu, wd, tm=256, moe_tile=MOE_TILE)
    return (y_p.reshape(nb_p, seq_p, D_MODEL), y_s.reshape(nb_s, seq_s, D_MODEL),
            gla_p[None], wkv_p[None], shift_p[None], gla_s[None], wkv_s[None], shift_s[None])
```

```python
import functools

import jax
import jax.numpy as jnp
from jax import lax
from jax.experimental import pallas as pl
from jax.experimental.pallas import tpu as pltpu

F32 = jnp.float32
BF16 = jnp.bfloat16

D_MODEL = 2048
D_GLA = 1024
D_RWKV = 1024
GLA_HEADS = 4
GLA_DV = 256
GLA_DK = 128
GLA_K = 512
GLA_GATE_TEMP = 16.0
RWKV_HEAD = 64
RWKV_HEADS = 16
N_GROUPS = 8
EXPERTS_PER_GROUP = 8
D_EXPERT = 512
NORM_EPS = 1e-6
GN_EPS = 64e-5
RWKV_OFF = 2 * GLA_K + 2 * D_GLA
D_IN = RWKV_OFF + 3 * D_RWKV

LANE = 128
SUBLANE = 8
MXU_TILE = 256
VMEM_LIMIT = 56 * 1024 * 1024

GLA_CHUNK = 16
WKV_CHUNK = 8
MOE_SUB = 128
MOE_SUBS_PER_TILE = 10
ROUTE_LANES = 128
X1_WIDTH = D_MODEL + ROUTE_LANES
DMA_UNROLL = 8


def _cparams(sem):
    return pltpu.CompilerParams(dimension_semantics=sem, vmem_limit_bytes=VMEM_LIMIT)


def _rms(x, gain):
    return x * lax.rsqrt(jnp.mean(x * x, axis=-1, keepdims=True) + NORM_EPS) * gain


def _softplus(y):
    return jnp.maximum(y, 0.0) + jnp.log1p(jnp.exp(-jnp.abs(y)))


def _dot(a, b):
    return jnp.dot(a.astype(BF16), b.astype(BF16), preferred_element_type=F32)


def _dot_nt(a, b):
    return lax.dot_general(a.astype(BF16), b.astype(BF16), (((1,), (1,)), ((), ())),
                           preferred_element_type=F32)


def _dot_tn(a, b):
    return lax.dot_general(a.astype(BF16), b.astype(BF16), (((0,), (0,)), ((), ())),
                           preferred_element_type=F32)


def _norm_proj_kernel(x_ref, hprev_ref, gain_ref, mu_ref, gw1_ref, gw2_ref, gb_ref,
                      w1_ref, w2_ref, w0_ref, a1_ref, a2_ref, a0_ref, g1_ref, g2_ref,
                      h_ref, la_ref, lw_ref, a_ref, gate_ref, hlast_ref, carry_ref,
                      *, tm, seq, multi_seq):
    i = pl.program_id(0)
    h = _rms(x_ref[...], gain_ref[...])
    row = lax.broadcasted_iota(jnp.int32, (tm, 1), 0)
    rolled = pltpu.roll(h, 1, 0)
    if multi_seq:
        h_shift = jnp.where(row % seq == 0, hprev_ref[...], rolled)
        hlast_ref[...] = h
    else:
        tiles_per_seq = seq // tm
        first = (i % tiles_per_seq) == 0
        prev = jnp.where(first, hprev_ref[0], carry_ref[...])
        h_shift = jnp.where(row == 0, prev, rolled)
        carry_ref[...] = h[tm - 1:tm]
        hlast_ref[0] = h[tm - 1:tm]
    h_ref[...] = h.astype(BF16)

    dx = h_shift - h
    mu = mu_ref[...]
    xw = h + dx * mu[0:1]
    xa = h + dx * mu[1:2]
    xg = h + dx * mu[2:3]

    gl = _dot(_dot(h, gw1_ref[...]), gw2_ref[...]) + gb_ref[...]
    la_ref[...] = -_softplus(-gl) * (1.0 / GLA_GATE_TEMP)

    u = w0_ref[...] + _dot(jnp.tanh(_dot(xw, w1_ref[...])), w2_ref[...])
    w_log = -_softplus(-u) - 0.5
    lw_ref[...] = -jnp.exp(w_log)

    a_ref[...] = jax.nn.sigmoid(a0_ref[...] + _dot(_dot(xa, a1_ref[...]), a2_ref[...]))
    gate_ref[...] = _dot(jax.nn.sigmoid(_dot(xg, g1_ref[...])), g2_ref[...])


def _pad_axis(w, axis, to):
    pad = [(0, 0)] * w.ndim
    pad[axis] = (0, to - w.shape[axis])
    return jnp.pad(w, pad)


def _norm_proj(x2, hprev, p, *, seq, tm):
    m = x2.shape[0]
    multi_seq = tm > seq
    nb = m // seq
    full = lambda shape: pl.BlockSpec(shape, lambda i: (0,) * len(shape))
    rows = lambda w: pl.BlockSpec((tm, w), lambda i: (i, 0))
    if multi_seq:
        assert tm == m
        hprev_spec = rows(D_MODEL)
        hlast_shape = jax.ShapeDtypeStruct((m, D_MODEL), F32)
        hlast_spec = rows(D_MODEL)
    else:
        assert seq % tm == 0
        tps = seq // tm
        hprev_spec = pl.BlockSpec((1, 1, D_MODEL), lambda i: (i // tps, 0, 0))
        hlast_shape = jax.ShapeDtypeStruct((nb, 1, D_MODEL), F32)
        hlast_spec = pl.BlockSpec((1, 1, D_MODEL), lambda i: (i // tps, 0, 0))
    weights = [p['gw1'], p['gw2'], p['gb'], p['w1'], p['w2'], p['w0'], p['a1'], p['a2'], p['a0'],
               p['g1'], p['g2']]
    return pl.pallas_call(
        functools.partial(_norm_proj_kernel, tm=tm, seq=seq, multi_seq=multi_seq),
        grid=(m // tm,),
        in_specs=[rows(D_MODEL), hprev_spec, full((1, D_MODEL)), full((3, D_MODEL))]
                 + [full(w.shape) for w in weights],
        out_specs=[rows(D_MODEL), rows(GLA_K), rows(D_RWKV), rows(D_RWKV), rows(D_RWKV), hlast_spec],
        out_shape=[jax.ShapeDtypeStruct((m, D_MODEL), BF16),
                   jax.ShapeDtypeStruct((m, GLA_K), F32),
                   jax.ShapeDtypeStruct((m, D_RWKV), F32),
                   jax.ShapeDtypeStruct((m, D_RWKV), F32),
                   jax.ShapeDtypeStruct((m, D_RWKV), F32),
                   hlast_shape],
        scratch_shapes=[pltpu.VMEM((1, D_MODEL), F32)],
        compiler_params=_cparams(("arbitrary",)),
        name="norm_proj",
    )(x2, hprev, p['mix_norm'], p['mu_wag'], *weights)


def _in_proj_kernel(h_ref, w_ref, mu_ref, zprev_ref, z_ref, carry_ref, *, tm, seq, multi_seq):
    i = pl.program_id(1)
    z = jnp.dot(h_ref[...], w_ref[...], preferred_element_type=F32)
    row = lax.broadcasted_iota(jnp.int32, (tm, 1), 0)
    rolled = pltpu.roll(z, 1, 0)
    if multi_seq:
        z_prev = jnp.where(row % seq == 0, zprev_ref[...], rolled)
    else:
        tiles_per_seq = seq // tm
        first = (i % tiles_per_seq) == 0
        prev = jnp.where(first, zprev_ref[0], carry_ref[...])
        z_prev = jnp.where(row == 0, prev, rolled)
        carry_ref[...] = z[tm - 1:tm]
    z_ref[...] = z + mu_ref[...] * (z_prev - z)


def _in_proj(h_bf, w_in, mu_ext, zprev, *, seq, tm, tn):
    m = h_bf.shape[0]
    multi_seq = tm > seq
    if multi_seq:
        zprev_spec = pl.BlockSpec((tm, tn), lambda j, i: (i, j))
    else:
        tps = seq // tm
        zprev_spec = pl.BlockSpec((1, 1, tn), lambda j, i: (i // tps, 0, j))
    return pl.pallas_call(
        functools.partial(_in_proj_kernel, tm=tm, seq=seq, multi_seq=multi_seq),
        grid=(D_IN // tn, m // tm),
        in_specs=[pl.BlockSpec((tm, D_MODEL), lambda j, i: (i, 0)),
                  pl.BlockSpec((D_MODEL, tn), lambda j, i: (0, j)),
                  pl.BlockSpec((1, tn), lambda j, i: (0, j)),
                  zprev_spec],
        out_specs=pl.BlockSpec((tm, tn), lambda j, i: (i, j)),
        out_shape=jax.ShapeDtypeStruct((m, D_IN), F32),
        scratch_shapes=[pltpu.VMEM((1, tn), F32)],
        compiler_params=_cparams(("arbitrary", "arbitrary")),
        name="in_proj",
    )(h_bf, w_in, mu_ext, zprev)


def _plain_proj_kernel(h_ref, w_ref, z_ref):
    z_ref[...] = jnp.dot(h_ref[...], w_ref[...], preferred_element_type=F32)


def _plain_proj(h_bf, w_in, *, tn):
    m = h_bf.shape[0]
    return pl.pallas_call(
        _plain_proj_kernel,
        grid=(D_IN // tn,),
        in_specs=[pl.BlockSpec((m, D_MODEL), lambda j: (0, 0)),
                  pl.BlockSpec((D_MODEL, tn), lambda j: (0, j))],
        out_specs=pl.BlockSpec((m, tn), lambda j: (0, j)),
        out_shape=jax.ShapeDtypeStruct((m, D_IN), F32),
        compiler_params=_cparams(("arbitrary",)),
        name="prev_row_proj",
    )(h_bf, w_in)


def _gla_kernel(q_ref, k_ref, v_ref, g_ref, la_ref, s0_ref, e_ref, gn_ref,
                o_ref, s_ref, st_ref, obuf_ref, *, tc, chunk, nseq):
    tb = pl.program_id(1)
    c = chunk

    @pl.when(tb == 0)
    def _():
        for q in range(nseq):
            for hh in range(GLA_HEADS):
                st_ref[q, :, hh * GLA_DK:(hh + 1) * GLA_DK] = s0_ref[q, hh].T

    row = lax.broadcasted_iota(jnp.int32, (c, 1), 0)
    e_mat = e_ref[...]
    neg = jnp.float32(-1e30)

    def one_chunk(q, r0):
        qq = q_ref[q, pl.ds(r0, c), :] * (GLA_DK ** -0.5)
        k = k_ref[q, pl.ds(r0, c), :]
        v = v_ref[q, pl.ds(r0, c), :]
        la = la_ref[q, pl.ds(r0, c), :]
        b = jnp.zeros_like(la)
        for s in range(c):
            b = b + jnp.where(row >= s, la[s:s + 1], 0.0)
        b_last = b[c - 1:c]

        parts = []
        for s in range(c):
            dec = jnp.exp(jnp.where(row >= s, b - b[s:s + 1], neg))
            parts.append((qq * k[s:s + 1] * dec).astype(BF16))
        pcat = jnp.concatenate(parts, axis=0)
        sc = jnp.concatenate(
            [jnp.dot(pcat[:, hp * MXU_TILE:(hp + 1) * MXU_TILE], e_mat, preferred_element_type=F32)
             for hp in range(GLA_K // MXU_TILE)], axis=1)
        o = jnp.zeros((c, D_GLA), F32)
        for s in range(c):
            o = o + sc[s * c:(s + 1) * c] * v[s:s + 1]

        qd = qq * jnp.exp(b)
        kd = k * jnp.exp(b_last - b)
        a_last = jnp.exp(b_last)
        o_heads = []
        for hh in range(GLA_HEADS):
            dk = slice(hh * GLA_DK, (hh + 1) * GLA_DK)
            dv = slice(hh * GLA_DV, (hh + 1) * GLA_DV)
            st = st_ref[q, :, dk]
            oh = o[:, dv] + _dot_nt(qd[:, dk], st)
            st_ref[q, :, dk] = a_last[:, dk] * st + _dot_tn(v[:, dv], kd[:, dk])
            oh = oh * lax.rsqrt(jnp.mean(oh * oh, axis=-1, keepdims=True) + NORM_EPS)
            o_heads.append(oh)
        og = jnp.concatenate(o_heads, axis=1)
        g = g_ref[q, pl.ds(r0, c), :]
        obuf_ref[q, pl.ds(r0, c), :] = og * gn_ref[...] * (g * jax.nn.sigmoid(g))

    def body(ci, carry):
        r0 = pl.multiple_of(ci * c, c)
        for q in range(nseq):
            one_chunk(q, r0)
        return carry

    lax.fori_loop(0, tc // c, body, 0)
    o_ref[...] = obuf_ref[...].astype(BF16)

    @pl.when(tb == pl.num_programs(1) - 1)
    def _():
        for q in range(nseq):
            for hh in range(GLA_HEADS):
                s_ref[q, hh] = st_ref[q, :, hh * GLA_DK:(hh + 1) * GLA_DK].T


def _gla(z3, la3, s0, e_mat, gla_norm, *, tc, chunk, nseq):
    nb, seq, _ = z3.shape
    blk = lambda w, col: pl.BlockSpec((nseq, tc, w), lambda b, t: (b, t, col))
    st_spec = pl.BlockSpec((nseq, GLA_HEADS, GLA_DK, GLA_DV), lambda b, t: (b, 0, 0, 0))
    return pl.pallas_call(
        functools.partial(_gla_kernel, tc=tc, chunk=chunk, nseq=nseq),
        grid=(nb // nseq, seq // tc),
        in_specs=[blk(GLA_K, 0), blk(GLA_K, 1), blk(D_GLA, 1), blk(D_GLA, 2), blk(GLA_K, 0),
                  st_spec,
                  pl.BlockSpec(e_mat.shape, lambda b, t: (0, 0)),
                  pl.BlockSpec((1, D_GLA), lambda b, t: (0, 0))],
        out_specs=[blk(D_GLA, 0), st_spec],
        out_shape=[jax.ShapeDtypeStruct((nb, seq, D_GLA), BF16),
                   jax.ShapeDtypeStruct((nb, GLA_HEADS, GLA_DK, GLA_DV), F32)],
        scratch_shapes=[pltpu.VMEM((nseq, GLA_DV, GLA_K), F32), pltpu.VMEM((nseq, tc, D_GLA), F32)],
        compiler_params=_cparams(("arbitrary", "arbitrary")),
        name="gla",
    )(z3, z3, z3, z3, la3, s0, e_mat, gla_norm)


N_LANE_TILES = D_RWKV // MXU_TILE
HEADS_PER_TILE = MXU_TILE // RWKV_HEAD


def _stack_tiles(x):
    return jnp.concatenate([x[:, t * MXU_TILE:(t + 1) * MXU_TILE] for t in range(N_LANE_TILES)], axis=0)


def _unstack_tiles(y, r):
    return jnp.concatenate([y[t * r:(t + 1) * r] for t in range(N_LANE_TILES)], axis=1)


def _seg_sums(xs, bd):
    c = xs[0].shape[0]
    st = jnp.concatenate([_stack_tiles(x) for x in xs], axis=0)
    hi = st.astype(BF16)
    lo = (st - hi.astype(F32)).astype(BF16)
    res = jnp.dot(jnp.concatenate([hi, lo], axis=0), bd, preferred_element_type=F32)
    half = st.shape[0]
    res = res[:half] + res[half:]
    rows = N_LANE_TILES * c
    return [_unstack_tiles(res[n * rows:(n + 1) * rows], c) for n in range(len(xs))]


def _cumsum_rows(x, row, c):
    if c == SUBLANE:
        step = 1
        while step < c:
            x = x + jnp.where(row >= step, pltpu.roll(x, step, 0), 0.0)
            step *= 2
        return x
    out = jnp.zeros_like(x)
    for s in range(c):
        out = out + jnp.where(row >= s, x[s:s + 1], 0.0)
    return out


def _wkv_kernel(r_ref, k_ref, v_ref, lw_ref, a_ref, gate_ref, s0_ref, bd_ref,
                kk_ref, ka_ref, rk_ref, lnw_ref, lnb_ref,
                o_ref, s_ref, st_ref, obuf_ref, *, tc, chunk, nseq):
    tb = pl.program_id(1)
    c = chunk
    cc = c * c

    @pl.when(tb == 0)
    def _():
        for q in range(nseq):
            st_ref[q] = jnp.concatenate([s0_ref[q, hh] for hh in range(RWKV_HEADS)], axis=1)

    row = lax.broadcasted_iota(jnp.int32, (c, 1), 0)
    bd = bd_ref[...]
    lane_head = lax.broadcasted_iota(jnp.int32, (RWKV_HEAD, MXU_TILE), 1) // RWKV_HEAD
    neg = jnp.float32(-1e30)

    def one_chunk(q, r0):
        r = r_ref[q, pl.ds(r0, c), :]
        k = k_ref[q, pl.ds(r0, c), :]
        v = v_ref[q, pl.ds(r0, c), :]
        lw = lw_ref[q, pl.ds(r0, c), :]
        a_sig = a_ref[q, pl.ds(r0, c), :]

        kk = k * kk_ref[...]
        k_eff = k * (1.0 + (a_sig - 1.0) * ka_ref[...])
        kk_ss, bonus_dot = _seg_sums([kk * kk, r * k_eff * rk_ref[...]], bd)
        kk = kk / jnp.maximum(jnp.sqrt(kk_ss), 1e-12)
        a_vec = -kk
        b_vec = kk * a_sig

        lb = _cumsum_rows(lw, row, c)
        lbp = lb - lw
        lb_last = lb[c - 1:c]
        kinds = ([], [], [], [])
        for s in range(c):
            lbs = lb[s:s + 1]
            bs = b_vec[s:s + 1]
            ks = k_eff[s:s + 1]
            a_d = a_vec * jnp.exp(jnp.where(row > s, lbp - lbs, neg))
            r_d = r * jnp.exp(jnp.where(row >= s, lb - lbs, neg))
            kinds[0].append(a_d * bs)
            kinds[1].append(a_d * ks)
            kinds[2].append(r_d * bs)
            kinds[3].append(r_d * ks)
        prod = jnp.concatenate(
            [p[:, t * MXU_TILE:(t + 1) * MXU_TILE]
             for t in range(N_LANE_TILES) for kind in kinds for p in kind], axis=0)
        coef_t = jnp.dot(prod.astype(BF16), bd, preferred_element_type=F32)

        def coef(kind, s):
            return jnp.concatenate(
                [coef_t[(t * 4 + kind) * cc + s * c:(t * 4 + kind) * cc + (s + 1) * c]
                 for t in range(N_LANE_TILES)], axis=1)

        lhs = jnp.concatenate([a_vec * jnp.exp(lbp), r * jnp.exp(lb)], axis=0).astype(BF16)
        x0 = []
        for t in range(N_LANE_TILES):
            ln = slice(t * MXU_TILE, (t + 1) * MXU_TILE)
            st_t = st_ref[q, :, ln].astype(BF16)
            w_t = jnp.concatenate([st_t] * HEADS_PER_TILE, axis=0) * bd
            x0.append(lax.dot_general(lhs[:, ln], w_t, (((1,), (1,)), ((), ())),
                                      preferred_element_type=F32))
        x0 = jnp.concatenate(x0, axis=1)
        u = x0[:c]
        o = x0[c:]

        for s in range(c):
            u = u + coef(1, s) * v[s:s + 1]
        for s in range(c):
            u = u + coef(0, s) * u[s:s + 1]
        for s in range(c):
            o = o + coef(2, s) * u[s:s + 1] + coef(3, s) * v[s:s + 1]

        tail = jnp.exp(lb_last - lb)
        uv = jnp.concatenate([u, v], axis=0).astype(BF16)
        bk = jnp.concatenate([b_vec * tail, k_eff * tail], axis=0).astype(BF16)
        w_last = jnp.exp(lb_last)
        for t in range(N_LANE_TILES):
            ln = slice(t * MXU_TILE, (t + 1) * MXU_TILE)
            zt = _dot_tn(uv[:, ln], bk[:, ln])
            upd = zt[(HEADS_PER_TILE - 1) * RWKV_HEAD:]
            for hh in range(HEADS_PER_TILE - 2, -1, -1):
                upd = jnp.where(lane_head == hh, zt[hh * RWKV_HEAD:(hh + 1) * RWKV_HEAD], upd)
            st_ref[q, :, ln] = w_last[:, ln] * st_ref[q, :, ln] + upd

        inv_n = 1.0 / RWKV_HEAD
        mu = _seg_sums([o], bd)[0] * inv_n
        dev = o - mu
        var = _seg_sums([dev * dev], bd)[0] * inv_n
        yn = dev * lax.rsqrt(var + GN_EPS) * lnw_ref[...] + lnb_ref[...]
        obuf_ref[q, pl.ds(r0, c), :] = (yn + bonus_dot * v) * gate_ref[q, pl.ds(r0, c), :]

    def body(ci, carry):
        r0 = pl.multiple_of(ci * c, c)
        for q in range(nseq):
            one_chunk(q, r0)
        return carry

    lax.fori_loop(0, tc // c, body, 0)
    o_ref[...] = obuf_ref[...].astype(BF16)

    @pl.when(tb == pl.num_programs(1) - 1)
    def _():
        for q in range(nseq):
            for hh in range(RWKV_HEADS):
                s_ref[q, hh] = st_ref[q, :, hh * RWKV_HEAD:(hh + 1) * RWKV_HEAD]


def _wkv(z3, lw3, a3, gate3, s0, bd, p, *, tc, chunk, nseq):
    nb, seq, _ = z3.shape
    col0 = RWKV_OFF // D_RWKV
    blk = lambda col: pl.BlockSpec((nseq, tc, D_RWKV), lambda b, t: (b, t, col))
    st_spec = pl.BlockSpec((nseq, RWKV_HEADS, RWKV_HEAD, RWKV_HEAD), lambda b, t: (b, 0, 0, 0))
    vec = pl.BlockSpec((1, D_RWKV), lambda b, t: (0, 0))
    return pl.pallas_call(
        functools.partial(_wkv_kernel, tc=tc, chunk=chunk, nseq=nseq),
        grid=(nb // nseq, seq // tc),
        in_specs=[blk(col0), blk(col0 + 1), blk(col0 + 2), blk(0), blk(0), blk(0), st_spec,
                  pl.BlockSpec(bd.shape, lambda b, t: (0, 0)), vec, vec, vec, vec, vec],
        out_specs=[blk(0), st_spec],
        out_shape=[jax.ShapeDtypeStruct((nb, seq, D_RWKV), BF16),
                   jax.ShapeDtypeStruct((nb, RWKV_HEADS, RWKV_HEAD, RWKV_HEAD), F32)],
        scratch_shapes=[pltpu.VMEM((nseq, RWKV_HEAD, D_RWKV), F32),
                        pltpu.VMEM((nseq, tc, D_RWKV), F32)],
        compiler_params=_cparams(("arbitrary", "arbitrary")),
        name="wkv",
    )(z3, z3, z3, lw3, a3, gate3, s0, bd, p['k_k'], p['k_a'], p['r_k'], p['ln_w'], p['ln_b'])


def _out_router_kernel(xp_ref, gp_ref, rp_ref, xs_ref, gs_ref, rs_ref, wo_ref, fg_ref, rw_ref, rb_ref,
                       out_ref, *, tm, tiles_p):
    is_p = pl.program_id(0) < tiles_p
    x = jnp.where(is_p, xp_ref[...], xs_ref[...])
    og = jnp.where(is_p, gp_ref[...], gs_ref[...])
    orr = jnp.where(is_p, rp_ref[...], rs_ref[...])
    mix = (jnp.dot(og, wo_ref[0], preferred_element_type=F32)
           + jnp.dot(orr, wo_ref[1], preferred_element_type=F32))
    x1 = x + mix
    out_ref[:, :D_MODEL] = x1
    h2 = _rms(x1, fg_ref[...])
    logits = _dot(h2, rw_ref[...]) + rb_ref[...]
    lane = lax.broadcasted_iota(jnp.int32, (tm, ROUTE_LANES), 1)
    neg = jnp.float32(-jnp.inf)
    big = jnp.int32(1 << 20)

    def first_argmax(vals):
        mx = jnp.max(vals, axis=-1, keepdims=True)
        idx = jnp.min(jnp.where(vals == mx, lane, big), axis=-1, keepdims=True)
        return mx, idx

    lg = jnp.where(lane < N_GROUPS, logits, neg)
    g_max, g_idx = first_argmax(lg)
    p_group = 1.0 / jnp.sum(jnp.exp(lg - g_max), axis=-1, keepdims=True)
    lo = N_GROUPS + g_idx * EXPERTS_PER_GROUP
    le = jnp.where((lane >= lo) & (lane < lo + EXPERTS_PER_GROUP), logits, neg)
    m1, i1 = first_argmax(le)
    m2, i2 = first_argmax(jnp.where(lane == i1, neg, le))
    e2 = jnp.exp(m2 - m1)
    w1 = 1.0 / (1.0 + e2)
    w2 = e2 / (1.0 + e2)
    route = jnp.where(lane == i1, w1, jnp.where(lane == i2, w2, 0.0)) * p_group
    out_ref[:, D_MODEL:] = jnp.where(lane == ROUTE_LANES - 1, g_idx.astype(F32), route)


def _out_router(xp, gp, rp, xs, gs, rs, wo, ffn_norm, rw, rb, *, tm):
    mp, ms = xp.shape[0], xs.shape[0]
    tiles_p, tiles_s = mp // tm, ms // tm
    rows_p = lambda w: pl.BlockSpec((tm, w), lambda i: (jnp.minimum(i, tiles_p - 1), 0))
    rows_s = lambda w: pl.BlockSpec((tm, w), lambda i: (jnp.maximum(i - tiles_p, 0), 0))
    full = lambda shape: pl.BlockSpec(shape, lambda i: (0,) * len(shape))
    return pl.pallas_call(
        functools.partial(_out_router_kernel, tm=tm, tiles_p=tiles_p),
        grid=(tiles_p + tiles_s,),
        in_specs=[rows_p(D_MODEL), rows_p(D_GLA), rows_p(D_RWKV),
                  rows_s(D_MODEL), rows_s(D_GLA), rows_s(D_RWKV),
                  full(wo.shape), full((1, D_MODEL)), full(rw.shape), full((1, ROUTE_LANES))],
        out_specs=pl.BlockSpec((tm, X1_WIDTH), lambda i: (i, 0)),
        out_shape=jax.ShapeDtypeStruct((mp + ms, X1_WIDTH), F32),
        compiler_params=_cparams(("arbitrary",)),
        name="out_router",
    )(xp, gp, rp, xs, gs, rs, wo, ffn_norm, rw, rb)


def _moe_kernel(tile_group_ref, tile_rows_ref, tile_pos_ref, order_ref,
                x1_hbm, wg_ref, wu_ref, wd_ref, fg_ref, fn_ref, yp_hbm, ys_hbm,
                acc, hbuf, wgb, wub, wdb, gsem, ssem, *, sub, rows_p, n_tokens):
    j = pl.program_id(0)
    e = pl.program_id(1)
    nrows = tile_rows_ref[j]
    pos0 = tile_pos_ref[j]
    n_sub = (nrows + sub - 1) // sub

    def token(rr):
        return order_ref[jnp.minimum(pos0 + rr, n_tokens - 1)]

    def gather_copy(rr):
        return pltpu.make_async_copy(x1_hbm.at[pl.ds(token(rr), 1)], acc.at[pl.ds(rr, 1)], gsem)

    def scatter_start(rr):
        tok = token(rr)
        row = acc.at[pl.ds(rr, 1), pl.ds(0, D_MODEL)]

        @pl.when(tok < rows_p)
        def _():
            pltpu.make_async_copy(row, yp_hbm.at[pl.ds(tok, 1)], ssem).start()

        @pl.when(tok >= rows_p)
        def _():
            pltpu.make_async_copy(row, ys_hbm.at[pl.ds(tok - rows_p, 1)], ssem).start()

    def scatter_wait(rr):
        pltpu.make_async_copy(acc.at[pl.ds(rr, 1), pl.ds(0, D_MODEL)], yp_hbm.at[pl.ds(0, 1)], ssem).wait()

    def for_rows(n_blocks, fn):
        def blk(bi, cy):
            for q in range(DMA_UNROLL):
                fn(bi * DMA_UNROLL + q)
            return cy
        lax.fori_loop(0, n_blocks, blk, 0)

    @pl.when((e == 0) & (nrows > 0))
    def _():
        n_blk = n_sub * (sub // DMA_UNROLL)
        for_rows(n_blk, lambda rr: gather_copy(rr).start())
        for_rows(n_blk, lambda rr: gather_copy(rr).wait())

        def norm(sb, cy):
            rs = pl.ds(pl.multiple_of(sb * sub, sub), sub)
            hbuf[rs, :] = _rms(acc[rs, :D_MODEL], fg_ref[...]).astype(BF16)
            return cy
        lax.fori_loop(0, n_sub, norm, 0)

    @pl.when(nrows > 0)
    def _():
        g = tile_group_ref[j]
        wgb[...] = wg_ref[0, 0].astype(BF16)
        wub[...] = wu_ref[0, 0].astype(BF16)
        wdb[...] = wd_ref[0, 0].astype(BF16)
        lane = lax.broadcasted_iota(jnp.int32, (sub, ROUTE_LANES), 1)
        sel = lane == N_GROUPS + g * EXPERTS_PER_GROUP + e

        def expert(sb, cy):
            rs = pl.ds(pl.multiple_of(sb * sub, sub), sub)
            wcol = jnp.sum(jnp.where(sel, acc[rs, D_MODEL:], 0.0), axis=-1, keepdims=True)
            h2 = hbuf[rs, :]
            gate = jnp.dot(h2, wgb[...], preferred_element_type=F32)
            up = jnp.dot(h2, wub[...], preferred_element_type=F32)
            hid = (gate * jax.nn.sigmoid(gate)) * up * wcol
            acc[rs, :D_MODEL] += jnp.dot(hid.astype(BF16), wdb[...], preferred_element_type=F32)
            return cy
        lax.fori_loop(0, n_sub, expert, 0)

    @pl.when((e == EXPERTS_PER_GROUP - 1) & (nrows > 0))
    def _():
        def final(sb, cy):
            rs = pl.ds(pl.multiple_of(sb * sub, sub), sub)
            acc[rs, :D_MODEL] = _rms(acc[rs, :D_MODEL], fn_ref[...])
            return cy
        lax.fori_loop(0, n_sub, final, 0)

        n_full = nrows // DMA_UNROLL
        for_rows(n_full, scatter_start)

        def rest_start(rr, cy):
            scatter_start(rr)
            return cy
        lax.fori_loop(n_full * DMA_UNROLL, nrows, rest_start, 0)
        for_rows(n_full, scatter_wait)

        def rest_wait(rr, cy):
            scatter_wait(rr)
            return cy
        lax.fori_loop(n_full * DMA_UNROLL, nrows, rest_wait, 0)


def _moe(x1e, tile_group, tile_rows, tile_pos, order, wg, wu, wd, ffn_norm, final_norm, *, tile, sub, rows_p):
    m = x1e.shape[0]
    n_tiles = tile_group.shape[0]

    def w_idx(j, e, tg, tr, tp, od):
        return (tg[j], jnp.where(tr[j] > 0, e, EXPERTS_PER_GROUP - 1), 0, 0)

    vec = pl.BlockSpec((1, D_MODEL), lambda j, e, tg, tr, tp, od: (0, 0))
    grid_spec = pltpu.PrefetchScalarGridSpec(
        num_scalar_prefetch=4,
        grid=(n_tiles, EXPERTS_PER_GROUP),
        in_specs=[pl.BlockSpec(memory_space=pl.ANY),
                  pl.BlockSpec((1, 1, D_MODEL, D_EXPERT), w_idx),
                  pl.BlockSpec((1, 1, D_MODEL, D_EXPERT), w_idx),
                  pl.BlockSpec((1, 1, D_EXPERT, D_MODEL), w_idx),
                  vec, vec],
        out_specs=[pl.BlockSpec(memory_space=pl.ANY), pl.BlockSpec(memory_space=pl.ANY)],
        scratch_shapes=[pltpu.VMEM((tile, X1_WIDTH), F32),
                        pltpu.VMEM((tile, D_MODEL), BF16),
                        pltpu.VMEM((D_MODEL, D_EXPERT), BF16),
                        pltpu.VMEM((D_MODEL, D_EXPERT), BF16),
                        pltpu.VMEM((D_EXPERT, D_MODEL), BF16),
                        pltpu.SemaphoreType.DMA(()),
                        pltpu.SemaphoreType.DMA(())],
    )
    return pl.pallas_call(
        functools.partial(_moe_kernel, sub=sub, rows_p=rows_p, n_tokens=m),
        grid_spec=grid_spec,
        out_shape=[jax.ShapeDtypeStruct((rows_p, D_MODEL), F32),
                   jax.ShapeDtypeStruct((m - rows_p, D_MODEL), F32)],
        compiler_params=_cparams(("arbitrary", "arbitrary")),
        name="moe",
    )(tile_group, tile_rows, tile_pos, order, x1e, wg, wu, wd, ffn_norm, final_norm)


def _moe_plan(group_id, tile):
    m = group_id.shape[0]
    n_tiles = m // tile + N_GROUPS
    order = jnp.argsort(group_id, stable=True).astype(jnp.int32)
    counts = jnp.sum(group_id[:, None] == jnp.arange(N_GROUPS, dtype=jnp.int32)[None, :], axis=0).astype(jnp.int32)
    tiles_per = (counts + tile - 1) // tile
    tile_end = jnp.cumsum(tiles_per)
    tile_start = tile_end - tiles_per
    row_start = jnp.cumsum(counts) - counts
    t = jnp.arange(n_tiles, dtype=jnp.int32)
    used = t < tile_end[-1]
    grp = jnp.minimum(jnp.sum(t[:, None] >= tile_end[None, :], axis=1), N_GROUPS - 1).astype(jnp.int32)
    last_group = jnp.max(jnp.where(counts > 0, jnp.arange(N_GROUPS, dtype=jnp.int32), 0))
    tile_group = jnp.where(used, grp, last_group).astype(jnp.int32)
    local = (t - tile_start[grp]) * tile
    tile_rows = jnp.where(used, jnp.clip(counts[grp] - local, 0, tile), 0).astype(jnp.int32)
    tile_pos = jnp.where(used, row_start[grp] + local, 0).astype(jnp.int32)
    return tile_group, tile_rows, tile_pos, order


def _prep_params(mix_norm, w_in, gla_gate_w1, gla_gate_w2, gla_gate_b, gla_norm, rwkv_mu_rkv, rwkv_mu_wag,
                 rwkv_w0, rwkv_w1, rwkv_w2, rwkv_a0, rwkv_a1, rwkv_a2, rwkv_g1, rwkv_g2,
                 rwkv_k_k, rwkv_k_a, rwkv_r_k, rwkv_ln_w, rwkv_ln_b, w_out, ffn_norm,
                 router_group_w, router_group_b, router_expert_w, router_expert_b, final_norm):
    row = lambda v: v.reshape(1, -1).astype(F32)
    g_rank = 2 * LANE
    p = dict(
        mix_norm=row(mix_norm[0]),
        mu_wag=rwkv_mu_wag[0],
        w_in=w_in[0].astype(BF16),
        gw1=_pad_axis(gla_gate_w1[0], 1, LANE).astype(BF16),
        gw2=_pad_axis(gla_gate_w2[0], 0, LANE).astype(BF16),
        gb=row(gla_gate_b[0]),
        w1=_pad_axis(rwkv_w1[0], 1, LANE).astype(BF16),
        w2=_pad_axis(rwkv_w2[0], 0, LANE).astype(BF16),
        w0=row(rwkv_w0[0]),
        a1=_pad_axis(rwkv_a1[0], 1, LANE).astype(BF16),
        a2=_pad_axis(rwkv_a2[0], 0, LANE).astype(BF16),
        a0=row(rwkv_a0[0]),
        g1=_pad_axis(rwkv_g1[0], 1, g_rank).astype(BF16),
        g2=_pad_axis(rwkv_g2[0], 0, g_rank).astype(BF16),
        gla_norm=row(gla_norm[0]),
        mu_ext=jnp.concatenate([jnp.zeros((1, RWKV_OFF), F32), rwkv_mu_rkv[0].reshape(1, -1)], axis=1),
        k_k=row(rwkv_k_k[0]), k_a=row(rwkv_k_a[0]), r_k=row(rwkv_r_k[0]),
        ln_w=row(rwkv_ln_w[0]), ln_b=row(rwkv_ln_b[0]),
        wo=w_out[0].astype(BF16).reshape(2, D_GLA, D_MODEL),
        ffn_norm=row(ffn_norm[0]),
        router_w=_pad_axis(jnp.concatenate([router_group_w[0], router_expert_w[0]], axis=1), 1,
                           ROUTE_LANES).astype(BF16),
        router_b=_pad_axis(jnp.concatenate([router_group_b[0], router_expert_b[0]]).reshape(1, -1), 1,
                           ROUTE_LANES).astype(F32),
        final_norm=row(final_norm),
    )
    r = jnp.arange(MXU_TILE)
    p['gla_e'] = (r[:, None] // GLA_DK == jnp.arange(2 * GLA_DV)[None, :] // GLA_DV).astype(BF16)
    p['wkv_bd'] = (r[:, None] // RWKV_HEAD == r[None, :] // RWKV_HEAD).astype(BF16)
    return p


def _mixer(x, s_gla, s_wkv, s_shift, p, *, tm, tc):
    nb, seq, _ = x.shape
    m = nb * seq
    x2 = x.reshape(m, D_MODEL)
    multi_seq = tm > seq

    h_last = jnp.zeros((nb, D_MODEL), F32) if s_shift is None else s_shift
    zl = _plain_proj(_pad_axis(h_last, 0, -(-nb // 16) * 16).astype(BF16), p['w_in'], tn=1024)[:nb]
    if multi_seq:
        expand = lambda a: jnp.repeat(a, seq, axis=0)
        hprev, zprev = expand(h_last), expand(zl)
    else:
        hprev, zprev = h_last.reshape(nb, 1, D_MODEL), zl.reshape(nb, 1, D_IN)

    h_bf, la, lw, a_sig, gate, hlast = _norm_proj(x2, hprev, p, seq=seq, tm=tm)
    shift_new = hlast.reshape(nb, seq, D_MODEL)[:, -1] if multi_seq else hlast.reshape(nb, D_MODEL)
    z = _in_proj(h_bf, p['w_in'], p['mu_ext'], zprev, seq=seq, tm=tm if multi_seq else min(seq, 1024), tn=1024)

    z3 = z.reshape(nb, seq, D_IN)
    r3 = lambda a: a.reshape(nb, seq, a.shape[-1])
    if s_gla is None:
        s_gla = jnp.zeros((nb, GLA_HEADS, GLA_DK, GLA_DV), F32)
        s_wkv = jnp.zeros((nb, RWKV_HEADS, RWKV_HEAD, RWKV_HEAD), F32)
    og, gla_new = _gla(z3, r3(la), s_gla, p['gla_e'], p['gla_norm'], tc=tc, chunk=min(GLA_CHUNK, seq),
                       nseq=2 if nb % 2 == 0 else 1)
    wkv_seqs = 4 if nb % 4 == 0 else 1
    orr, wkv_new = _wkv(z3, r3(lw), r3(a_sig), r3(gate), s_wkv, p['wkv_bd'], p, tc=min(tc, 512 // wkv_seqs),
                        chunk=min(WKV_CHUNK, seq), nseq=wkv_seqs)
    return (x2, og.reshape(m, D_GLA), orr.reshape(m, D_RWKV)), (gla_new, wkv_new, shift_new)


def _ffn(rows_p, rows_s, p, wg, wu, wd, *, tm, moe_tile, moe_sub):
    mp = rows_p[0].shape[0]
    x1e = _out_router(*rows_p, *rows_s, p['wo'], p['ffn_norm'], p['router_w'], p['router_b'], tm=tm)
    group_id = x1e[:, X1_WIDTH - 1].astype(jnp.int32)
    tile_group, tile_rows, tile_pos, order = _moe_plan(group_id, moe_tile)
    return _moe(x1e, tile_group, tile_rows, tile_pos, order, wg, wu, wd, p['ffn_norm'], p['final_norm'],
                tile=moe_tile, sub=moe_sub, rows_p=mp)


def kernel(x_prompt, x_sample, state_gla, state_wkv, state_shift, mix_norm, w_in, gla_gate_w1, gla_gate_w2, gla_gate_b, gla_norm, rwkv_mu_rkv, rwkv_mu_wag, rwkv_w0, rwkv_w1, rwkv_w2, rwkv_a0, rwkv_a1, rwkv_a2, rwkv_g1, rwkv_g2, rwkv_k_k, rwkv_k_a, rwkv_r_k, rwkv_ln_w, rwkv_ln_b, w_out, ffn_norm, router_group_w, router_group_b, router_expert_w, router_expert_b, expert_w_gate, expert_w_up, expert_w_down, final_norm):
    p = _prep_params(mix_norm, w_in, gla_gate_w1, gla_gate_w2, gla_gate_b, gla_norm, rwkv_mu_rkv, rwkv_mu_wag,
                     rwkv_w0, rwkv_w1, rwkv_w2, rwkv_a0, rwkv_a1, rwkv_a2, rwkv_g1, rwkv_g2,
                     rwkv_k_k, rwkv_k_a, rwkv_r_k, rwkv_ln_w, rwkv_ln_b, w_out, ffn_norm,
                     router_group_w, router_group_b, router_expert_w, router_expert_b, final_norm)
    wg, wu, wd = expert_w_gate[0], expert_w_up[0], expert_w_down[0]
    nb_p, seq_p, _ = x_prompt.shape
    nb_s, seq_s, _ = x_sample.shape
    rows_p, (gla_p, wkv_p, shift_p) = _mixer(x_prompt, None, None, None, p, tm=256, tc=min(256, seq_p))
    rows_s, (gla_s, wkv_s, shift_s) = _mixer(x_sample, state_gla[0], state_wkv[0], state_shift[0], p,
                                             tm=nb_s * seq_s, tc=seq_s)
    y_p, y_s = _ffn(rows_p, rows_s, p, wg, wu, wd, tm=256, moe_tile=MOE_SUBS_PER_TILE * MOE_SUB, moe_sub=MOE_SUB)
    return (y_p.reshape(nb_p, seq_p, D_MODEL), y_s.reshape(nb_s, seq_s, D_MODEL),
            gla_p[None], wkv_p[None], shift_p[None], gla_s[None], wkv_s[None], shift_s[None])
```

```python
import functools

import jax
import jax.numpy as jnp
from jax import lax
from jax.experimental import pallas as pl
from jax.experimental.pallas import tpu as pltpu

F32 = jnp.float32
BF16 = jnp.bfloat16

D_MODEL = 2048
D_GLA = 1024
D_RWKV = 1024
GLA_HEADS = 4
GLA_DV = 256
GLA_DK = 128
GLA_K = 512
GLA_GATE_TEMP = 16.0
RWKV_HEAD = 64
RWKV_HEADS = 16
N_GROUPS = 8
EXPERTS_PER_GROUP = 8
D_EXPERT = 512
NORM_EPS = 1e-6
GN_EPS = 64e-5
RWKV_OFF = 2 * GLA_K + 2 * D_GLA
D_IN = RWKV_OFF + 3 * D_RWKV

LANE = 128
SUBLANE = 8
MXU_TILE = 256
VMEM_LIMIT = 56 * 1024 * 1024

GLA_CHUNK = 16
WKV_CHUNK = 8
MOE_SUB = 256
MOE_SUBS_PER_TILE = 5
ROUTE_LANES = 128
X1_WIDTH = D_MODEL + ROUTE_LANES
DMA_UNROLL = 8


def _cparams(sem):
    return pltpu.CompilerParams(dimension_semantics=sem, vmem_limit_bytes=VMEM_LIMIT)


def _rms(x, gain):
    return x * lax.rsqrt(jnp.mean(x * x, axis=-1, keepdims=True) + NORM_EPS) * gain


def _softplus(y):
    return jnp.maximum(y, 0.0) + jnp.log1p(jnp.exp(-jnp.abs(y)))


def _dot(a, b):
    return jnp.dot(a.astype(BF16), b.astype(BF16), preferred_element_type=F32)


def _dot_nt(a, b):
    return lax.dot_general(a.astype(BF16), b.astype(BF16), (((1,), (1,)), ((), ())),
                           preferred_element_type=F32)


def _dot_tn(a, b):
    return lax.dot_general(a.astype(BF16), b.astype(BF16), (((0,), (0,)), ((), ())),
                           preferred_element_type=F32)


def _norm_proj_kernel(x_ref, hprev_ref, gain_ref, mu_ref, gw1_ref, gw2_ref, gb_ref,
                      w1_ref, w2_ref, w0_ref, a1_ref, a2_ref, a0_ref, g1_ref, g2_ref,
                      h_ref, la_ref, lw_ref, a_ref, gate_ref, hlast_ref, carry_ref,
                      *, tm, seq, multi_seq):
    i = pl.program_id(0)
    h = _rms(x_ref[...], gain_ref[...])
    row = lax.broadcasted_iota(jnp.int32, (tm, 1), 0)
    rolled = pltpu.roll(h, 1, 0)
    if multi_seq:
        h_shift = jnp.where(row % seq == 0, hprev_ref[...], rolled)
        hlast_ref[...] = h
    else:
        tiles_per_seq = seq // tm
        first = (i % tiles_per_seq) == 0
        prev = jnp.where(first, hprev_ref[0], carry_ref[...])
        h_shift = jnp.where(row == 0, prev, rolled)
        carry_ref[...] = h[tm - 1:tm]
        hlast_ref[0] = h[tm - 1:tm]
    h_ref[...] = h.astype(BF16)

    dx = h_shift - h
    mu = mu_ref[...]
    xw = h + dx * mu[0:1]
    xa = h + dx * mu[1:2]
    xg = h + dx * mu[2:3]

    gl = _dot(_dot(h, gw1_ref[...]), gw2_ref[...]) + gb_ref[...]
    la_ref[...] = -_softplus(-gl) * (1.0 / GLA_GATE_TEMP)

    u = w0_ref[...] + _dot(jnp.tanh(_dot(xw, w1_ref[...])), w2_ref[...])
    w_log = -_softplus(-u) - 0.5
    lw_ref[...] = -jnp.exp(w_log)

    a_ref[...] = jax.nn.sigmoid(a0_ref[...] + _dot(_dot(xa, a1_ref[...]), a2_ref[...]))
    gate_ref[...] = _dot(jax.nn.sigmoid(_dot(xg, g1_ref[...])), g2_ref[...])


def _pad_axis(w, axis, to):
    pad = [(0, 0)] * w.ndim
    pad[axis] = (0, to - w.shape[axis])
    return jnp.pad(w, pad)


def _norm_proj(x2, hprev, p, *, seq, tm):
    m = x2.shape[0]
    multi_seq = tm > seq
    nb = m // seq
    full = lambda shape: pl.BlockSpec(shape, lambda i: (0,) * len(shape))
    rows = lambda w: pl.BlockSpec((tm, w), lambda i: (i, 0))
    if multi_seq:
        assert tm == m
        hprev_spec = rows(D_MODEL)
        hlast_shape = jax.ShapeDtypeStruct((m, D_MODEL), F32)
        hlast_spec = rows(D_MODEL)
    else:
        assert seq % tm == 0
        tps = seq // tm
        hprev_spec = pl.BlockSpec((1, 1, D_MODEL), lambda i: (i // tps, 0, 0))
        hlast_shape = jax.ShapeDtypeStruct((nb, 1, D_MODEL), F32)
        hlast_spec = pl.BlockSpec((1, 1, D_MODEL), lambda i: (i // tps, 0, 0))
    weights = [p['gw1'], p['gw2'], p['gb'], p['w1'], p['w2'], p['w0'], p['a1'], p['a2'], p['a0'],
               p['g1'], p['g2']]
    return pl.pallas_call(
        functools.partial(_norm_proj_kernel, tm=tm, seq=seq, multi_seq=multi_seq),
        grid=(m // tm,),
        in_specs=[rows(D_MODEL), hprev_spec, full((1, D_MODEL)), full((3, D_MODEL))]
                 + [full(w.shape) for w in weights],
        out_specs=[rows(D_MODEL), rows(GLA_K), rows(D_RWKV), rows(D_RWKV), rows(D_RWKV), hlast_spec],
        out_shape=[jax.ShapeDtypeStruct((m, D_MODEL), BF16),
                   jax.ShapeDtypeStruct((m, GLA_K), F32),
                   jax.ShapeDtypeStruct((m, D_RWKV), F32),
                   jax.ShapeDtypeStruct((m, D_RWKV), F32),
                   jax.ShapeDtypeStruct((m, D_RWKV), F32),
                   hlast_shape],
        scratch_shapes=[pltpu.VMEM((1, D_MODEL), F32)],
        compiler_params=_cparams(("arbitrary",)),
        name="norm_proj",
    )(x2, hprev, p['mix_norm'], p['mu_wag'], *weights)


def _in_proj_kernel(h_ref, w_ref, mu_ref, zprev_ref, z_ref, carry_ref, *, tm, seq, multi_seq):
    i = pl.program_id(1)
    z = jnp.dot(h_ref[...], w_ref[...], preferred_element_type=F32)
    row = lax.broadcasted_iota(jnp.int32, (tm, 1), 0)
    rolled = pltpu.roll(z, 1, 0)
    if multi_seq:
        z_prev = jnp.where(row % seq == 0, zprev_ref[...], rolled)
    else:
        tiles_per_seq = seq // tm
        first = (i % tiles_per_seq) == 0
        prev = jnp.where(first, zprev_ref[0], carry_ref[...])
        z_prev = jnp.where(row == 0, prev, rolled)
        carry_ref[...] = z[tm - 1:tm]
    z_ref[...] = z + mu_ref[...] * (z_prev - z)


def _in_proj(h_bf, w_in, mu_ext, zprev, *, seq, tm, tn):
    m = h_bf.shape[0]
    multi_seq = tm > seq
    if multi_seq:
        zprev_spec = pl.BlockSpec((tm, tn), lambda j, i: (i, j))
    else:
        tps = seq // tm
        zprev_spec = pl.BlockSpec((1, 1, tn), lambda j, i: (i // tps, 0, j))
    return pl.pallas_call(
        functools.partial(_in_proj_kernel, tm=tm, seq=seq, multi_seq=multi_seq),
        grid=(D_IN // tn, m // tm),
        in_specs=[pl.BlockSpec((tm, D_MODEL), lambda j, i: (i, 0)),
                  pl.BlockSpec((D_MODEL, tn), lambda j, i: (0, j)),
                  pl.BlockSpec((1, tn), lambda j, i: (0, j)),
                  zprev_spec],
        out_specs=pl.BlockSpec((tm, tn), lambda j, i: (i, j)),
        out_shape=jax.ShapeDtypeStruct((m, D_IN), F32),
        scratch_shapes=[pltpu.VMEM((1, tn), F32)],
        compiler_params=_cparams(("arbitrary", "arbitrary")),
        name="in_proj",
    )(h_bf, w_in, mu_ext, zprev)


def _plain_proj_kernel(h_ref, w_ref, z_ref):
    z_ref[...] = jnp.dot(h_ref[...], w_ref[...], preferred_element_type=F32)


def _plain_proj(h_bf, w_in, *, tn):
    m = h_bf.shape[0]
    return pl.pallas_call(
        _plain_proj_kernel,
        grid=(D_IN // tn,),
        in_specs=[pl.BlockSpec((m, D_MODEL), lambda j: (0, 0)),
                  pl.BlockSpec((D_MODEL, tn), lambda j: (0, j))],
        out_specs=pl.BlockSpec((m, tn), lambda j: (0, j)),
        out_shape=jax.ShapeDtypeStruct((m, D_IN), F32),
        compiler_params=_cparams(("arbitrary",)),
        name="prev_row_proj",
    )(h_bf, w_in)


def _gla_kernel(q_ref, k_ref, v_ref, g_ref, la_ref, s0_ref, e_ref, gn_ref,
                o_ref, s_ref, st_ref, obuf_ref, *, tc, chunk, nseq):
    tb = pl.program_id(1)
    c = chunk

    @pl.when(tb == 0)
    def _():
        for q in range(nseq):
            for hh in range(GLA_HEADS):
                st_ref[q, :, hh * GLA_DK:(hh + 1) * GLA_DK] = s0_ref[q, hh].T

    row = lax.broadcasted_iota(jnp.int32, (c, 1), 0)
    e_mat = e_ref[...]
    neg = jnp.float32(-1e30)

    def one_chunk(q, r0):
        qq = q_ref[q, pl.ds(r0, c), :] * (GLA_DK ** -0.5)
        k = k_ref[q, pl.ds(r0, c), :]
        v = v_ref[q, pl.ds(r0, c), :]
        la = la_ref[q, pl.ds(r0, c), :]
        b = jnp.zeros_like(la)
        for s in range(c):
            b = b + jnp.where(row >= s, la[s:s + 1], 0.0)
        b_last = b[c - 1:c]

        parts = []
        for s in range(c):
            dec = jnp.exp(jnp.where(row >= s, b - b[s:s + 1], neg))
            parts.append((qq * k[s:s + 1] * dec).astype(BF16))
        pcat = jnp.concatenate(parts, axis=0)
        sc = jnp.concatenate(
            [jnp.dot(pcat[:, hp * MXU_TILE:(hp + 1) * MXU_TILE], e_mat, preferred_element_type=F32)
             for hp in range(GLA_K // MXU_TILE)], axis=1)
        o = jnp.zeros((c, D_GLA), F32)
        for s in range(c):
            o = o + sc[s * c:(s + 1) * c] * v[s:s + 1]

        qd = qq * jnp.exp(b)
        kd = k * jnp.exp(b_last - b)
        a_last = jnp.exp(b_last)
        o_heads = []
        for hh in range(GLA_HEADS):
            dk = slice(hh * GLA_DK, (hh + 1) * GLA_DK)
            dv = slice(hh * GLA_DV, (hh + 1) * GLA_DV)
            st = st_ref[q, :, dk]
            oh = o[:, dv] + _dot_nt(qd[:, dk], st)
            st_ref[q, :, dk] = a_last[:, dk] * st + _dot_tn(v[:, dv], kd[:, dk])
            oh = oh * lax.rsqrt(jnp.mean(oh * oh, axis=-1, keepdims=True) + NORM_EPS)
            o_heads.append(oh)
        og = jnp.concatenate(o_heads, axis=1)
        g = g_ref[q, pl.ds(r0, c), :]
        obuf_ref[q, pl.ds(r0, c), :] = og * gn_ref[...] * (g * jax.nn.sigmoid(g))

    def body(ci, carry):
        r0 = pl.multiple_of(ci * c, c)
        for q in range(nseq):
            one_chunk(q, r0)
        return carry

    lax.fori_loop(0, tc // c, body, 0)
    o_ref[...] = obuf_ref[...].astype(BF16)

    @pl.when(tb == pl.num_programs(1) - 1)
    def _():
        for q in range(nseq):
            for hh in range(GLA_HEADS):
                s_ref[q, hh] = st_ref[q, :, hh * GLA_DK:(hh + 1) * GLA_DK].T


def _gla(z3, la3, s0, e_mat, gla_norm, *, tc, chunk, nseq):
    nb, seq, _ = z3.shape
    blk = lambda w, col: pl.BlockSpec((nseq, tc, w), lambda b, t: (b, t, col))
    st_spec = pl.BlockSpec((nseq, GLA_HEADS, GLA_DK, GLA_DV), lambda b, t: (b, 0, 0, 0))
    return pl.pallas_call(
        functools.partial(_gla_kernel, tc=tc, chunk=chunk, nseq=nseq),
        grid=(nb // nseq, seq // tc),
        in_specs=[blk(GLA_K, 0), blk(GLA_K, 1), blk(D_GLA, 1), blk(D_GLA, 2), blk(GLA_K, 0),
                  st_spec,
                  pl.BlockSpec(e_mat.shape, lambda b, t: (0, 0)),
                  pl.BlockSpec((1, D_GLA), lambda b, t: (0, 0))],
        out_specs=[blk(D_GLA, 0), st_spec],
        out_shape=[jax.ShapeDtypeStruct((nb, seq, D_GLA), BF16),
                   jax.ShapeDtypeStruct((nb, GLA_HEADS, GLA_DK, GLA_DV), F32)],
        scratch_shapes=[pltpu.VMEM((nseq, GLA_DV, GLA_K), F32), pltpu.VMEM((nseq, tc, D_GLA), F32)],
        compiler_params=_cparams(("arbitrary", "arbitrary")),
        name="gla",
    )(z3, z3, z3, z3, la3, s0, e_mat, gla_norm)


N_LANE_TILES = D_RWKV // MXU_TILE
HEADS_PER_TILE = MXU_TILE // RWKV_HEAD


def _stack_tiles(x):
    return jnp.concatenate([x[:, t * MXU_TILE:(t + 1) * MXU_TILE] for t in range(N_LANE_TILES)], axis=0)


def _unstack_tiles(y, r):
    return jnp.concatenate([y[t * r:(t + 1) * r] for t in range(N_LANE_TILES)], axis=1)


def _seg_sums(xs, bd):
    c = xs[0].shape[0]
    st = jnp.concatenate([_stack_tiles(x) for x in xs], axis=0)
    hi = st.astype(BF16)
    lo = (st - hi.astype(F32)).astype(BF16)
    res = jnp.dot(jnp.concatenate([hi, lo], axis=0), bd, preferred_element_type=F32)
    half = st.shape[0]
    res = res[:half] + res[half:]
    rows = N_LANE_TILES * c
    return [_unstack_tiles(res[n * rows:(n + 1) * rows], c) for n in range(len(xs))]


def _cumsum_rows(x, row, c):
    if c == SUBLANE:
        step = 1
        while step < c:
            x = x + jnp.where(row >= step, pltpu.roll(x, step, 0), 0.0)
            step *= 2
        return x
    out = jnp.zeros_like(x)
    for s in range(c):
        out = out + jnp.where(row >= s, x[s:s + 1], 0.0)
    return out


def _wkv_kernel(r_ref, k_ref, v_ref, lw_ref, a_ref, gate_ref, s0_ref, bd_ref,
                kk_ref, ka_ref, rk_ref, lnw_ref, lnb_ref,
                o_ref, s_ref, st_ref, obuf_ref, *, tc, chunk, nseq):
    tb = pl.program_id(1)
    c = chunk
    cc = c * c

    @pl.when(tb == 0)
    def _():
        for q in range(nseq):
            st_ref[q] = jnp.concatenate([s0_ref[q, hh] for hh in range(RWKV_HEADS)], axis=1)

    row = lax.broadcasted_iota(jnp.int32, (c, 1), 0)
    bd = bd_ref[...]
    lane_head = lax.broadcasted_iota(jnp.int32, (RWKV_HEAD, MXU_TILE), 1) // RWKV_HEAD
    neg = jnp.float32(-1e30)

    def one_chunk(q, r0):
        r = r_ref[q, pl.ds(r0, c), :]
        k = k_ref[q, pl.ds(r0, c), :]
        v = v_ref[q, pl.ds(r0, c), :]
        lw = lw_ref[q, pl.ds(r0, c), :]
        a_sig = a_ref[q, pl.ds(r0, c), :]

        kk = k * kk_ref[...]
        k_eff = k * (1.0 + (a_sig - 1.0) * ka_ref[...])
        kk_ss, bonus_dot = _seg_sums([kk * kk, r * k_eff * rk_ref[...]], bd)
        kk = kk / jnp.maximum(jnp.sqrt(kk_ss), 1e-12)
        a_vec = -kk
        b_vec = kk * a_sig

        lb = _cumsum_rows(lw, row, c)
        lbp = lb - lw
        lb_last = lb[c - 1:c]
        kinds = ([], [], [], [])
        for s in range(c):
            lbs = lb[s:s + 1]
            bs = b_vec[s:s + 1]
            ks = k_eff[s:s + 1]
            a_d = a_vec * jnp.exp(jnp.where(row > s, lbp - lbs, neg))
            r_d = r * jnp.exp(jnp.where(row >= s, lb - lbs, neg))
            kinds[0].append(a_d * bs)
            kinds[1].append(a_d * ks)
            kinds[2].append(r_d * bs)
            kinds[3].append(r_d * ks)
        prod = jnp.concatenate(
            [p[:, t * MXU_TILE:(t + 1) * MXU_TILE]
             for t in range(N_LANE_TILES) for kind in kinds for p in kind], axis=0)
        coef_t = jnp.dot(prod.astype(BF16), bd, preferred_element_type=F32)

        def coef(kind, s):
            return jnp.concatenate(
                [coef_t[(t * 4 + kind) * cc + s * c:(t * 4 + kind) * cc + (s + 1) * c]
                 for t in range(N_LANE_TILES)], axis=1)

        lhs = jnp.concatenate([a_vec * jnp.exp(lbp), r * jnp.exp(lb)], axis=0).astype(BF16)
        x0 = []
        for t in range(N_LANE_TILES):
            ln = slice(t * MXU_TILE, (t + 1) * MXU_TILE)
            st_t = st_ref[q, :, ln].astype(BF16)
            w_t = jnp.concatenate([st_t] * HEADS_PER_TILE, axis=0) * bd
            x0.append(lax.dot_general(lhs[:, ln], w_t, (((1,), (1,)), ((), ())),
                                      preferred_element_type=F32))
        x0 = jnp.concatenate(x0, axis=1)
        u = x0[:c]
        o = x0[c:]

        for s in range(c):
            u = u + coef(1, s) * v[s:s + 1]
        for s in range(c):
            u = u + coef(0, s) * u[s:s + 1]
        for s in range(c):
            o = o + coef(2, s) * u[s:s + 1] + coef(3, s) * v[s:s + 1]

        tail = jnp.exp(lb_last - lb)
        uv = jnp.concatenate([u, v], axis=0).astype(BF16)
        bk = jnp.concatenate([b_vec * tail, k_eff * tail], axis=0).astype(BF16)
        w_last = jnp.exp(lb_last)
        for t in range(N_LANE_TILES):
            ln = slice(t * MXU_TILE, (t + 1) * MXU_TILE)
            zt = _dot_tn(uv[:, ln], bk[:, ln])
            upd = zt[(HEADS_PER_TILE - 1) * RWKV_HEAD:]
            for hh in range(HEADS_PER_TILE - 2, -1, -1):
                upd = jnp.where(lane_head == hh, zt[hh * RWKV_HEAD:(hh + 1) * RWKV_HEAD], upd)
            st_ref[q, :, ln] = w_last[:, ln] * st_ref[q, :, ln] + upd

        inv_n = 1.0 / RWKV_HEAD
        mu = _seg_sums([o], bd)[0] * inv_n
        dev = o - mu
        var = _seg_sums([dev * dev], bd)[0] * inv_n
        yn = dev * lax.rsqrt(var + GN_EPS) * lnw_ref[...] + lnb_ref[...]
        obuf_ref[q, pl.ds(r0, c), :] = (yn + bonus_dot * v) * gate_ref[q, pl.ds(r0, c), :]

    def body(ci, carry):
        r0 = pl.multiple_of(ci * c, c)
        for q in range(nseq):
            one_chunk(q, r0)
        return carry

    lax.fori_loop(0, tc // c, body, 0)
    o_ref[...] = obuf_ref[...].astype(BF16)

    @pl.when(tb == pl.num_programs(1) - 1)
    def _():
        for q in range(nseq):
            for hh in range(RWKV_HEADS):
                s_ref[q, hh] = st_ref[q, :, hh * RWKV_HEAD:(hh + 1) * RWKV_HEAD]


def _wkv(z3, lw3, a3, gate3, s0, bd, p, *, tc, chunk, nseq):
    nb, seq, _ = z3.shape
    col0 = RWKV_OFF // D_RWKV
    blk = lambda col: pl.BlockSpec((nseq, tc, D_RWKV), lambda b, t: (b, t, col))
    st_spec = pl.BlockSpec((nseq, RWKV_HEADS, RWKV_HEAD, RWKV_HEAD), lambda b, t: (b, 0, 0, 0))
    vec = pl.BlockSpec((1, D_RWKV), lambda b, t: (0, 0))
    return pl.pallas_call(
        functools.partial(_wkv_kernel, tc=tc, chunk=chunk, nseq=nseq),
        grid=(nb // nseq, seq // tc),
        in_specs=[blk(col0), blk(col0 + 1), blk(col0 + 2), blk(0), blk(0), blk(0), st_spec,
                  pl.BlockSpec(bd.shape, lambda b, t: (0, 0)), vec, vec, vec, vec, vec],
        out_specs=[blk(0), st_spec],
        out_shape=[jax.ShapeDtypeStruct((nb, seq, D_RWKV), BF16),
                   jax.ShapeDtypeStruct((nb, RWKV_HEADS, RWKV_HEAD, RWKV_HEAD), F32)],
        scratch_shapes=[pltpu.VMEM((nseq, RWKV_HEAD, D_RWKV), F32),
                        pltpu.VMEM((nseq, tc, D_RWKV), F32)],
        compiler_params=_cparams(("arbitrary", "arbitrary")),
        name="wkv",
    )(z3, z3, z3, lw3, a3, gate3, s0, bd, p['k_k'], p['k_a'], p['r_k'], p['ln_w'], p['ln_b'])


def _out_router_kernel(xp_ref, gp_ref, rp_ref, xs_ref, gs_ref, rs_ref, wo_ref, fg_ref, rw_ref, rb_ref,
                       out_ref, *, tm, tiles_p):
    is_p = pl.program_id(0) < tiles_p
    x = jnp.where(is_p, xp_ref[...], xs_ref[...])
    og = jnp.where(is_p, gp_ref[...], gs_ref[...])
    orr = jnp.where(is_p, rp_ref[...], rs_ref[...])
    mix = (jnp.dot(og, wo_ref[0], preferred_element_type=F32)
           + jnp.dot(orr, wo_ref[1], preferred_element_type=F32))
    x1 = x + mix
    out_ref[:, :D_MODEL] = x1
    h2 = _rms(x1, fg_ref[...])
    logits = _dot(h2, rw_ref[...]) + rb_ref[...]
    lane = lax.broadcasted_iota(jnp.int32, (tm, ROUTE_LANES), 1)
    neg = jnp.float32(-jnp.inf)
    big = jnp.int32(1 << 20)

    def first_argmax(vals):
        mx = jnp.max(vals, axis=-1, keepdims=True)
        idx = jnp.min(jnp.where(vals == mx, lane, big), axis=-1, keepdims=True)
        return mx, idx

    lg = jnp.where(lane < N_GROUPS, logits, neg)
    g_max, g_idx = first_argmax(lg)
    p_group = 1.0 / jnp.sum(jnp.exp(lg - g_max), axis=-1, keepdims=True)
    lo = N_GROUPS + g_idx * EXPERTS_PER_GROUP
    le = jnp.where((lane >= lo) & (lane < lo + EXPERTS_PER_GROUP), logits, neg)
    m1, i1 = first_argmax(le)
    m2, i2 = first_argmax(jnp.where(lane == i1, neg, le))
    e2 = jnp.exp(m2 - m1)
    w1 = 1.0 / (1.0 + e2)
    w2 = e2 / (1.0 + e2)
    route = jnp.where(lane == i1, w1, jnp.where(lane == i2, w2, 0.0)) * p_group
    out_ref[:, D_MODEL:] = jnp.where(lane == ROUTE_LANES - 1, g_idx.astype(F32), route)


def _out_router(xp, gp, rp, xs, gs, rs, wo, ffn_norm, rw, rb, *, tm):
    mp, ms = xp.shape[0], xs.shape[0]
    tiles_p, tiles_s = mp // tm, ms // tm
    rows_p = lambda w: pl.BlockSpec((tm, w), lambda i: (jnp.minimum(i, tiles_p - 1), 0))
    rows_s = lambda w: pl.BlockSpec((tm, w), lambda i: (jnp.maximum(i - tiles_p, 0), 0))
    full = lambda shape: pl.BlockSpec(shape, lambda i: (0,) * len(shape))
    return pl.pallas_call(
        functools.partial(_out_router_kernel, tm=tm, tiles_p=tiles_p),
        grid=(tiles_p + tiles_s,),
        in_specs=[rows_p(D_MODEL), rows_p(D_GLA), rows_p(D_RWKV),
                  rows_s(D_MODEL), rows_s(D_GLA), rows_s(D_RWKV),
                  full(wo.shape), full((1, D_MODEL)), full(rw.shape), full((1, ROUTE_LANES))],
        out_specs=pl.BlockSpec((tm, X1_WIDTH), lambda i: (i, 0)),
        out_shape=jax.ShapeDtypeStruct((mp + ms, X1_WIDTH), F32),
        compiler_params=_cparams(("arbitrary",)),
        name="out_router",
    )(xp, gp, rp, xs, gs, rs, wo, ffn_norm, rw, rb)


def _moe_kernel(tile_group_ref, tile_rows_ref, tile_pos_ref, order_ref,
                x1_hbm, wg_ref, wu_ref, wd_ref, fg_ref, fn_ref, yp_hbm, ys_hbm,
                acc, hbuf, wgb, wub, wdb, gsem, ssem, *, sub, rows_p, n_tokens):
    j = pl.program_id(0)
    e = pl.program_id(1)
    nrows = tile_rows_ref[j]
    pos0 = tile_pos_ref[j]
    half = sub // 2
    rem = nrows % sub
    has_small = (rem > 0) & (rem <= half)
    n_big = nrows // sub + jnp.where(rem > half, 1, 0)
    rows_used = n_big * sub + jnp.where(has_small, half, 0)

    def for_blocks(fn):
        def big(sb, cy):
            fn(pl.ds(pl.multiple_of(sb * sub, sub), sub), sub)
            return cy
        lax.fori_loop(0, n_big, big, 0)

        @pl.when(has_small)
        def _():
            fn(pl.ds(pl.multiple_of(n_big * sub, half), half), half)

    def token(rr):
        return order_ref[jnp.minimum(pos0 + rr, n_tokens - 1)]

    def gather_copy(rr):
        return pltpu.make_async_copy(x1_hbm.at[pl.ds(token(rr), 1)], acc.at[pl.ds(rr, 1)], gsem)

    def scatter_start(rr):
        tok = token(rr)
        row = acc.at[pl.ds(rr, 1), pl.ds(0, D_MODEL)]

        @pl.when(tok < rows_p)
        def _():
            pltpu.make_async_copy(row, yp_hbm.at[pl.ds(tok, 1)], ssem).start()

        @pl.when(tok >= rows_p)
        def _():
            pltpu.make_async_copy(row, ys_hbm.at[pl.ds(tok - rows_p, 1)], ssem).start()

    def scatter_wait(rr):
        pltpu.make_async_copy(acc.at[pl.ds(rr, 1), pl.ds(0, D_MODEL)], yp_hbm.at[pl.ds(0, 1)], ssem).wait()

    def for_rows(n_blocks, fn):
        def blk(bi, cy):
            for q in range(DMA_UNROLL):
                fn(bi * DMA_UNROLL + q)
            return cy
        lax.fori_loop(0, n_blocks, blk, 0)

    @pl.when((e == 0) & (nrows > 0))
    def _():
        n_blk = rows_used // DMA_UNROLL
        for_rows(n_blk, lambda rr: gather_copy(rr).start())
        for_rows(n_blk, lambda rr: gather_copy(rr).wait())

        def norm(rs, size):
            hbuf[rs, :] = _rms(acc[rs, :D_MODEL], fg_ref[...]).astype(BF16)
        for_blocks(norm)

    @pl.when(nrows > 0)
    def _():
        g = tile_group_ref[j]
        wgb[...] = wg_ref[0, 0].astype(BF16)
        wub[...] = wu_ref[0, 0].astype(BF16)
        wdb[...] = wd_ref[0, 0].astype(BF16)

        def expert(rs, size):
            lane = lax.broadcasted_iota(jnp.int32, (size, ROUTE_LANES), 1)
            sel = lane == N_GROUPS + g * EXPERTS_PER_GROUP + e
            wcol = jnp.sum(jnp.where(sel, acc[rs, D_MODEL:], 0.0), axis=-1, keepdims=True)
            h2 = hbuf[rs, :]
            gate = jnp.dot(h2, wgb[...], preferred_element_type=F32)
            up = jnp.dot(h2, wub[...], preferred_element_type=F32)
            hid = (gate * jax.nn.sigmoid(gate)) * up * wcol
            acc[rs, :D_MODEL] += jnp.dot(hid.astype(BF16), wdb[...], preferred_element_type=F32)
        for_blocks(expert)

    @pl.when((e == EXPERTS_PER_GROUP - 1) & (nrows > 0))
    def _():
        def final(rs, size):
            acc[rs, :D_MODEL] = _rms(acc[rs, :D_MODEL], fn_ref[...])
        for_blocks(final)

        n_full = nrows // DMA_UNROLL
        for_rows(n_full, scatter_start)

        def rest_start(rr, cy):
            scatter_start(rr)
            return cy
        lax.fori_loop(n_full * DMA_UNROLL, nrows, rest_start, 0)
        for_rows(n_full, scatter_wait)

        def rest_wait(rr, cy):
            scatter_wait(rr)
            return cy
        lax.fori_loop(n_full * DMA_UNROLL, nrows, rest_wait, 0)


def _moe(x1e, tile_group, tile_rows, tile_pos, order, wg, wu, wd, ffn_norm, final_norm, *, tile, sub, rows_p):
    m = x1e.shape[0]
    n_tiles = tile_group.shape[0]

    def w_idx(j, e, tg, tr, tp, od):
        return (tg[j], jnp.where(tr[j] > 0, e, EXPERTS_PER_GROUP - 1), 0, 0)

    vec = pl.BlockSpec((1, D_MODEL), lambda j, e, tg, tr, tp, od: (0, 0))
    grid_spec = pltpu.PrefetchScalarGridSpec(
        num_scalar_prefetch=4,
        grid=(n_tiles, EXPERTS_PER_GROUP),
        in_specs=[pl.BlockSpec(memory_space=pl.ANY),
                  pl.BlockSpec((1, 1, D_MODEL, D_EXPERT), w_idx),
                  pl.BlockSpec((1, 1, D_MODEL, D_EXPERT), w_idx),
                  pl.BlockSpec((1, 1, D_EXPERT, D_MODEL), w_idx),
                  vec, vec],
        out_specs=[pl.BlockSpec(memory_space=pl.ANY), pl.BlockSpec(memory_space=pl.ANY)],
        scratch_shapes=[pltpu.VMEM((tile, X1_WIDTH), F32),
                        pltpu.VMEM((tile, D_MODEL), BF16),
                        pltpu.VMEM((D_MODEL, D_EXPERT), BF16),
                        pltpu.VMEM((D_MODEL, D_EXPERT), BF16),
                        pltpu.VMEM((D_EXPERT, D_MODEL), BF16),
                        pltpu.SemaphoreType.DMA(()),
                        pltpu.SemaphoreType.DMA(())],
    )
    return pl.pallas_call(
        functools.partial(_moe_kernel, sub=sub, rows_p=rows_p, n_tokens=m),
        grid_spec=grid_spec,
        out_shape=[jax.ShapeDtypeStruct((rows_p, D_MODEL), F32),
                   jax.ShapeDtypeStruct((m - rows_p, D_MODEL), F32)],
        compiler_params=_cparams(("arbitrary", "arbitrary")),
        name="moe",
    )(tile_group, tile_rows, tile_pos, order, x1e, wg, wu, wd, ffn_norm, final_norm)


def _moe_plan(group_id, tile):
    m = group_id.shape[0]
    n_tiles = m // tile + N_GROUPS
    order = jnp.argsort(group_id, stable=True).astype(jnp.int32)
    counts = jnp.sum(group_id[:, None] == jnp.arange(N_GROUPS, dtype=jnp.int32)[None, :], axis=0).astype(jnp.int32)
    tiles_per = (counts + tile - 1) // tile
    tile_end = jnp.cumsum(tiles_per)
    tile_start = tile_end - tiles_per
    row_start = jnp.cumsum(counts) - counts
    t = jnp.arange(n_tiles, dtype=jnp.int32)
    used = t < tile_end[-1]
    grp = jnp.minimum(jnp.sum(t[:, None] >= tile_end[None, :], axis=1), N_GROUPS - 1).astype(jnp.int32)
    last_group = jnp.max(jnp.where(counts > 0, jnp.arange(N_GROUPS, dtype=jnp.int32), 0))
    tile_group = jnp.where(used, grp, last_group).astype(jnp.int32)
    local = (t - tile_start[grp]) * tile
    tile_rows = jnp.where(used, jnp.clip(counts[grp] - local, 0, tile), 0).astype(jnp.int32)
    tile_pos = jnp.where(used, row_start[grp] + local, 0).astype(jnp.int32)
    return tile_group, tile_rows, tile_pos, order


def _prep_params(mix_norm, w_in, gla_gate_w1, gla_gate_w2, gla_gate_b, gla_norm, rwkv_mu_rkv, rwkv_mu_wag,
                 rwkv_w0, rwkv_w1, rwkv_w2, rwkv_a0, rwkv_a1, rwkv_a2, rwkv_g1, rwkv_g2,
                 rwkv_k_k, rwkv_k_a, rwkv_r_k, rwkv_ln_w, rwkv_ln_b, w_out, ffn_norm,
                 router_group_w, router_group_b, router_expert_w, router_expert_b, final_norm):
    row = lambda v: v.reshape(1, -1).astype(F32)
    g_rank = 2 * LANE
    p = dict(
        mix_norm=row(mix_norm[0]),
        mu_wag=rwkv_mu_wag[0],
        w_in=w_in[0].astype(BF16),
        gw1=_pad_axis(gla_gate_w1[0], 1, LANE).astype(BF16),
        gw2=_pad_axis(gla_gate_w2[0], 0, LANE).astype(BF16),
        gb=row(gla_gate_b[0]),
        w1=_pad_axis(rwkv_w1[0], 1, LANE).astype(BF16),
        w2=_pad_axis(rwkv_w2[0], 0, LANE).astype(BF16),
        w0=row(rwkv_w0[0]),
        a1=_pad_axis(rwkv_a1[0], 1, LANE).astype(BF16),
        a2=_pad_axis(rwkv_a2[0], 0, LANE).astype(BF16),
        a0=row(rwkv_a0[0]),
        g1=_pad_axis(rwkv_g1[0], 1, g_rank).astype(BF16),
        g2=_pad_axis(rwkv_g2[0], 0, g_rank).astype(BF16),
        gla_norm=row(gla_norm[0]),
        mu_ext=jnp.concatenate([jnp.zeros((1, RWKV_OFF), F32), rwkv_mu_rkv[0].reshape(1, -1)], axis=1),
        k_k=row(rwkv_k_k[0]), k_a=row(rwkv_k_a[0]), r_k=row(rwkv_r_k[0]),
        ln_w=row(rwkv_ln_w[0]), ln_b=row(rwkv_ln_b[0]),
        wo=w_out[0].astype(BF16).reshape(2, D_GLA, D_MODEL),
        ffn_norm=row(ffn_norm[0]),
        router_w=_pad_axis(jnp.concatenate([router_group_w[0], router_expert_w[0]], axis=1), 1,
                           ROUTE_LANES).astype(BF16),
        router_b=_pad_axis(jnp.concatenate([router_group_b[0], router_expert_b[0]]).reshape(1, -1), 1,
                           ROUTE_LANES).astype(F32),
        final_norm=row(final_norm),
    )
    r = jnp.arange(MXU_TILE)
    p['gla_e'] = (r[:, None] // GLA_DK == jnp.arange(2 * GLA_DV)[None, :] // GLA_DV).astype(BF16)
    p['wkv_bd'] = (r[:, None] // RWKV_HEAD == r[None, :] // RWKV_HEAD).astype(BF16)
    return p


def _mixer(x, s_gla, s_wkv, s_shift, p, *, tm, tc):
    nb, seq, _ = x.shape
    m = nb * seq
    x2 = x.reshape(m, D_MODEL)
    multi_seq = tm > seq

    h_last = jnp.zeros((nb, D_MODEL), F32) if s_shift is None else s_shift
    zl = _plain_proj(_pad_axis(h_last, 0, -(-nb // 16) * 16).astype(BF16), p['w_in'], tn=1024)[:nb]
    if multi_seq:
        expand = lambda a: jnp.repeat(a, seq, axis=0)
        hprev, zprev = expand(h_last), expand(zl)
    else:
        hprev, zprev = h_last.reshape(nb, 1, D_MODEL), zl.reshape(nb, 1, D_IN)

    h_bf, la, lw, a_sig, gate, hlast = _norm_proj(x2, hprev, p, seq=seq, tm=tm)
    shift_new = hlast.reshape(nb, seq, D_MODEL)[:, -1] if multi_seq else hlast.reshape(nb, D_MODEL)
    z = _in_proj(h_bf, p['w_in'], p['mu_ext'], zprev, seq=seq, tm=tm if multi_seq else min(seq, 1024), tn=1024)

    z3 = z.reshape(nb, seq, D_IN)
    r3 = lambda a: a.reshape(nb, seq, a.shape[-1])
    if s_gla is None:
        s_gla = jnp.zeros((nb, GLA_HEADS, GLA_DK, GLA_DV), F32)
        s_wkv = jnp.zeros((nb, RWKV_HEADS, RWKV_HEAD, RWKV_HEAD), F32)
    og, gla_new = _gla(z3, r3(la), s_gla, p['gla_e'], p['gla_norm'], tc=tc, chunk=min(GLA_CHUNK, seq),
                       nseq=2 if nb % 2 == 0 else 1)
    wkv_seqs = 4 if nb % 4 == 0 else 1
    orr, wkv_new = _wkv(z3, r3(lw), r3(a_sig), r3(gate), s_wkv, p['wkv_bd'], p, tc=min(tc, 512 // wkv_seqs),
                        chunk=min(WKV_CHUNK, seq), nseq=wkv_seqs)
    return (x2, og.reshape(m, D_GLA), orr.reshape(m, D_RWKV)), (gla_new, wkv_new, shift_new)


def _ffn(rows_p, rows_s, p, wg, wu, wd, *, tm, moe_tile, moe_sub):
    mp = rows_p[0].shape[0]
    x1e = _out_router(*rows_p, *rows_s, p['wo'], p['ffn_norm'], p['router_w'], p['router_b'], tm=tm)
    group_id = x1e[:, X1_WIDTH - 1].astype(jnp.int32)
    tile_group, tile_rows, tile_pos, order = _moe_plan(group_id, moe_tile)
    return _moe(x1e, tile_group, tile_rows, tile_pos, order, wg, wu, wd, p['ffn_norm'], p['final_norm'],
                tile=moe_tile, sub=moe_sub, rows_p=mp)


def kernel(x_prompt, x_sample, state_gla, state_wkv, state_shift, mix_norm, w_in, gla_gate_w1, gla_gate_w2, gla_gate_b, gla_norm, rwkv_mu_rkv, rwkv_mu_wag, rwkv_w0, rwkv_w1, rwkv_w2, rwkv_a0, rwkv_a1, rwkv_a2, rwkv_g1, rwkv_g2, rwkv_k_k, rwkv_k_a, rwkv_r_k, rwkv_ln_w, rwkv_ln_b, w_out, ffn_norm, router_group_w, router_group_b, router_expert_w, router_expert_b, expert_w_gate, expert_w_up, expert_w_down, final_norm):
    p = _prep_params(mix_norm, w_in, gla_gate_w1, gla_gate_w2, gla_gate_b, gla_norm, rwkv_mu_rkv, rwkv_mu_wag,
                     rwkv_w0, rwkv_w1, rwkv_w2, rwkv_a0, rwkv_a1, rwkv_a2, rwkv_g1, rwkv_g2,
                     rwkv_k_k, rwkv_k_a, rwkv_r_k, rwkv_ln_w, rwkv_ln_b, w_out, ffn_norm,
                     router_group_w, router_group_b, router_expert_w, router_expert_b, final_norm)
    wg, wu, wd = expert_w_gate[0], expert_w_up[0], expert_w_down[0]
    nb_p, seq_p, _ = x_prompt.shape
    nb_s, seq_s, _ = x_sample.shape
    rows_p, (gla_p, wkv_p, shift_p) = _mixer(x_prompt, None, None, None, p, tm=256, tc=min(256, seq_p))
    rows_s, (gla_s, wkv_s, shift_s) = _mixer(x_sample, state_gla[0], state_wkv[0], state_shift[0], p,
                                             tm=nb_s * seq_s, tc=seq_s)
    y_p, y_s = _ffn(rows_p, rows_s, p, wg, wu, wd, tm=256, moe_tile=MOE_SUBS_PER_TILE * MOE_SUB, moe_sub=MOE_SUB)
    return (y_p.reshape(nb_p, seq_p, D_MODEL), y_s.reshape(nb_s, seq_s, D_MODEL),
            gla_p[None], wkv_p[None], shift_p[None], gla_s[None], wkv_s[None], shift_s[None])
```

```python
import functools

import jax
import jax.numpy as jnp
from jax import lax
from jax.experimental import pallas as pl
from jax.experimental.pallas import tpu as pltpu

F32 = jnp.float32
BF16 = jnp.bfloat16

D_MODEL = 2048
D_GLA = 1024
D_RWKV = 1024
GLA_HEADS = 4
GLA_DV = 256
GLA_DK = 128
GLA_K = 512
GLA_GATE_TEMP = 16.0
RWKV_HEAD = 64
RWKV_HEADS = 16
N_GROUPS = 8
EXPERTS_PER_GROUP = 8
D_EXPERT = 512
NORM_EPS = 1e-6
GN_EPS = 64e-5
RWKV_OFF = 2 * GLA_K + 2 * D_GLA
D_IN = RWKV_OFF + 3 * D_RWKV

LANE = 128
SUBLANE = 8
MXU_TILE = 256
VMEM_LIMIT = 56 * 1024 * 1024

GLA_CHUNK = 16
WKV_CHUNK = 8
MOE_SUB = 256
MOE_SUBS_PER_TILE = 5
ROUTE_LANES = 128
X1_WIDTH = D_MODEL + ROUTE_LANES
DMA_UNROLL = 8


def _cparams(sem):
    return pltpu.CompilerParams(dimension_semantics=sem, vmem_limit_bytes=VMEM_LIMIT)


def _rms(x, gain):
    return x * lax.rsqrt(jnp.mean(x * x, axis=-1, keepdims=True) + NORM_EPS) * gain


def _softplus(y):
    return jnp.maximum(y, 0.0) + jnp.log1p(jnp.exp(-jnp.abs(y)))


def _dot(a, b):
    return jnp.dot(a.astype(BF16), b.astype(BF16), preferred_element_type=F32)


def _dot_nt(a, b):
    return lax.dot_general(a.astype(BF16), b.astype(BF16), (((1,), (1,)), ((), ())),
                           preferred_element_type=F32)


def _dot_tn(a, b):
    return lax.dot_general(a.astype(BF16), b.astype(BF16), (((0,), (0,)), ((), ())),
                           preferred_element_type=F32)


def _norm_proj_kernel(x_ref, hprev_ref, gain_ref, mu_ref, gw1_ref, gw2_ref, gb_ref,
                      w1_ref, w2_ref, w0_ref, a1_ref, a2_ref, a0_ref, g1_ref, g2_ref,
                      h_ref, la_ref, lw_ref, a_ref, gate_ref, hlast_ref, carry_ref,
                      *, tm, seq, multi_seq):
    i = pl.program_id(0)
    h = _rms(x_ref[...], gain_ref[...])
    row = lax.broadcasted_iota(jnp.int32, (tm, 1), 0)
    rolled = pltpu.roll(h, 1, 0)
    if multi_seq:
        h_shift = jnp.where(row % seq == 0, hprev_ref[...], rolled)
        hlast_ref[...] = h
    else:
        tiles_per_seq = seq // tm
        first = (i % tiles_per_seq) == 0
        prev = jnp.where(first, hprev_ref[0], carry_ref[...])
        h_shift = jnp.where(row == 0, prev, rolled)
        carry_ref[...] = h[tm - 1:tm]
        hlast_ref[0] = h[tm - 1:tm]
    h_ref[...] = h.astype(BF16)

    dx = h_shift - h
    mu = mu_ref[...]
    xw = h + dx * mu[0:1]
    xa = h + dx * mu[1:2]
    xg = h + dx * mu[2:3]

    gl = _dot(_dot(h, gw1_ref[...]), gw2_ref[...]) + gb_ref[...]
    la_ref[...] = -_softplus(-gl) * (1.0 / GLA_GATE_TEMP)

    u = w0_ref[...] + _dot(jnp.tanh(_dot(xw, w1_ref[...])), w2_ref[...])
    w_log = -_softplus(-u) - 0.5
    lw_ref[...] = -jnp.exp(w_log)

    a_ref[...] = jax.nn.sigmoid(a0_ref[...] + _dot(_dot(xa, a1_ref[...]), a2_ref[...]))
    gate_ref[...] = _dot(jax.nn.sigmoid(_dot(xg, g1_ref[...])), g2_ref[...])


def _pad_axis(w, axis, to):
    pad = [(0, 0)] * w.ndim
    pad[axis] = (0, to - w.shape[axis])
    return jnp.pad(w, pad)


def _norm_proj(x2, hprev, p, *, seq, tm):
    m = x2.shape[0]
    multi_seq = tm > seq
    nb = m // seq
    full = lambda shape: pl.BlockSpec(shape, lambda i: (0,) * len(shape))
    rows = lambda w: pl.BlockSpec((tm, w), lambda i: (i, 0))
    if multi_seq:
        assert tm == m
        hprev_spec = rows(D_MODEL)
        hlast_shape = jax.ShapeDtypeStruct((m, D_MODEL), F32)
        hlast_spec = rows(D_MODEL)
    else:
        assert seq % tm == 0
        tps = seq // tm
        hprev_spec = pl.BlockSpec((1, 1, D_MODEL), lambda i: (i // tps, 0, 0))
        hlast_shape = jax.ShapeDtypeStruct((nb, 1, D_MODEL), F32)
        hlast_spec = pl.BlockSpec((1, 1, D_MODEL), lambda i: (i // tps, 0, 0))
    weights = [p['gw1'], p['gw2'], p['gb'], p['w1'], p['w2'], p['w0'], p['a1'], p['a2'], p['a0'],
               p['g1'], p['g2']]
    return pl.pallas_call(
        functools.partial(_norm_proj_kernel, tm=tm, seq=seq, multi_seq=multi_seq),
        grid=(m // tm,),
        in_specs=[rows(D_MODEL), hprev_spec, full((1, D_MODEL)), full((3, D_MODEL))]
                 + [full(w.shape) for w in weights],
        out_specs=[rows(D_MODEL), rows(GLA_K), rows(D_RWKV), rows(D_RWKV), rows(D_RWKV), hlast_spec],
        out_shape=[jax.ShapeDtypeStruct((m, D_MODEL), BF16),
                   jax.ShapeDtypeStruct((m, GLA_K), F32),
                   jax.ShapeDtypeStruct((m, D_RWKV), F32),
                   jax.ShapeDtypeStruct((m, D_RWKV), F32),
                   jax.ShapeDtypeStruct((m, D_RWKV), F32),
                   hlast_shape],
        scratch_shapes=[pltpu.VMEM((1, D_MODEL), F32)],
        compiler_params=_cparams(("arbitrary",)),
        name="norm_proj",
    )(x2, hprev, p['mix_norm'], p['mu_wag'], *weights)


def _in_proj_kernel(h_ref, w_ref, mu_ref, zprev_ref, z_ref, carry_ref, *, tm, seq, multi_seq):
    i = pl.program_id(1)
    z = jnp.dot(h_ref[...], w_ref[...], preferred_element_type=F32)
    row = lax.broadcasted_iota(jnp.int32, (tm, 1), 0)
    rolled = pltpu.roll(z, 1, 0)
    if multi_seq:
        z_prev = jnp.where(row % seq == 0, zprev_ref[...], rolled)
    else:
        tiles_per_seq = seq // tm
        first = (i % tiles_per_seq) == 0
        prev = jnp.where(first, zprev_ref[0], carry_ref[...])
        z_prev = jnp.where(row == 0, prev, rolled)
        carry_ref[...] = z[tm - 1:tm]
    z_ref[...] = z + mu_ref[...] * (z_prev - z)


def _in_proj(h_bf, w_in, mu_ext, zprev, *, seq, tm, tn):
    m = h_bf.shape[0]
    multi_seq = tm > seq
    if multi_seq:
        zprev_spec = pl.BlockSpec((tm, tn), lambda j, i: (i, j))
    else:
        tps = seq // tm
        zprev_spec = pl.BlockSpec((1, 1, tn), lambda j, i: (i // tps, 0, j))
    return pl.pallas_call(
        functools.partial(_in_proj_kernel, tm=tm, seq=seq, multi_seq=multi_seq),
        grid=(D_IN // tn, m // tm),
        in_specs=[pl.BlockSpec((tm, D_MODEL), lambda j, i: (i, 0)),
                  pl.BlockSpec((D_MODEL, tn), lambda j, i: (0, j)),
                  pl.BlockSpec((1, tn), lambda j, i: (0, j)),
                  zprev_spec],
        out_specs=pl.BlockSpec((tm, tn), lambda j, i: (i, j)),
        out_shape=jax.ShapeDtypeStruct((m, D_IN), F32),
        scratch_shapes=[pltpu.VMEM((1, tn), F32)],
        compiler_params=_cparams(("arbitrary", "arbitrary")),
        name="in_proj",
    )(h_bf, w_in, mu_ext, zprev)


def _plain_proj_kernel(h_ref, w_ref, z_ref):
    z_ref[...] = jnp.dot(h_ref[...], w_ref[...], preferred_element_type=F32)


def _plain_proj(h_bf, w_in, *, tn):
    m = h_bf.shape[0]
    return pl.pallas_call(
        _plain_proj_kernel,
        grid=(D_IN // tn,),
        in_specs=[pl.BlockSpec((m, D_MODEL), lambda j: (0, 0)),
                  pl.BlockSpec((D_MODEL, tn), lambda j: (0, j))],
        out_specs=pl.BlockSpec((m, tn), lambda j: (0, j)),
        out_shape=jax.ShapeDtypeStruct((m, D_IN), F32),
        compiler_params=_cparams(("arbitrary",)),
        name="prev_row_proj",
    )(h_bf, w_in)


def _gla_kernel(q_ref, k_ref, v_ref, g_ref, la_ref, s0_ref, e_ref, gn_ref,
                o_ref, s_ref, st_ref, obuf_ref, *, tc, chunk, nseq):
    tb = pl.program_id(1)
    c = chunk

    @pl.when(tb == 0)
    def _():
        for q in range(nseq):
            for hh in range(GLA_HEADS):
                st_ref[q, :, hh * GLA_DK:(hh + 1) * GLA_DK] = s0_ref[q, hh].T

    row = lax.broadcasted_iota(jnp.int32, (c, 1), 0)
    e_mat = e_ref[...]
    neg = jnp.float32(-1e30)

    def one_chunk(q, r0):
        qq = q_ref[q, pl.ds(r0, c), :] * (GLA_DK ** -0.5)
        k = k_ref[q, pl.ds(r0, c), :]
        v = v_ref[q, pl.ds(r0, c), :]
        la = la_ref[q, pl.ds(r0, c), :]
        b = jnp.zeros_like(la)
        for s in range(c):
            b = b + jnp.where(row >= s, la[s:s + 1], 0.0)
        b_last = b[c - 1:c]

        parts = []
        for s in range(c):
            dec = jnp.exp(jnp.where(row >= s, b - b[s:s + 1], neg))
            parts.append((qq * k[s:s + 1] * dec).astype(BF16))
        pcat = jnp.concatenate(parts, axis=0)
        sc = jnp.concatenate(
            [jnp.dot(pcat[:, hp * MXU_TILE:(hp + 1) * MXU_TILE], e_mat, preferred_element_type=F32)
             for hp in range(GLA_K // MXU_TILE)], axis=1)
        o = jnp.zeros((c, D_GLA), F32)
        for s in range(c):
            o = o + sc[s * c:(s + 1) * c] * v[s:s + 1]

        qd = qq * jnp.exp(b)
        kd = k * jnp.exp(b_last - b)
        a_last = jnp.exp(b_last)
        o_heads = []
        for hh in range(GLA_HEADS):
            dk = slice(hh * GLA_DK, (hh + 1) * GLA_DK)
            dv = slice(hh * GLA_DV, (hh + 1) * GLA_DV)
            st = st_ref[q, :, dk]
            oh = o[:, dv] + _dot_nt(qd[:, dk], st)
            st_ref[q, :, dk] = a_last[:, dk] * st + _dot_tn(v[:, dv], kd[:, dk])
            oh = oh * lax.rsqrt(jnp.mean(oh * oh, axis=-1, keepdims=True) + NORM_EPS)
            o_heads.append(oh)
        og = jnp.concatenate(o_heads, axis=1)
        g = g_ref[q, pl.ds(r0, c), :]
        obuf_ref[q, pl.ds(r0, c), :] = og * gn_ref[...] * (g * jax.nn.sigmoid(g))

    def body(ci, carry):
        r0 = pl.multiple_of(ci * c, c)
        for q in range(nseq):
            one_chunk(q, r0)
        return carry

    lax.fori_loop(0, tc // c, body, 0)
    o_ref[...] = obuf_ref[...].astype(BF16)

    @pl.when(tb == pl.num_programs(1) - 1)
    def _():
        for q in range(nseq):
            for hh in range(GLA_HEADS):
                s_ref[q, hh] = st_ref[q, :, hh * GLA_DK:(hh + 1) * GLA_DK].T


def _gla(z3, la3, s0, e_mat, gla_norm, *, tc, chunk, nseq):
    nb, seq, _ = z3.shape
    blk = lambda w, col: pl.BlockSpec((nseq, tc, w), lambda b, t: (b, t, col))
    st_spec = pl.BlockSpec((nseq, GLA_HEADS, GLA_DK, GLA_DV), lambda b, t: (b, 0, 0, 0))
    return pl.pallas_call(
        functools.partial(_gla_kernel, tc=tc, chunk=chunk, nseq=nseq),
        grid=(nb // nseq, seq // tc),
        in_specs=[blk(GLA_K, 0), blk(GLA_K, 1), blk(D_GLA, 1), blk(D_GLA, 2), blk(GLA_K, 0),
                  st_spec,
                  pl.BlockSpec(e_mat.shape, lambda b, t: (0, 0)),
                  pl.BlockSpec((1, D_GLA), lambda b, t: (0, 0))],
        out_specs=[blk(D_GLA, 0), st_spec],
        out_shape=[jax.ShapeDtypeStruct((nb, seq, D_GLA), BF16),
                   jax.ShapeDtypeStruct((nb, GLA_HEADS, GLA_DK, GLA_DV), F32)],
        scratch_shapes=[pltpu.VMEM((nseq, GLA_DV, GLA_K), F32), pltpu.VMEM((nseq, tc, D_GLA), F32)],
        compiler_params=_cparams(("arbitrary", "arbitrary")),
        name="gla",
    )(z3, z3, z3, z3, la3, s0, e_mat, gla_norm)


N_LANE_TILES = D_RWKV // MXU_TILE
HEADS_PER_TILE = MXU_TILE // RWKV_HEAD


def _stack_tiles(x):
    return jnp.concatenate([x[:, t * MXU_TILE:(t + 1) * MXU_TILE] for t in range(N_LANE_TILES)], axis=0)


def _unstack_tiles(y, r):
    return jnp.concatenate([y[t * r:(t + 1) * r] for t in range(N_LANE_TILES)], axis=1)


def _seg_sums(xs, bd):
    c = xs[0].shape[0]
    st = jnp.concatenate([_stack_tiles(x) for x in xs], axis=0)
    hi = st.astype(BF16)
    lo = (st - hi.astype(F32)).astype(BF16)
    res = jnp.dot(jnp.concatenate([hi, lo], axis=0), bd, preferred_element_type=F32)
    half = st.shape[0]
    res = res[:half] + res[half:]
    rows = N_LANE_TILES * c
    return [_unstack_tiles(res[n * rows:(n + 1) * rows], c) for n in range(len(xs))]


def _cumsum_rows(x, row, c):
    if c == SUBLANE:
        step = 1
        while step < c:
            x = x + jnp.where(row >= step, pltpu.roll(x, step, 0), 0.0)
            step *= 2
        return x
    out = jnp.zeros_like(x)
    for s in range(c):
        out = out + jnp.where(row >= s, x[s:s + 1], 0.0)
    return out


def _wkv_kernel(r_ref, k_ref, v_ref, lw_ref, a_ref, gate_ref, s0_ref, bd_ref,
                kk_ref, ka_ref, rk_ref, lnw_ref, lnb_ref,
                o_ref, s_ref, st_ref, obuf_ref, *, tc, chunk, nseq):
    tb = pl.program_id(1)
    c = chunk
    cc = c * c

    @pl.when(tb == 0)
    def _():
        for q in range(nseq):
            st_ref[q] = jnp.concatenate([s0_ref[q, hh] for hh in range(RWKV_HEADS)], axis=1)

    row = lax.broadcasted_iota(jnp.int32, (c, 1), 0)
    bd = bd_ref[...]
    lane_head = lax.broadcasted_iota(jnp.int32, (RWKV_HEAD, MXU_TILE), 1) // RWKV_HEAD
    neg = jnp.float32(-1e30)

    def one_chunk(q, r0):
        r = r_ref[q, pl.ds(r0, c), :]
        k = k_ref[q, pl.ds(r0, c), :]
        v = v_ref[q, pl.ds(r0, c), :]
        lw = lw_ref[q, pl.ds(r0, c), :]
        a_sig = a_ref[q, pl.ds(r0, c), :]

        kk = k * kk_ref[...]
        k_eff = k * (1.0 + (a_sig - 1.0) * ka_ref[...])
        kk_ss, bonus_dot = _seg_sums([kk * kk, r * k_eff * rk_ref[...]], bd)
        kk = kk / jnp.maximum(jnp.sqrt(kk_ss), 1e-12)
        a_vec = -kk
        b_vec = kk * a_sig

        lb = _cumsum_rows(lw, row, c)
        lbp = lb - lw
        lb_last = lb[c - 1:c]
        kinds = ([], [], [], [])
        for s in range(c):
            lbs = lb[s:s + 1]
            bs = b_vec[s:s + 1]
            ks = k_eff[s:s + 1]
            a_d = a_vec * jnp.exp(jnp.where(row > s, lbp - lbs, neg))
            r_d = r * jnp.exp(jnp.where(row >= s, lb - lbs, neg))
            kinds[0].append(a_d * bs)
            kinds[1].append(a_d * ks)
            kinds[2].append(r_d * bs)
            kinds[3].append(r_d * ks)
        prod = jnp.concatenate(
            [p[:, t * MXU_TILE:(t + 1) * MXU_TILE]
             for t in range(N_LANE_TILES) for kind in kinds for p in kind], axis=0)
        coef_t = jnp.dot(prod.astype(BF16), bd, preferred_element_type=F32)

        def coef(kind, s):
            return jnp.concatenate(
                [coef_t[(t * 4 + kind) * cc + s * c:(t * 4 + kind) * cc + (s + 1) * c]
                 for t in range(N_LANE_TILES)], axis=1)

        lhs = jnp.concatenate([a_vec * jnp.exp(lbp), r * jnp.exp(lb)], axis=0).astype(BF16)
        x0 = []
        for t in range(N_LANE_TILES):
            ln = slice(t * MXU_TILE, (t + 1) * MXU_TILE)
            st_t = st_ref[q, :, ln].astype(BF16)
            w_t = jnp.concatenate([st_t] * HEADS_PER_TILE, axis=0) * bd
            x0.append(lax.dot_general(lhs[:, ln], w_t, (((1,), (1,)), ((), ())),
                                      preferred_element_type=F32))
        x0 = jnp.concatenate(x0, axis=1)
        u = x0[:c]
        o = x0[c:]

        for s in range(c):
            u = u + coef(1, s) * v[s:s + 1]
        for s in range(c):
            u = u + coef(0, s) * u[s:s + 1]
        for s in range(c):
            o = o + coef(2, s) * u[s:s + 1] + coef(3, s) * v[s:s + 1]

        tail = jnp.exp(lb_last - lb)
        uv = jnp.concatenate([u, v], axis=0).astype(BF16)
        bk = jnp.concatenate([b_vec * tail, k_eff * tail], axis=0).astype(BF16)
        w_last = jnp.exp(lb_last)
        for t in range(N_LANE_TILES):
            ln = slice(t * MXU_TILE, (t + 1) * MXU_TILE)
            zt = _dot_tn(uv[:, ln], bk[:, ln])
            upd = zt[(HEADS_PER_TILE - 1) * RWKV_HEAD:]
            for hh in range(HEADS_PER_TILE - 2, -1, -1):
                upd = jnp.where(lane_head == hh, zt[hh * RWKV_HEAD:(hh + 1) * RWKV_HEAD], upd)
            st_ref[q, :, ln] = w_last[:, ln] * st_ref[q, :, ln] + upd

        inv_n = 1.0 / RWKV_HEAD
        mu = _seg_sums([o], bd)[0] * inv_n
        dev = o - mu
        var = _seg_sums([dev * dev], bd)[0] * inv_n
        yn = dev * lax.rsqrt(var + GN_EPS) * lnw_ref[...] + lnb_ref[...]
        obuf_ref[q, pl.ds(r0, c), :] = (yn + bonus_dot * v) * gate_ref[q, pl.ds(r0, c), :]

    def body(ci, carry):
        r0 = pl.multiple_of(ci * c, c)
        for q in range(nseq):
            one_chunk(q, r0)
        return carry

    lax.fori_loop(0, tc // c, body, 0)
    o_ref[...] = obuf_ref[...].astype(BF16)

    @pl.when(tb == pl.num_programs(1) - 1)
    def _():
        for q in range(nseq):
            for hh in range(RWKV_HEADS):
                s_ref[q, hh] = st_ref[q, :, hh * RWKV_HEAD:(hh + 1) * RWKV_HEAD]


def _wkv(z3, lw3, a3, gate3, s0, bd, p, *, tc, chunk, nseq):
    nb, seq, _ = z3.shape
    col0 = RWKV_OFF // D_RWKV
    blk = lambda col: pl.BlockSpec((nseq, tc, D_RWKV), lambda b, t: (b, t, col))
    st_spec = pl.BlockSpec((nseq, RWKV_HEADS, RWKV_HEAD, RWKV_HEAD), lambda b, t: (b, 0, 0, 0))
    vec = pl.BlockSpec((1, D_RWKV), lambda b, t: (0, 0))
    return pl.pallas_call(
        functools.partial(_wkv_kernel, tc=tc, chunk=chunk, nseq=nseq),
        grid=(nb // nseq, seq // tc),
        in_specs=[blk(col0), blk(col0 + 1), blk(col0 + 2), blk(0), blk(0), blk(0), st_spec,
                  pl.BlockSpec(bd.shape, lambda b, t: (0, 0)), vec, vec, vec, vec, vec],
        out_specs=[blk(0), st_spec],
        out_shape=[jax.ShapeDtypeStruct((nb, seq, D_RWKV), BF16),
                   jax.ShapeDtypeStruct((nb, RWKV_HEADS, RWKV_HEAD, RWKV_HEAD), F32)],
        scratch_shapes=[pltpu.VMEM((nseq, RWKV_HEAD, D_RWKV), F32),
                        pltpu.VMEM((nseq, tc, D_RWKV), F32)],
        compiler_params=_cparams(("arbitrary", "arbitrary")),
        name="wkv",
    )(z3, z3, z3, lw3, a3, gate3, s0, bd, p['k_k'], p['k_a'], p['r_k'], p['ln_w'], p['ln_b'])


def _out_router_kernel(xp_ref, gp_ref, rp_ref, xs_ref, gs_ref, rs_ref, wo_ref, fg_ref, rw_ref, rb_ref,
                       out_ref, *, tm, tiles_p):
    is_p = pl.program_id(0) < tiles_p
    x = jnp.where(is_p, xp_ref[...], xs_ref[...])
    og = jnp.where(is_p, gp_ref[...], gs_ref[...])
    orr = jnp.where(is_p, rp_ref[...], rs_ref[...])
    mix = (jnp.dot(og, wo_ref[0], preferred_element_type=F32)
           + jnp.dot(orr, wo_ref[1], preferred_element_type=F32))
    x1 = x + mix
    out_ref[:, :D_MODEL] = x1
    h2 = _rms(x1, fg_ref[...])
    logits = _dot(h2, rw_ref[...]) + rb_ref[...]
    lane = lax.broadcasted_iota(jnp.int32, (tm, ROUTE_LANES), 1)
    neg = jnp.float32(-jnp.inf)
    big = jnp.int32(1 << 20)

    def first_argmax(vals):
        mx = jnp.max(vals, axis=-1, keepdims=True)
        idx = jnp.min(jnp.where(vals == mx, lane, big), axis=-1, keepdims=True)
        return mx, idx

    lg = jnp.where(lane < N_GROUPS, logits, neg)
    g_max, g_idx = first_argmax(lg)
    p_group = 1.0 / jnp.sum(jnp.exp(lg - g_max), axis=-1, keepdims=True)
    lo = N_GROUPS + g_idx * EXPERTS_PER_GROUP
    le = jnp.where((lane >= lo) & (lane < lo + EXPERTS_PER_GROUP), logits, neg)
    m1, i1 = first_argmax(le)
    m2, i2 = first_argmax(jnp.where(lane == i1, neg, le))
    e2 = jnp.exp(m2 - m1)
    w1 = 1.0 / (1.0 + e2)
    w2 = e2 / (1.0 + e2)
    route = jnp.where(lane == i1, w1, jnp.where(lane == i2, w2, 0.0)) * p_group
    out_ref[:, D_MODEL:] = jnp.where(lane == ROUTE_LANES - 1, g_idx.astype(F32), route)


def _out_router(xp, gp, rp, xs, gs, rs, wo, ffn_norm, rw, rb, *, tm):
    mp, ms = xp.shape[0], xs.shape[0]
    tiles_p, tiles_s = mp // tm, ms // tm
    rows_p = lambda w: pl.BlockSpec((tm, w), lambda i: (jnp.minimum(i, tiles_p - 1), 0))
    rows_s = lambda w: pl.BlockSpec((tm, w), lambda i: (jnp.maximum(i - tiles_p, 0), 0))
    full = lambda shape: pl.BlockSpec(shape, lambda i: (0,) * len(shape))
    return pl.pallas_call(
        functools.partial(_out_router_kernel, tm=tm, tiles_p=tiles_p),
        grid=(tiles_p + tiles_s,),
        in_specs=[rows_p(D_MODEL), rows_p(D_GLA), rows_p(D_RWKV),
                  rows_s(D_MODEL), rows_s(D_GLA), rows_s(D_RWKV),
                  full(wo.shape), full((1, D_MODEL)), full(rw.shape), full((1, ROUTE_LANES))],
        out_specs=pl.BlockSpec((tm, X1_WIDTH), lambda i: (i, 0)),
        out_shape=jax.ShapeDtypeStruct((mp + ms, X1_WIDTH), F32),
        compiler_params=_cparams(("arbitrary",)),
        name="out_router",
    )(xp, gp, rp, xs, gs, rs, wo, ffn_norm, rw, rb)


def _moe_kernel(tile_group_ref, tile_rows_ref, tile_pos_ref, order_ref,
                x1_hbm, wg_ref, wu_ref, wd_ref, fg_ref, fn_ref, yp_hbm, ys_hbm,
                acc, hbuf, wgb, wub, wdb, gsem, ssem, *, sub, rows_p, n_tokens):
    j = pl.program_id(0)
    e = pl.program_id(1)
    nrows = tile_rows_ref[j]
    pos0 = tile_pos_ref[j]
    half = sub // 2
    rem = nrows % sub
    has_small = (rem > 0) & (rem <= half)
    n_big = nrows // sub + jnp.where(rem > half, 1, 0)
    rows_used = n_big * sub + jnp.where(has_small, half, 0)

    def for_blocks(fn):
        def big(sb, cy):
            fn(pl.ds(pl.multiple_of(sb * sub, sub), sub), sub)
            return cy
        lax.fori_loop(0, n_big, big, 0)

        @pl.when(has_small)
        def _():
            fn(pl.ds(pl.multiple_of(n_big * sub, half), half), half)

    def token(rr):
        return order_ref[jnp.minimum(pos0 + rr, n_tokens - 1)]

    def gather_copy(rr):
        return pltpu.make_async_copy(x1_hbm.at[pl.ds(token(rr), 1)], acc.at[pl.ds(rr, 1)], gsem)

    def scatter_start(rr):
        tok = token(rr)
        row = acc.at[pl.ds(rr, 1), pl.ds(0, D_MODEL)]

        @pl.when(tok < rows_p)
        def _():
            pltpu.make_async_copy(row, yp_hbm.at[pl.ds(tok, 1)], ssem).start()

        @pl.when(tok >= rows_p)
        def _():
            pltpu.make_async_copy(row, ys_hbm.at[pl.ds(tok - rows_p, 1)], ssem).start()

    def scatter_wait(rr):
        pltpu.make_async_copy(acc.at[pl.ds(rr, 1), pl.ds(0, D_MODEL)], yp_hbm.at[pl.ds(0, 1)], ssem).wait()

    def for_rows(n_blocks, fn):
        def blk(bi, cy):
            for q in range(DMA_UNROLL):
                fn(bi * DMA_UNROLL + q)
            return cy
        lax.fori_loop(0, n_blocks, blk, 0)

    @pl.when((e == 0) & (nrows > 0))
    def _():
        n_blk = rows_used // DMA_UNROLL
        for_rows(n_blk, lambda rr: gather_copy(rr).start())
        for_rows(n_blk, lambda rr: gather_copy(rr).wait())

        def norm(rs, size):
            hbuf[rs, :] = _rms(acc[rs, :D_MODEL], fg_ref[...]).astype(BF16)
        for_blocks(norm)

    @pl.when(nrows > 0)
    def _():
        g = tile_group_ref[j]
        wgb[...] = wg_ref[0, 0].astype(BF16)
        wub[...] = wu_ref[0, 0].astype(BF16)
        wdb[...] = wd_ref[0, 0].astype(BF16)

        def expert(rs, size):
            lane = lax.broadcasted_iota(jnp.int32, (size, ROUTE_LANES), 1)
            sel = lane == N_GROUPS + g * EXPERTS_PER_GROUP + e
            wcol = jnp.sum(jnp.where(sel, acc[rs, D_MODEL:], 0.0), axis=-1, keepdims=True)
            h2 = hbuf[rs, :]
            gate = jnp.dot(h2, wgb[...], preferred_element_type=F32)
            up = jnp.dot(h2, wub[...], preferred_element_type=F32)
            hid = (gate * jax.nn.sigmoid(gate)) * up * wcol
            acc[rs, :D_MODEL] += jnp.dot(hid.astype(BF16), wdb[...], preferred_element_type=F32)
        for_blocks(expert)

    @pl.when((e == EXPERTS_PER_GROUP - 1) & (nrows > 0))
    def _():
        def final(rs, size):
            acc[rs, :D_MODEL] = _rms(acc[rs, :D_MODEL], fn_ref[...])
        for_blocks(final)

        n_full = nrows // DMA_UNROLL
        for_rows(n_full, scatter_start)

        def rest_start(rr, cy):
            scatter_start(rr)
            return cy
        lax.fori_loop(n_full * DMA_UNROLL, nrows, rest_start, 0)
        for_rows(n_full, scatter_wait)

        def rest_wait(rr, cy):
            scatter_wait(rr)
            return cy
        lax.fori_loop(n_full * DMA_UNROLL, nrows, rest_wait, 0)


def _moe(x1e, tile_group, tile_rows, tile_pos, order, wg, wu, wd, ffn_norm, final_norm, *, tile, sub, rows_p):
    m = x1e.shape[0]
    n_tiles = tile_group.shape[0]

    def w_idx(j, e, tg, tr, tp, od):
        return (tg[j], jnp.where(tr[j] > 0, e, EXPERTS_PER_GROUP - 1), 0, 0)

    vec = pl.BlockSpec((1, D_MODEL), lambda j, e, tg, tr, tp, od: (0, 0))
    grid_spec = pltpu.PrefetchScalarGridSpec(
        num_scalar_prefetch=4,
        grid=(n_tiles, EXPERTS_PER_GROUP),
        in_specs=[pl.BlockSpec(memory_space=pl.ANY),
                  pl.BlockSpec((1, 1, D_MODEL, D_EXPERT), w_idx),
                  pl.BlockSpec((1, 1, D_MODEL, D_EXPERT), w_idx),
                  pl.BlockSpec((1, 1, D_EXPERT, D_MODEL), w_idx),
                  vec, vec],
        out_specs=[pl.BlockSpec(memory_space=pl.ANY), pl.BlockSpec(memory_space=pl.ANY)],
        scratch_shapes=[pltpu.VMEM((tile, X1_WIDTH), F32),
                        pltpu.VMEM((tile, D_MODEL), BF16),
                        pltpu.VMEM((D_MODEL, D_EXPERT), BF16),
                        pltpu.VMEM((D_MODEL, D_EXPERT), BF16),
                        pltpu.VMEM((D_EXPERT, D_MODEL), BF16),
                        pltpu.SemaphoreType.DMA(()),
                        pltpu.SemaphoreType.DMA(())],
    )
    return pl.pallas_call(
        functools.partial(_moe_kernel, sub=sub, rows_p=rows_p, n_tokens=m),
        grid_spec=grid_spec,
        out_shape=[jax.ShapeDtypeStruct((rows_p, D_MODEL), F32),
                   jax.ShapeDtypeStruct((m - rows_p, D_MODEL), F32)],
        compiler_params=_cparams(("arbitrary", "arbitrary")),
        name="moe",
    )(tile_group, tile_rows, tile_pos, order, x1e, wg, wu, wd, ffn_norm, final_norm)


def _moe_plan(group_id, tile):
    m = group_id.shape[0]
    n_tiles = m // tile + N_GROUPS
    order = jnp.argsort(group_id, stable=True).astype(jnp.int32)
    counts = jnp.sum(group_id[:, None] == jnp.arange(N_GROUPS, dtype=jnp.int32)[None, :], axis=0).astype(jnp.int32)
    tiles_per = (counts + tile - 1) // tile
    tile_end = jnp.cumsum(tiles_per)
    tile_start = tile_end - tiles_per
    row_start = jnp.cumsum(counts) - counts
    t = jnp.arange(n_tiles, dtype=jnp.int32)
    used = t < tile_end[-1]
    grp = jnp.minimum(jnp.sum(t[:, None] >= tile_end[None, :], axis=1), N_GROUPS - 1).astype(jnp.int32)
    last_group = jnp.max(jnp.where(counts > 0, jnp.arange(N_GROUPS, dtype=jnp.int32), 0))
    tile_group = jnp.where(used, grp, last_group).astype(jnp.int32)
    local = (t - tile_start[grp]) * tile
    tile_rows = jnp.where(used, jnp.clip(counts[grp] - local, 0, tile), 0).astype(jnp.int32)
    tile_pos = jnp.where(used, row_start[grp] + local, 0).astype(jnp.int32)
    return tile_group, tile_rows, tile_pos, order


def _prep_params(mix_norm, w_in, gla_gate_w1, gla_gate_w2, gla_gate_b, gla_norm, rwkv_mu_rkv, rwkv_mu_wag,
                 rwkv_w0, rwkv_w1, rwkv_w2, rwkv_a0, rwkv_a1, rwkv_a2, rwkv_g1, rwkv_g2,
                 rwkv_k_k, rwkv_k_a, rwkv_r_k, rwkv_ln_w, rwkv_ln_b, w_out, ffn_norm,
                 router_group_w, router_group_b, router_expert_w, router_expert_b, final_norm):
    row = lambda v: v.reshape(1, -1).astype(F32)
    g_rank = 2 * LANE
    p = dict(
        mix_norm=row(mix_norm[0]),
        mu_wag=rwkv_mu_wag[0],
        w_in=w_in[0].astype(BF16),
        gw1=_pad_axis(gla_gate_w1[0], 1, LANE).astype(BF16),
        gw2=_pad_axis(gla_gate_w2[0], 0, LANE).astype(BF16),
        gb=row(gla_gate_b[0]),
        w1=_pad_axis(rwkv_w1[0], 1, LANE).astype(BF16),
        w2=_pad_axis(rwkv_w2[0], 0, LANE).astype(BF16),
        w0=row(rwkv_w0[0]),
        a1=_pad_axis(rwkv_a1[0], 1, LANE).astype(BF16),
        a2=_pad_axis(rwkv_a2[0], 0, LANE).astype(BF16),
        a0=row(rwkv_a0[0]),
        g1=_pad_axis(rwkv_g1[0], 1, g_rank).astype(BF16),
        g2=_pad_axis(rwkv_g2[0], 0, g_rank).astype(BF16),
        gla_norm=row(gla_norm[0]),
        mu_ext=jnp.concatenate([jnp.zeros((1, RWKV_OFF), F32), rwkv_mu_rkv[0].reshape(1, -1)], axis=1),
        k_k=row(rwkv_k_k[0]), k_a=row(rwkv_k_a[0]), r_k=row(rwkv_r_k[0]),
        ln_w=row(rwkv_ln_w[0]), ln_b=row(rwkv_ln_b[0]),
        wo=w_out[0].astype(BF16).reshape(2, D_GLA, D_MODEL),
        ffn_norm=row(ffn_norm[0]),
        router_w=_pad_axis(jnp.concatenate([router_group_w[0], router_expert_w[0]], axis=1), 1,
                           ROUTE_LANES).astype(BF16),
        router_b=_pad_axis(jnp.concatenate([router_group_b[0], router_expert_b[0]]).reshape(1, -1), 1,
                           ROUTE_LANES).astype(F32),
        final_norm=row(final_norm),
    )
    r = jnp.arange(MXU_TILE)
    p['gla_e'] = (r[:, None] // GLA_DK == jnp.arange(2 * GLA_DV)[None, :] // GLA_DV).astype(BF16)
    p['wkv_bd'] = (r[:, None] // RWKV_HEAD == r[None, :] // RWKV_HEAD).astype(BF16)
    return p


def _mixer(x, s_gla, s_wkv, s_shift, p, *, tm, tc):
    nb, seq, _ = x.shape
    m = nb * seq
    x2 = x.reshape(m, D_MODEL)
    multi_seq = tm > seq

    h_last = jnp.zeros((nb, D_MODEL), F32) if s_shift is None else s_shift
    zl = _plain_proj(_pad_axis(h_last, 0, -(-nb // 16) * 16).astype(BF16), p['w_in'], tn=1024)[:nb]
    if multi_seq:
        expand = lambda a: jnp.repeat(a, seq, axis=0)
        hprev, zprev = expand(h_last), expand(zl)
    else:
        hprev, zprev = h_last.reshape(nb, 1, D_MODEL), zl.reshape(nb, 1, D_IN)

    h_bf, la, lw, a_sig, gate, hlast = _norm_proj(x2, hprev, p, seq=seq, tm=tm)
    shift_new = hlast.reshape(nb, seq, D_MODEL)[:, -1] if multi_seq else hlast.reshape(nb, D_MODEL)
    z = _in_proj(h_bf, p['w_in'], p['mu_ext'], zprev, seq=seq, tm=tm if multi_seq else min(seq, 1024), tn=1024)

    z3 = z.reshape(nb, seq, D_IN)
    r3 = lambda a: a.reshape(nb, seq, a.shape[-1])
    if s_gla is None:
        s_gla = jnp.zeros((nb, GLA_HEADS, GLA_DK, GLA_DV), F32)
        s_wkv = jnp.zeros((nb, RWKV_HEADS, RWKV_HEAD, RWKV_HEAD), F32)
    seqs = 4 if nb % 4 == 0 else 1
    tc_seqs = min(tc, 512 // seqs)
    og, gla_new = _gla(z3, r3(la), s_gla, p['gla_e'], p['gla_norm'], tc=tc_seqs, chunk=min(GLA_CHUNK, seq),
                       nseq=seqs)
    orr, wkv_new = _wkv(z3, r3(lw), r3(a_sig), r3(gate), s_wkv, p['wkv_bd'], p, tc=tc_seqs,
                        chunk=min(WKV_CHUNK, seq), nseq=seqs)
    return (x2, og.reshape(m, D_GLA), orr.reshape(m, D_RWKV)), (gla_new, wkv_new, shift_new)


def _ffn(rows_p, rows_s, p, wg, wu, wd, *, tm, moe_tile, moe_sub):
    mp = rows_p[0].shape[0]
    x1e = _out_router(*rows_p, *rows_s, p['wo'], p['ffn_norm'], p['router_w'], p['router_b'], tm=tm)
    group_id = x1e[:, X1_WIDTH - 1].astype(jnp.int32)
    tile_group, tile_rows, tile_pos, order = _moe_plan(group_id, moe_tile)
    return _moe(x1e, tile_group, tile_rows, tile_pos, order, wg, wu, wd, p['ffn_norm'], p['final_norm'],
                tile=moe_tile, sub=moe_sub, rows_p=mp)


def kernel(x_prompt, x_sample, state_gla, state_wkv, state_shift, mix_norm, w_in, gla_gate_w1, gla_gate_w2, gla_gate_b, gla_norm, rwkv_mu_rkv, rwkv_mu_wag, rwkv_w0, rwkv_w1, rwkv_w2, rwkv_a0, rwkv_a1, rwkv_a2, rwkv_g1, rwkv_g2, rwkv_k_k, rwkv_k_a, rwkv_r_k, rwkv_ln_w, rwkv_ln_b, w_out, ffn_norm, router_group_w, router_group_b, router_expert_w, router_expert_b, expert_w_gate, expert_w_up, expert_w_down, final_norm):
    p = _prep_params(mix_norm, w_in, gla_gate_w1, gla_gate_w2, gla_gate_b, gla_norm, rwkv_mu_rkv, rwkv_mu_wag,
                     rwkv_w0, rwkv_w1, rwkv_w2, rwkv_a0, rwkv_a1, rwkv_a2, rwkv_g1, rwkv_g2,
                     rwkv_k_k, rwkv_k_a, rwkv_r_k, rwkv_ln_w, rwkv_ln_b, w_out, ffn_norm,
                     router_group_w, router_group_b, router_expert_w, router_expert_b, final_norm)
    wg, wu, wd = expert_w_gate[0], expert_w_up[0], expert_w_down[0]
    nb_p, seq_p, _ = x_prompt.shape
    nb_s, seq_s, _ = x_sample.shape
    rows_p, (gla_p, wkv_p, shift_p) = _mixer(x_prompt, None, None, None, p, tm=256, tc=min(256, seq_p))
    rows_s, (gla_s, wkv_s, shift_s) = _mixer(x_sample, state_gla[0], state_wkv[0], state_shift[0], p,
                                             tm=nb_s * seq_s, tc=seq_s)
    y_p, y_s = _ffn(rows_p, rows_s, p, wg, wu, wd, tm=256, moe_tile=MOE_SUBS_PER_TILE * MOE_SUB, moe_sub=MOE_SUB)
    return (y_p.reshape(nb_p, seq_p, D_MODEL), y_s.reshape(nb_s, seq_s, D_MODEL),
            gla_p[None], wkv_p[None], shift_p[None], gla_s[None], wkv_s[None], shift_s[None])
```

```python
import functools

import jax
import jax.numpy as jnp
from jax import lax
from jax.experimental import pallas as pl
from jax.experimental.pallas import tpu as pltpu

F32 = jnp.float32
BF16 = jnp.bfloat16

D_MODEL = 2048
D_GLA = 1024
D_RWKV = 1024
GLA_HEADS = 4
GLA_DV = 256
GLA_DK = 128
GLA_K = 512
GLA_GATE_TEMP = 16.0
RWKV_HEAD = 64
RWKV_HEADS = 16
N_GROUPS = 8
EXPERTS_PER_GROUP = 8
D_EXPERT = 512
NORM_EPS = 1e-6
GN_EPS = 64e-5
RWKV_OFF = 2 * GLA_K + 2 * D_GLA
D_IN = RWKV_OFF + 3 * D_RWKV

LANE = 128
SUBLANE = 8
MXU_TILE = 256
VMEM_LIMIT = 56 * 1024 * 1024

GLA_CHUNK = 16
WKV_CHUNK = 8
MOE_SUB = 256
MOE_SUBS_PER_TILE = 5
ROUTE_LANES = 128
X1_WIDTH = D_MODEL + ROUTE_LANES
DMA_UNROLL = 8


def _cparams(sem):
    return pltpu.CompilerParams(dimension_semantics=sem, vmem_limit_bytes=VMEM_LIMIT)


def _rms(x, gain):
    return x * lax.rsqrt(jnp.mean(x * x, axis=-1, keepdims=True) + NORM_EPS) * gain


def _softplus(y):
    return jnp.maximum(y, 0.0) + jnp.log1p(jnp.exp(-jnp.abs(y)))


def _dot(a, b):
    return jnp.dot(a.astype(BF16), b.astype(BF16), preferred_element_type=F32)


def _dot_nt(a, b):
    return lax.dot_general(a.astype(BF16), b.astype(BF16), (((1,), (1,)), ((), ())),
                           preferred_element_type=F32)


def _dot_tn(a, b):
    return lax.dot_general(a.astype(BF16), b.astype(BF16), (((0,), (0,)), ((), ())),
                           preferred_element_type=F32)


def _norm_proj_kernel(x_ref, hprev_ref, gain_ref, mu_ref, gw1_ref, gw2_ref, gb_ref,
                      w1_ref, w2_ref, w0_ref, a1_ref, a2_ref, a0_ref, g1_ref, g2_ref,
                      h_ref, la_ref, lw_ref, a_ref, gate_ref, hlast_ref, carry_ref,
                      *, tm, seq, multi_seq):
    i = pl.program_id(0)
    h = _rms(x_ref[...], gain_ref[...])
    row = lax.broadcasted_iota(jnp.int32, (tm, 1), 0)
    rolled = pltpu.roll(h, 1, 0)
    if multi_seq:
        h_shift = jnp.where(row % seq == 0, hprev_ref[...], rolled)
        hlast_ref[...] = h
    else:
        tiles_per_seq = seq // tm
        first = (i % tiles_per_seq) == 0
        prev = jnp.where(first, hprev_ref[0], carry_ref[...])
        h_shift = jnp.where(row == 0, prev, rolled)
        carry_ref[...] = h[tm - 1:tm]
        hlast_ref[0] = h[tm - 1:tm]
    h_ref[...] = h.astype(BF16)

    dx = h_shift - h
    mu = mu_ref[...]
    xw = h + dx * mu[0:1]
    xa = h + dx * mu[1:2]
    xg = h + dx * mu[2:3]

    gl = _dot(_dot(h, gw1_ref[...]), gw2_ref[...]) + gb_ref[...]
    la_ref[...] = -_softplus(-gl) * (1.0 / GLA_GATE_TEMP)

    u = w0_ref[...] + _dot(jnp.tanh(_dot(xw, w1_ref[...])), w2_ref[...])
    w_log = -_softplus(-u) - 0.5
    lw_ref[...] = -jnp.exp(w_log)

    a_ref[...] = jax.nn.sigmoid(a0_ref[...] + _dot(_dot(xa, a1_ref[...]), a2_ref[...]))
    gate_ref[...] = _dot(jax.nn.sigmoid(_dot(xg, g1_ref[...])), g2_ref[...])


def _pad_axis(w, axis, to):
    pad = [(0, 0)] * w.ndim
    pad[axis] = (0, to - w.shape[axis])
    return jnp.pad(w, pad)


def _norm_proj(x2, hprev, p, *, seq, tm):
    m = x2.shape[0]
    multi_seq = tm > seq
    nb = m // seq
    full = lambda shape: pl.BlockSpec(shape, lambda i: (0,) * len(shape))
    rows = lambda w: pl.BlockSpec((tm, w), lambda i: (i, 0))
    if multi_seq:
        assert tm == m
        hprev_spec = rows(D_MODEL)
        hlast_shape = jax.ShapeDtypeStruct((m, D_MODEL), F32)
        hlast_spec = rows(D_MODEL)
    else:
        assert seq % tm == 0
        tps = seq // tm
        hprev_spec = pl.BlockSpec((1, 1, D_MODEL), lambda i: (i // tps, 0, 0))
        hlast_shape = jax.ShapeDtypeStruct((nb, 1, D_MODEL), F32)
        hlast_spec = pl.BlockSpec((1, 1, D_MODEL), lambda i: (i // tps, 0, 0))
    weights = [p['gw1'], p['gw2'], p['gb'], p['w1'], p['w2'], p['w0'], p['a1'], p['a2'], p['a0'],
               p['g1'], p['g2']]
    return pl.pallas_call(
        functools.partial(_norm_proj_kernel, tm=tm, seq=seq, multi_seq=multi_seq),
        grid=(m // tm,),
        in_specs=[rows(D_MODEL), hprev_spec, full((1, D_MODEL)), full((3, D_MODEL))]
                 + [full(w.shape) for w in weights],
        out_specs=[rows(D_MODEL), rows(GLA_K), rows(D_RWKV), rows(D_RWKV), rows(D_RWKV), hlast_spec],
        out_shape=[jax.ShapeDtypeStruct((m, D_MODEL), BF16),
                   jax.ShapeDtypeStruct((m, GLA_K), F32),
                   jax.ShapeDtypeStruct((m, D_RWKV), F32),
                   jax.ShapeDtypeStruct((m, D_RWKV), F32),
                   jax.ShapeDtypeStruct((m, D_RWKV), F32),
                   hlast_shape],
        scratch_shapes=[pltpu.VMEM((1, D_MODEL), F32)],
        compiler_params=_cparams(("arbitrary",)),
        name="norm_proj",
    )(x2, hprev, p['mix_norm'], p['mu_wag'], *weights)


def _in_proj_kernel(h_ref, w_ref, mu_ref, zprev_ref, z_ref, carry_ref, *, tm, seq, multi_seq):
    i = pl.program_id(1)
    z = jnp.dot(h_ref[...], w_ref[...], preferred_element_type=F32)
    row = lax.broadcasted_iota(jnp.int32, (tm, 1), 0)
    rolled = pltpu.roll(z, 1, 0)
    if multi_seq:
        z_prev = jnp.where(row % seq == 0, zprev_ref[...], rolled)
    else:
        tiles_per_seq = seq // tm
        first = (i % tiles_per_seq) == 0
        prev = jnp.where(first, zprev_ref[0], carry_ref[...])
        z_prev = jnp.where(row == 0, prev, rolled)
        carry_ref[...] = z[tm - 1:tm]
    z_ref[...] = z + mu_ref[...] * (z_prev - z)


def _in_proj(h_bf, w_in, mu_ext, zprev, *, seq, tm, tn):
    m = h_bf.shape[0]
    multi_seq = tm > seq
    if multi_seq:
        zprev_spec = pl.BlockSpec((tm, tn), lambda j, i: (i, j))
    else:
        tps = seq // tm
        zprev_spec = pl.BlockSpec((1, 1, tn), lambda j, i: (i // tps, 0, j))
    return pl.pallas_call(
        functools.partial(_in_proj_kernel, tm=tm, seq=seq, multi_seq=multi_seq),
        grid=(D_IN // tn, m // tm),
        in_specs=[pl.BlockSpec((tm, D_MODEL), lambda j, i: (i, 0)),
                  pl.BlockSpec((D_MODEL, tn), lambda j, i: (0, j)),
                  pl.BlockSpec((1, tn), lambda j, i: (0, j)),
                  zprev_spec],
        out_specs=pl.BlockSpec((tm, tn), lambda j, i: (i, j)),
        out_shape=jax.ShapeDtypeStruct((m, D_IN), F32),
        scratch_shapes=[pltpu.VMEM((1, tn), F32)],
        compiler_params=_cparams(("arbitrary", "arbitrary")),
        name="in_proj",
    )(h_bf, w_in, mu_ext, zprev)


def _plain_proj_kernel(h_ref, w_ref, z_ref):
    z_ref[...] = jnp.dot(h_ref[...], w_ref[...], preferred_element_type=F32)


def _plain_proj(h_bf, w_in, *, tn):
    m = h_bf.shape[0]
    return pl.pallas_call(
        _plain_proj_kernel,
        grid=(D_IN // tn,),
        in_specs=[pl.BlockSpec((m, D_MODEL), lambda j: (0, 0)),
                  pl.BlockSpec((D_MODEL, tn), lambda j: (0, j))],
        out_specs=pl.BlockSpec((m, tn), lambda j: (0, j)),
        out_shape=jax.ShapeDtypeStruct((m, D_IN), F32),
        compiler_params=_cparams(("arbitrary",)),
        name="prev_row_proj",
    )(h_bf, w_in)


def _gla_kernel(q_ref, k_ref, v_ref, g_ref, la_ref, s0_ref, e_ref, gn_ref,
                o_ref, s_ref, st_ref, obuf_ref, *, tc, chunk, nseq):
    tb = pl.program_id(1)
    c = chunk

    @pl.when(tb == 0)
    def _():
        for q in range(nseq):
            for hh in range(GLA_HEADS):
                st_ref[q, :, hh * GLA_DK:(hh + 1) * GLA_DK] = s0_ref[q, hh].T

    row = lax.broadcasted_iota(jnp.int32, (c, 1), 0)
    e_mat = e_ref[...]
    neg = jnp.float32(-1e30)

    def one_chunk(q, r0):
        qq = q_ref[q, pl.ds(r0, c), :] * (GLA_DK ** -0.5)
        k = k_ref[q, pl.ds(r0, c), :]
        v = v_ref[q, pl.ds(r0, c), :]
        la = la_ref[q, pl.ds(r0, c), :]
        b = jnp.zeros_like(la)
        for s in range(c):
            b = b + jnp.where(row >= s, la[s:s + 1], 0.0)
        b_last = b[c - 1:c]

        parts = []
        for s in range(c):
            dec = jnp.exp(jnp.where(row >= s, b - b[s:s + 1], neg))
            parts.append((qq * k[s:s + 1] * dec).astype(BF16))
        pcat = jnp.concatenate(parts, axis=0)
        sc = jnp.concatenate(
            [jnp.dot(pcat[:, hp * MXU_TILE:(hp + 1) * MXU_TILE], e_mat, preferred_element_type=F32)
             for hp in range(GLA_K // MXU_TILE)], axis=1)
        o = jnp.zeros((c, D_GLA), F32)
        for s in range(c):
            o = o + sc[s * c:(s + 1) * c] * v[s:s + 1]

        qd = qq * jnp.exp(b)
        kd = k * jnp.exp(b_last - b)
        a_last = jnp.exp(b_last)
        o_heads = []
        for hh in range(GLA_HEADS):
            dk = slice(hh * GLA_DK, (hh + 1) * GLA_DK)
            dv = slice(hh * GLA_DV, (hh + 1) * GLA_DV)
            st = st_ref[q, :, dk]
            oh = o[:, dv] + _dot_nt(qd[:, dk], st)
            st_ref[q, :, dk] = a_last[:, dk] * st + _dot_tn(v[:, dv], kd[:, dk])
            oh = oh * lax.rsqrt(jnp.mean(oh * oh, axis=-1, keepdims=True) + NORM_EPS)
            o_heads.append(oh)
        og = jnp.concatenate(o_heads, axis=1)
        g = g_ref[q, pl.ds(r0, c), :]
        obuf_ref[q, pl.ds(r0, c), :] = og * gn_ref[...] * (g * jax.nn.sigmoid(g))

    def body(ci, carry):
        r0 = pl.multiple_of(ci * c, c)
        for q in range(nseq):
            one_chunk(q, r0)
        return carry

    lax.fori_loop(0, tc // c, body, 0)
    o_ref[...] = obuf_ref[...].astype(BF16)

    @pl.when(tb == pl.num_programs(1) - 1)
    def _():
        for q in range(nseq):
            for hh in range(GLA_HEADS):
                s_ref[q, hh] = st_ref[q, :, hh * GLA_DK:(hh + 1) * GLA_DK].T


def _gla(z3, la3, s0, e_mat, gla_norm, *, tc, chunk, nseq):
    nb, seq, _ = z3.shape
    blk = lambda w, col: pl.BlockSpec((nseq, tc, w), lambda b, t: (b, t, col))
    st_spec = pl.BlockSpec((nseq, GLA_HEADS, GLA_DK, GLA_DV), lambda b, t: (b, 0, 0, 0))
    return pl.pallas_call(
        functools.partial(_gla_kernel, tc=tc, chunk=chunk, nseq=nseq),
        grid=(nb // nseq, seq // tc),
        in_specs=[blk(GLA_K, 0), blk(GLA_K, 1), blk(D_GLA, 1), blk(D_GLA, 2), blk(GLA_K, 0),
                  st_spec,
                  pl.BlockSpec(e_mat.shape, lambda b, t: (0, 0)),
                  pl.BlockSpec((1, D_GLA), lambda b, t: (0, 0))],
        out_specs=[blk(D_GLA, 0), st_spec],
        out_shape=[jax.ShapeDtypeStruct((nb, seq, D_GLA), BF16),
                   jax.ShapeDtypeStruct((nb, GLA_HEADS, GLA_DK, GLA_DV), F32)],
        scratch_shapes=[pltpu.VMEM((nseq, GLA_DV, GLA_K), F32), pltpu.VMEM((nseq, tc, D_GLA), F32)],
        compiler_params=_cparams(("arbitrary", "arbitrary")),
        name="gla",
    )(z3, z3, z3, z3, la3, s0, e_mat, gla_norm)


N_LANE_TILES = D_RWKV // MXU_TILE
HEADS_PER_TILE = MXU_TILE // RWKV_HEAD


def _stack_tiles(x):
    return jnp.concatenate([x[:, t * MXU_TILE:(t + 1) * MXU_TILE] for t in range(N_LANE_TILES)], axis=0)


def _unstack_tiles(y, r):
    return jnp.concatenate([y[t * r:(t + 1) * r] for t in range(N_LANE_TILES)], axis=1)


def _seg_sums(xs, bd):
    c = xs[0].shape[0]
    st = jnp.concatenate([_stack_tiles(x) for x in xs], axis=0)
    hi = st.astype(BF16)
    lo = (st - hi.astype(F32)).astype(BF16)
    res = jnp.dot(jnp.concatenate([hi, lo], axis=0), bd, preferred_element_type=F32)
    half = st.shape[0]
    res = res[:half] + res[half:]
    rows = N_LANE_TILES * c
    return [_unstack_tiles(res[n * rows:(n + 1) * rows], c) for n in range(len(xs))]


def _cumsum_rows(x, row, c):
    if c == SUBLANE:
        step = 1
        while step < c:
            x = x + jnp.where(row >= step, pltpu.roll(x, step, 0), 0.0)
            step *= 2
        return x
    out = jnp.zeros_like(x)
    for s in range(c):
        out = out + jnp.where(row >= s, x[s:s + 1], 0.0)
    return out


def _wkv_kernel(r_ref, k_ref, v_ref, lw_ref, a_ref, gate_ref, s0_ref, bd_ref,
                kk_ref, ka_ref, rk_ref, lnw_ref, lnb_ref,
                o_ref, s_ref, st_ref, obuf_ref, *, tc, chunk, nseq):
    tb = pl.program_id(1)
    c = chunk
    cc = c * c

    @pl.when(tb == 0)
    def _():
        for q in range(nseq):
            st_ref[q] = jnp.concatenate([s0_ref[q, hh] for hh in range(RWKV_HEADS)], axis=1)

    row = lax.broadcasted_iota(jnp.int32, (c, 1), 0)
    bd = bd_ref[...]
    lane_head = lax.broadcasted_iota(jnp.int32, (RWKV_HEAD, MXU_TILE), 1) // RWKV_HEAD
    neg = jnp.float32(-1e30)

    def one_chunk(q, r0):
        r = r_ref[q, pl.ds(r0, c), :]
        k = k_ref[q, pl.ds(r0, c), :]
        v = v_ref[q, pl.ds(r0, c), :]
        lw = lw_ref[q, pl.ds(r0, c), :]
        a_sig = a_ref[q, pl.ds(r0, c), :]

        kk = k * kk_ref[...]
        k_eff = k * (1.0 + (a_sig - 1.0) * ka_ref[...])
        kk_ss, bonus_dot = _seg_sums([kk * kk, r * k_eff * rk_ref[...]], bd)
        kk = kk / jnp.maximum(jnp.sqrt(kk_ss), 1e-12)
        a_vec = -kk
        b_vec = kk * a_sig

        lb = _cumsum_rows(lw, row, c)
        lbp = lb - lw
        lb_last = lb[c - 1:c]
        kinds = ([], [], [], [])
        for s in range(c):
            lbs = lb[s:s + 1]
            bs = b_vec[s:s + 1]
            ks = k_eff[s:s + 1]
            a_d = a_vec * jnp.exp(jnp.where(row > s, lbp - lbs, neg))
            r_d = r * jnp.exp(jnp.where(row >= s, lb - lbs, neg))
            kinds[0].append(a_d * bs)
            kinds[1].append(a_d * ks)
            kinds[2].append(r_d * bs)
            kinds[3].append(r_d * ks)
        prod = jnp.concatenate(
            [p[:, t * MXU_TILE:(t + 1) * MXU_TILE]
             for t in range(N_LANE_TILES) for kind in kinds for p in kind], axis=0)
        coef_t = jnp.dot(prod.astype(BF16), bd, preferred_element_type=F32)

        def coef(kind, s):
            return jnp.concatenate(
                [coef_t[(t * 4 + kind) * cc + s * c:(t * 4 + kind) * cc + (s + 1) * c]
                 for t in range(N_LANE_TILES)], axis=1)

        lhs = jnp.concatenate([a_vec * jnp.exp(lbp), r * jnp.exp(lb)], axis=0).astype(BF16)
        x0 = []
        for t in range(N_LANE_TILES):
            ln = slice(t * MXU_TILE, (t + 1) * MXU_TILE)
            st_t = st_ref[q, :, ln].astype(BF16)
            w_t = jnp.concatenate([st_t] * HEADS_PER_TILE, axis=0) * bd
            x0.append(lax.dot_general(lhs[:, ln], w_t, (((1,), (1,)), ((), ())),
                                      preferred_element_type=F32))
        x0 = jnp.concatenate(x0, axis=1)
        u = x0[:c]
        o = x0[c:]

        for s in range(c):
            u = u + coef(1, s) * v[s:s + 1]
        for s in range(c):
            u = u + coef(0, s) * u[s:s + 1]
        for s in range(c):
            o = o + coef(2, s) * u[s:s + 1] + coef(3, s) * v[s:s + 1]

        tail = jnp.exp(lb_last - lb)
        uv = jnp.concatenate([u, v], axis=0).astype(BF16)
        bk = jnp.concatenate([b_vec * tail, k_eff * tail], axis=0).astype(BF16)
        w_last = jnp.exp(lb_last)
        for t in range(N_LANE_TILES):
            ln = slice(t * MXU_TILE, (t + 1) * MXU_TILE)
            zt = _dot_tn(uv[:, ln], bk[:, ln])
            upd = zt[(HEADS_PER_TILE - 1) * RWKV_HEAD:]
            for hh in range(HEADS_PER_TILE - 2, -1, -1):
                upd = jnp.where(lane_head == hh, zt[hh * RWKV_HEAD:(hh + 1) * RWKV_HEAD], upd)
            st_ref[q, :, ln] = w_last[:, ln] * st_ref[q, :, ln] + upd

        inv_n = 1.0 / RWKV_HEAD
        mu = _seg_sums([o], bd)[0] * inv_n
        dev = o - mu
        var = _seg_sums([dev * dev], bd)[0] * inv_n
        yn = dev * lax.rsqrt(var + GN_EPS) * lnw_ref[...] + lnb_ref[...]
        obuf_ref[q, pl.ds(r0, c), :] = (yn + bonus_dot * v) * gate_ref[q, pl.ds(r0, c), :]

    def body(ci, carry):
        r0 = pl.multiple_of(ci * c, c)
        for q in range(nseq):
            one_chunk(q, r0)
        return carry

    lax.fori_loop(0, tc // c, body, 0)
    o_ref[...] = obuf_ref[...].astype(BF16)

    @pl.when(tb == pl.num_programs(1) - 1)
    def _():
        for q in range(nseq):
            for hh in range(RWKV_HEADS):
                s_ref[q, hh] = st_ref[q, :, hh * RWKV_HEAD:(hh + 1) * RWKV_HEAD]


def _wkv(z3, lw3, a3, gate3, s0, bd, p, *, tc, chunk, nseq):
    nb, seq, _ = z3.shape
    col0 = RWKV_OFF // D_RWKV
    blk = lambda col: pl.BlockSpec((nseq, tc, D_RWKV), lambda b, t: (b, t, col))
    st_spec = pl.BlockSpec((nseq, RWKV_HEADS, RWKV_HEAD, RWKV_HEAD), lambda b, t: (b, 0, 0, 0))
    vec = pl.BlockSpec((1, D_RWKV), lambda b, t: (0, 0))
    return pl.pallas_call(
        functools.partial(_wkv_kernel, tc=tc, chunk=chunk, nseq=nseq),
        grid=(nb // nseq, seq // tc),
        in_specs=[blk(col0), blk(col0 + 1), blk(col0 + 2), blk(0), blk(0), blk(0), st_spec,
                  pl.BlockSpec(bd.shape, lambda b, t: (0, 0)), vec, vec, vec, vec, vec],
        out_specs=[blk(0), st_spec],
        out_shape=[jax.ShapeDtypeStruct((nb, seq, D_RWKV), BF16),
                   jax.ShapeDtypeStruct((nb, RWKV_HEADS, RWKV_HEAD, RWKV_HEAD), F32)],
        scratch_shapes=[pltpu.VMEM((nseq, RWKV_HEAD, D_RWKV), F32),
                        pltpu.VMEM((nseq, tc, D_RWKV), F32)],
        compiler_params=_cparams(("arbitrary", "arbitrary")),
        name="wkv",
    )(z3, z3, z3, lw3, a3, gate3, s0, bd, p['k_k'], p['k_a'], p['r_k'], p['ln_w'], p['ln_b'])


def _out_router_kernel(xp_ref, gp_ref, rp_ref, xs_ref, gs_ref, rs_ref, wo_ref, fg_ref, rw_ref, rb_ref,
                       out_ref, *, tm, tiles_p):
    is_p = pl.program_id(0) < tiles_p
    x = jnp.where(is_p, xp_ref[...], xs_ref[...])
    og = jnp.where(is_p, gp_ref[...], gs_ref[...])
    orr = jnp.where(is_p, rp_ref[...], rs_ref[...])
    mix = (jnp.dot(og, wo_ref[0], preferred_element_type=F32)
           + jnp.dot(orr, wo_ref[1], preferred_element_type=F32))
    x1 = x + mix
    out_ref[:, :D_MODEL] = x1
    h2 = _rms(x1, fg_ref[...])
    logits = _dot(h2, rw_ref[...]) + rb_ref[...]
    lane = lax.broadcasted_iota(jnp.int32, (tm, ROUTE_LANES), 1)
    neg = jnp.float32(-jnp.inf)
    big = jnp.int32(1 << 20)

    def first_argmax(vals):
        mx = jnp.max(vals, axis=-1, keepdims=True)
        idx = jnp.min(jnp.where(vals == mx, lane, big), axis=-1, keepdims=True)
        return mx, idx

    lg = jnp.where(lane < N_GROUPS, logits, neg)
    g_max, g_idx = first_argmax(lg)
    p_group = 1.0 / jnp.sum(jnp.exp(lg - g_max), axis=-1, keepdims=True)
    lo = N_GROUPS + g_idx * EXPERTS_PER_GROUP
    le = jnp.where((lane >= lo) & (lane < lo + EXPERTS_PER_GROUP), logits, neg)
    m1, i1 = first_argmax(le)
    m2, i2 = first_argmax(jnp.where(lane == i1, neg, le))
    e2 = jnp.exp(m2 - m1)
    w1 = 1.0 / (1.0 + e2)
    w2 = e2 / (1.0 + e2)
    route = jnp.where(lane == i1, w1, jnp.where(lane == i2, w2, 0.0)) * p_group
    out_ref[:, D_MODEL:] = jnp.where(lane == ROUTE_LANES - 1, g_idx.astype(F32), route)


def _out_router(xp, gp, rp, xs, gs, rs, wo, ffn_norm, rw, rb, *, tm):
    mp, ms = xp.shape[0], xs.shape[0]
    tiles_p, tiles_s = mp // tm, ms // tm
    rows_p = lambda w: pl.BlockSpec((tm, w), lambda i: (jnp.minimum(i, tiles_p - 1), 0))
    rows_s = lambda w: pl.BlockSpec((tm, w), lambda i: (jnp.maximum(i - tiles_p, 0), 0))
    full = lambda shape: pl.BlockSpec(shape, lambda i: (0,) * len(shape))
    return pl.pallas_call(
        functools.partial(_out_router_kernel, tm=tm, tiles_p=tiles_p),
        grid=(tiles_p + tiles_s,),
        in_specs=[rows_p(D_MODEL), rows_p(D_GLA), rows_p(D_RWKV),
                  rows_s(D_MODEL), rows_s(D_GLA), rows_s(D_RWKV),
                  full(wo.shape), full((1, D_MODEL)), full(rw.shape), full((1, ROUTE_LANES))],
        out_specs=pl.BlockSpec((tm, X1_WIDTH), lambda i: (i, 0)),
        out_shape=jax.ShapeDtypeStruct((mp + ms, X1_WIDTH), F32),
        compiler_params=_cparams(("arbitrary",)),
        name="out_router",
    )(xp, gp, rp, xs, gs, rs, wo, ffn_norm, rw, rb)


def _moe_kernel(tile_group_ref, tile_rows_ref, tile_pos_ref, order_ref,
                x1_hbm, wg_ref, wu_ref, wd_ref, fg_ref, fn_ref, yp_hbm, ys_hbm,
                acc, hbuf, wgb, wub, wdb, gsem, ssem, *, sub, rows_p, n_tokens):
    j = pl.program_id(0)
    e = pl.program_id(1)
    nrows = tile_rows_ref[j]
    pos0 = tile_pos_ref[j]
    half = sub // 2
    rem = nrows % sub
    has_small = (rem > 0) & (rem <= half)
    n_big = nrows // sub + jnp.where(rem > half, 1, 0)
    rows_used = n_big * sub + jnp.where(has_small, half, 0)

    def for_blocks(fn):
        def big(sb, cy):
            fn(pl.ds(pl.multiple_of(sb * sub, sub), sub), sub)
            return cy
        lax.fori_loop(0, n_big, big, 0)

        @pl.when(has_small)
        def _():
            fn(pl.ds(pl.multiple_of(n_big * sub, half), half), half)

    def token(rr):
        return order_ref[jnp.minimum(pos0 + rr, n_tokens - 1)]

    def gather_copy(rr):
        return pltpu.make_async_copy(x1_hbm.at[pl.ds(token(rr), 1)], acc.at[pl.ds(rr, 1)], gsem)

    def scatter_start(rr, lane=0):
        tok = token(rr)
        row = acc.at[pl.ds(rr, 1), pl.ds(0, D_MODEL)]

        @pl.when(tok < rows_p)
        def _():
            pltpu.make_async_copy(row, yp_hbm.at[pl.ds(tok, 1)], ssem).start(priority=lane)

        @pl.when(tok >= rows_p)
        def _():
            pltpu.make_async_copy(row, ys_hbm.at[pl.ds(tok - rows_p, 1)], ssem).start(priority=lane)

    def scatter_wait(rr, lane=0):
        pltpu.make_async_copy(acc.at[pl.ds(rr, 1), pl.ds(0, D_MODEL)], yp_hbm.at[pl.ds(0, 1)], ssem).wait()

    def for_rows(n_blocks, fn):
        def blk(bi, cy):
            for q in range(DMA_UNROLL):
                fn(bi * DMA_UNROLL + q, q % 2)
            return cy
        lax.fori_loop(0, n_blocks, blk, 0)

    @pl.when((e == 0) & (nrows > 0))
    def _():
        n_blk = rows_used // DMA_UNROLL
        for_rows(n_blk, lambda rr, lane: gather_copy(rr).start(priority=lane))
        for_rows(n_blk, lambda rr, lane: gather_copy(rr).wait())

        def norm(rs, size):
            hbuf[rs, :] = _rms(acc[rs, :D_MODEL], fg_ref[...]).astype(BF16)
        for_blocks(norm)

    @pl.when(nrows > 0)
    def _():
        g = tile_group_ref[j]
        wgb[...] = wg_ref[0, 0].astype(BF16)
        wub[...] = wu_ref[0, 0].astype(BF16)
        wdb[...] = wd_ref[0, 0].astype(BF16)

        def expert(rs, size):
            lane = lax.broadcasted_iota(jnp.int32, (size, ROUTE_LANES), 1)
            sel = lane == N_GROUPS + g * EXPERTS_PER_GROUP + e
            wcol = jnp.sum(jnp.where(sel, acc[rs, D_MODEL:], 0.0), axis=-1, keepdims=True)
            h2 = hbuf[rs, :]
            gate = jnp.dot(h2, wgb[...], preferred_element_type=F32)
            up = jnp.dot(h2, wub[...], preferred_element_type=F32)
            hid = (gate * jax.nn.sigmoid(gate)) * up * wcol
            acc[rs, :D_MODEL] += jnp.dot(hid.astype(BF16), wdb[...], preferred_element_type=F32)
        for_blocks(expert)

    @pl.when((e == EXPERTS_PER_GROUP - 1) & (nrows > 0))
    def _():
        def final(rs, size):
            acc[rs, :D_MODEL] = _rms(acc[rs, :D_MODEL], fn_ref[...])
        for_blocks(final)

        n_full = nrows // DMA_UNROLL
        for_rows(n_full, scatter_start)

        def rest_start(rr, cy):
            scatter_start(rr)
            return cy
        lax.fori_loop(n_full * DMA_UNROLL, nrows, rest_start, 0)
        for_rows(n_full, scatter_wait)

        def rest_wait(rr, cy):
            scatter_wait(rr)
            return cy
        lax.fori_loop(n_full * DMA_UNROLL, nrows, rest_wait, 0)


def _moe(x1e, tile_group, tile_rows, tile_pos, order, wg, wu, wd, ffn_norm, final_norm, *, tile, sub, rows_p):
    m = x1e.shape[0]
    n_tiles = tile_group.shape[0]

    def w_idx(j, e, tg, tr, tp, od):
        return (tg[j], jnp.where(tr[j] > 0, e, EXPERTS_PER_GROUP - 1), 0, 0)

    vec = pl.BlockSpec((1, D_MODEL), lambda j, e, tg, tr, tp, od: (0, 0))
    grid_spec = pltpu.PrefetchScalarGridSpec(
        num_scalar_prefetch=4,
        grid=(n_tiles, EXPERTS_PER_GROUP),
        in_specs=[pl.BlockSpec(memory_space=pl.ANY),
                  pl.BlockSpec((1, 1, D_MODEL, D_EXPERT), w_idx),
                  pl.BlockSpec((1, 1, D_MODEL, D_EXPERT), w_idx),
                  pl.BlockSpec((1, 1, D_EXPERT, D_MODEL), w_idx),
                  vec, vec],
        out_specs=[pl.BlockSpec(memory_space=pl.ANY), pl.BlockSpec(memory_space=pl.ANY)],
        scratch_shapes=[pltpu.VMEM((tile, X1_WIDTH), F32),
                        pltpu.VMEM((tile, D_MODEL), BF16),
                        pltpu.VMEM((D_MODEL, D_EXPERT), BF16),
                        pltpu.VMEM((D_MODEL, D_EXPERT), BF16),
                        pltpu.VMEM((D_EXPERT, D_MODEL), BF16),
                        pltpu.SemaphoreType.DMA(()),
                        pltpu.SemaphoreType.DMA(())],
    )
    return pl.pallas_call(
        functools.partial(_moe_kernel, sub=sub, rows_p=rows_p, n_tokens=m),
        grid_spec=grid_spec,
        out_shape=[jax.ShapeDtypeStruct((rows_p, D_MODEL), F32),
                   jax.ShapeDtypeStruct((m - rows_p, D_MODEL), F32)],
        compiler_params=_cparams(("arbitrary", "arbitrary")),
        name="moe",
    )(tile_group, tile_rows, tile_pos, order, x1e, wg, wu, wd, ffn_norm, final_norm)


def _moe_plan(group_id, tile):
    m = group_id.shape[0]
    n_tiles = m // tile + N_GROUPS
    order = jnp.argsort(group_id, stable=True).astype(jnp.int32)
    counts = jnp.sum(group_id[:, None] == jnp.arange(N_GROUPS, dtype=jnp.int32)[None, :], axis=0).astype(jnp.int32)
    tiles_per = (counts + tile - 1) // tile
    tile_end = jnp.cumsum(tiles_per)
    tile_start = tile_end - tiles_per
    row_start = jnp.cumsum(counts) - counts
    t = jnp.arange(n_tiles, dtype=jnp.int32)
    used = t < tile_end[-1]
    grp = jnp.minimum(jnp.sum(t[:, None] >= tile_end[None, :], axis=1), N_GROUPS - 1).astype(jnp.int32)
    last_group = jnp.max(jnp.where(counts > 0, jnp.arange(N_GROUPS, dtype=jnp.int32), 0))
    tile_group = jnp.where(used, grp, last_group).astype(jnp.int32)
    local = (t - tile_start[grp]) * tile
    tile_rows = jnp.where(used, jnp.clip(counts[grp] - local, 0, tile), 0).astype(jnp.int32)
    tile_pos = jnp.where(used, row_start[grp] + local, 0).astype(jnp.int32)
    return tile_group, tile_rows, tile_pos, order


def _prep_params(mix_norm, w_in, gla_gate_w1, gla_gate_w2, gla_gate_b, gla_norm, rwkv_mu_rkv, rwkv_mu_wag,
                 rwkv_w0, rwkv_w1, rwkv_w2, rwkv_a0, rwkv_a1, rwkv_a2, rwkv_g1, rwkv_g2,
                 rwkv_k_k, rwkv_k_a, rwkv_r_k, rwkv_ln_w, rwkv_ln_b, w_out, ffn_norm,
                 router_group_w, router_group_b, router_expert_w, router_expert_b, final_norm):
    row = lambda v: v.reshape(1, -1).astype(F32)
    g_rank = 2 * LANE
    p = dict(
        mix_norm=row(mix_norm[0]),
        mu_wag=rwkv_mu_wag[0],
        w_in=w_in[0].astype(BF16),
        gw1=_pad_axis(gla_gate_w1[0], 1, LANE).astype(BF16),
        gw2=_pad_axis(gla_gate_w2[0], 0, LANE).astype(BF16),
        gb=row(gla_gate_b[0]),
        w1=_pad_axis(rwkv_w1[0], 1, LANE).astype(BF16),
        w2=_pad_axis(rwkv_w2[0], 0, LANE).astype(BF16),
        w0=row(rwkv_w0[0]),
        a1=_pad_axis(rwkv_a1[0], 1, LANE).astype(BF16),
        a2=_pad_axis(rwkv_a2[0], 0, LANE).astype(BF16),
        a0=row(rwkv_a0[0]),
        g1=_pad_axis(rwkv_g1[0], 1, g_rank).astype(BF16),
        g2=_pad_axis(rwkv_g2[0], 0, g_rank).astype(BF16),
        gla_norm=row(gla_norm[0]),
        mu_ext=jnp.concatenate([jnp.zeros((1, RWKV_OFF), F32), rwkv_mu_rkv[0].reshape(1, -1)], axis=1),
        k_k=row(rwkv_k_k[0]), k_a=row(rwkv_k_a[0]), r_k=row(rwkv_r_k[0]),
        ln_w=row(rwkv_ln_w[0]), ln_b=row(rwkv_ln_b[0]),
        wo=w_out[0].astype(BF16).reshape(2, D_GLA, D_MODEL),
        ffn_norm=row(ffn_norm[0]),
        router_w=_pad_axis(jnp.concatenate([router_group_w[0], router_expert_w[0]], axis=1), 1,
                           ROUTE_LANES).astype(BF16),
        router_b=_pad_axis(jnp.concatenate([router_group_b[0], router_expert_b[0]]).reshape(1, -1), 1,
                           ROUTE_LANES).astype(F32),
        final_norm=row(final_norm),
    )
    r = jnp.arange(MXU_TILE)
    p['gla_e'] = (r[:, None] // GLA_DK == jnp.arange(2 * GLA_DV)[None, :] // GLA_DV).astype(BF16)
    p['wkv_bd'] = (r[:, None] // RWKV_HEAD == r[None, :] // RWKV_HEAD).astype(BF16)
    return p


def _mixer(x, s_gla, s_wkv, s_shift, p, *, tm, tc):
    nb, seq, _ = x.shape
    m = nb * seq
    x2 = x.reshape(m, D_MODEL)
    multi_seq = tm > seq

    h_last = jnp.zeros((nb, D_MODEL), F32) if s_shift is None else s_shift
    zl = _plain_proj(_pad_axis(h_last, 0, -(-nb // 16) * 16).astype(BF16), p['w_in'], tn=1024)[:nb]
    if multi_seq:
        expand = lambda a: jnp.repeat(a, seq, axis=0)
        hprev, zprev = expand(h_last), expand(zl)
    else:
        hprev, zprev = h_last.reshape(nb, 1, D_MODEL), zl.reshape(nb, 1, D_IN)

    h_bf, la, lw, a_sig, gate, hlast = _norm_proj(x2, hprev, p, seq=seq, tm=tm)
    shift_new = hlast.reshape(nb, seq, D_MODEL)[:, -1] if multi_seq else hlast.reshape(nb, D_MODEL)
    z = _in_proj(h_bf, p['w_in'], p['mu_ext'], zprev, seq=seq, tm=tm if multi_seq else min(seq, 1024), tn=1024)

    z3 = z.reshape(nb, seq, D_IN)
    r3 = lambda a: a.reshape(nb, seq, a.shape[-1])
    if s_gla is None:
        s_gla = jnp.zeros((nb, GLA_HEADS, GLA_DK, GLA_DV), F32)
        s_wkv = jnp.zeros((nb, RWKV_HEADS, RWKV_HEAD, RWKV_HEAD), F32)
    seqs = 4 if nb % 4 == 0 else 1
    tc_seqs = min(tc, 512 // seqs)
    og, gla_new = _gla(z3, r3(la), s_gla, p['gla_e'], p['gla_norm'], tc=tc_seqs, chunk=min(GLA_CHUNK, seq),
                       nseq=seqs)
    orr, wkv_new = _wkv(z3, r3(lw), r3(a_sig), r3(gate), s_wkv, p['wkv_bd'], p, tc=tc_seqs,
                        chunk=min(WKV_CHUNK, seq), nseq=seqs)
    return (x2, og.reshape(m, D_GLA), orr.reshape(m, D_RWKV)), (gla_new, wkv_new, shift_new)


def _ffn(rows_p, rows_s, p, wg, wu, wd, *, tm, moe_tile, moe_sub):
    mp = rows_p[0].shape[0]
    x1e = _out_router(*rows_p, *rows_s, p['wo'], p['ffn_norm'], p['router_w'], p['router_b'], tm=tm)
    group_id = x1e[:, X1_WIDTH - 1].astype(jnp.int32)
    tile_group, tile_rows, tile_pos, order = _moe_plan(group_id, moe_tile)
    return _moe(x1e, tile_group, tile_rows, tile_pos, order, wg, wu, wd, p['ffn_norm'], p['final_norm'],
                tile=moe_tile, sub=moe_sub, rows_p=mp)


def kernel(x_prompt, x_sample, state_gla, state_wkv, state_shift, mix_norm, w_in, gla_gate_w1, gla_gate_w2, gla_gate_b, gla_norm, rwkv_mu_rkv, rwkv_mu_wag, rwkv_w0, rwkv_w1, rwkv_w2, rwkv_a0, rwkv_a1, rwkv_a2, rwkv_g1, rwkv_g2, rwkv_k_k, rwkv_k_a, rwkv_r_k, rwkv_ln_w, rwkv_ln_b, w_out, ffn_norm, router_group_w, router_group_b, router_expert_w, router_expert_b, expert_w_gate, expert_w_up, expert_w_down, final_norm):
    p = _prep_params(mix_norm, w_in, gla_gate_w1, gla_gate_w2, gla_gate_b, gla_norm, rwkv_mu_rkv, rwkv_mu_wag,
                     rwkv_w0, rwkv_w1, rwkv_w2, rwkv_a0, rwkv_a1, rwkv_a2, rwkv_g1, rwkv_g2,
                     rwkv_k_k, rwkv_k_a, rwkv_r_k, rwkv_ln_w, rwkv_ln_b, w_out, ffn_norm,
                     router_group_w, router_group_b, router_expert_w, router_expert_b, final_norm)
    wg, wu, wd = expert_w_gate[0], expert_w_up[0], expert_w_down[0]
    nb_p, seq_p, _ = x_prompt.shape
    nb_s, seq_s, _ = x_sample.shape
    rows_p, (gla_p, wkv_p, shift_p) = _mixer(x_prompt, None, None, None, p, tm=256, tc=min(256, seq_p))
    rows_s, (gla_s, wkv_s, shift_s) = _mixer(x_sample, state_gla[0], state_wkv[0], state_shift[0], p,
                                             tm=nb_s * seq_s, tc=seq_s)
    y_p, y_s = _ffn(rows_p, rows_s, p, wg, wu, wd, tm=256, moe_tile=MOE_SUBS_PER_TILE * MOE_SUB, moe_sub=MOE_SUB)
    return (y_p.reshape(nb_p, seq_p, D_MODEL), y_s.reshape(nb_s, seq_s, D_MODEL),
            gla_p[None], wkv_p[None], shift_p[None], gla_s[None], wkv_s[None], shift_s[None])
```
